```python
import math
import jax, jax.numpy as jnp
from jax import lax
import numpy as np

D_MODEL = 2048
BATCH = 4
SEQ = 4096
DEPTH = 4

HEAD_DIM = 64
N_MIX_HEADS = D_MODEL // HEAD_DIM
A_Q_HEADS = N_MIX_HEADS // 2
A_KV_HEADS = A_Q_HEADS // 4
A_WINDOW = 128
B_Q_HEADS = N_MIX_HEADS - A_Q_HEADS
B_KV_HEADS = 2
NSA_CMP_LEN = 32
NSA_CMP_STRIDE = 16
NSA_CMP_HIDDEN = 4 * HEAD_DIM
NSA_SEL_LEN = 64
NSA_TOP_N = 16
NSA_WINDOW = 512
NSA_FORCE_SCORE = 1.0e4
C_HEADS = N_MIX_HEADS
D_FF = 4 * D_MODEL
REL_BUCKETS = 32
REL_MAX_DIST = 1024
REL_HEADS = A_Q_HEADS + B_Q_HEADS
Q_BLOCK = 128
RMS_EPS = 1e-6
ATTN_SCALE = HEAD_DIM ** -0.5
EVEN_SIZES = (A_Q_HEADS * HEAD_DIM, A_KV_HEADS * HEAD_DIM, A_KV_HEADS * HEAD_DIM, B_Q_HEADS * HEAD_DIM) + (B_KV_HEADS * HEAD_DIM,) * 6 + (3 * B_Q_HEADS,)
EVEN_IN = sum(EVEN_SIZES)
EVEN_MIX = (A_Q_HEADS + B_Q_HEADS) * HEAD_DIM
C_MIX = C_HEADS * HEAD_DIM
ODD_IN = 3 * C_MIX + C_HEADS

kernel_name = 'hybrid_swa_nsa_fox_trunk'


def rms_norm(x, g):
    xf = x.astype(jnp.float32)
    y = xf * lax.rsqrt(jnp.mean(xf * xf, axis=-1, keepdims=True) + RMS_EPS)
    return (y * g.astype(jnp.float32)).astype(x.dtype)


def rel_bucket(dist):
    n = jnp.maximum(dist, 0)
    max_exact = REL_BUCKETS // 2
    nf = jnp.maximum(n, 1).astype(jnp.float32)
    large = max_exact + (jnp.log(nf / max_exact) / math.log(REL_MAX_DIST / max_exact) * (REL_BUCKETS - max_exact)).astype(jnp.int32)
    return jnp.where(n < max_exact, n, jnp.minimum(large, REL_BUCKETS - 1))


def head_bias(tab, dist, g, r):
    b = tab[rel_bucket(dist)]
    return jnp.moveaxis(b, -1, 0).reshape(g, r, *dist.shape)


def masked_softmax(s, mask):
    s = jnp.where(mask, s.astype(jnp.float32), -jnp.inf)
    m = jnp.max(s, axis=-1, keepdims=True)
    m = jnp.where(jnp.isfinite(m), m, 0.0)
    e = jnp.exp(s - m)
    return e / jnp.maximum(jnp.sum(e, axis=-1, keepdims=True), 1e-30)


def sink_softmax(s, mask, sink):
    s = jnp.where(mask, s.astype(jnp.float32), -jnp.inf)
    m = jnp.maximum(jnp.max(s, axis=-1, keepdims=True), sink)
    e = jnp.exp(s - m)
    return e / (jnp.sum(e, axis=-1, keepdims=True) + jnp.exp(sink - m))


def swa_sink_attention(q, k, v, sinks, tab):
    B, T = q.shape[0], q.shape[1]
    G, R = A_KV_HEADS, A_Q_HEADS // A_KV_HEADS
    nb = T // Q_BLOCK
    n_prev = -(-A_WINDOW // Q_BLOCK)
    n_keys = (n_prev + 1) * Q_BLOCK

    def band(t):
        tp = jnp.pad(t, ((0, 0), (n_prev * Q_BLOCK, 0), (0, 0), (0, 0)))
        parts = [tp[:, j * Q_BLOCK:j * Q_BLOCK + T].reshape(B, nb, Q_BLOCK, G, HEAD_DIM) for j in range(n_prev + 1)]
        return jnp.concatenate(parts, axis=2)

    kb, vb = band(k), band(v)
    qb = q.reshape(B, nb, Q_BLOCK, G, R, HEAD_DIM)
    kk = jnp.arange(n_keys)
    dist = jnp.arange(Q_BLOCK)[:, None] + n_prev * Q_BLOCK - kk[None, :]
    kpos = jnp.arange(nb)[:, None] * Q_BLOCK - n_prev * Q_BLOCK + kk[None, :]
    mask = ((dist >= 0) & (dist < A_WINDOW))[None] & (kpos >= 0)[:, None, :]
    s = jnp.einsum('bnqgrd,bnkgd->bgrnqk', qb, kb) * ATTN_SCALE
    s = s.astype(jnp.float32) + head_bias(tab, dist, G, R)[:, :, None]
    p = sink_softmax(s, mask, sinks.astype(jnp.float32).reshape(G, R, 1, 1, 1))
    o = jnp.einsum('bgrnqk,bnkgd->bnqgrd', p.astype(v.dtype), vb)
    return o.reshape(B, T, A_Q_HEADS * HEAD_DIM)


def nsa_compress(kv, pe, w1, w2):
    T = kv.shape[1]
    n_cmp = (T - NSA_CMP_LEN) // NSA_CMP_STRIDE + 1
    idx = (jnp.arange(n_cmp) * NSA_CMP_STRIDE)[:, None] + jnp.arange(NSA_CMP_LEN)[None, :]
    blocks = kv[:, idx] + pe[None, None, :, None, :]
    hid = jax.nn.gelu(jnp.einsum('bnlgd,ldf->bngf', blocks, w1.reshape(NSA_CMP_LEN, HEAD_DIM, NSA_CMP_HIDDEN)))
    return jnp.einsum('bngf,fd->bngd', hid, w2)


def nsa_attention(q, k_c, v_c, k_s, v_s, k_w, v_w, gates, tab):
    B, T = q.shape[0], q.shape[1]
    G, R = B_KV_HEADS, B_Q_HEADS // B_KV_HEADS
    nb = T // Q_BLOCK
    NC = k_c.shape[1]
    NS = T // NSA_SEL_LEN
    top_n = min(NSA_TOP_N, NS)
    cstart = jnp.arange(NC) * NSA_CMP_STRIDE
    cend = cstart + NSA_CMP_LEN - 1
    sstart = jnp.arange(NS) * NSA_SEL_LEN
    overlap = ((cstart[:, None] < sstart[None, :] + NSA_SEL_LEN) & (cstart[:, None] + NSA_CMP_LEN > sstart[None, :])).astype(jnp.float32)
    tab_g = jnp.transpose(tab.reshape(REL_BUCKETS, G, R), (1, 0, 2))
    g_ids = jnp.arange(G)[None, :, None, None]
    ks_blk = jnp.transpose(k_s.reshape(B, NS, NSA_SEL_LEN, G, HEAD_DIM), (0, 3, 1, 2, 4))
    vs_blk = jnp.transpose(v_s.reshape(B, NS, NSA_SEL_LEN, G, HEAD_DIM), (0, 3, 1, 2, 4))
    kw_pad = jnp.pad(k_w, ((0, 0), (NSA_WINDOW, 0), (0, 0), (0, 0)))
    vw_pad = jnp.pad(v_w, ((0, 0), (NSA_WINDOW, 0), (0, 0), (0, 0)))
    take_blocks = jax.vmap(jax.vmap(lambda blk, ix: blk[ix]))
    q_blocks = jnp.transpose(q.reshape(B, nb, Q_BLOCK, G, R, HEAD_DIM), (1, 0, 2, 3, 4, 5))
    g_blocks = jnp.transpose(gates.reshape(B, nb, Q_BLOCK, G, R, 3), (1, 0, 2, 3, 4, 5))
    blk_ids = jnp.arange(NS)

    def one_block(args):
        i, qb, gb = args
        qpos = i * Q_BLOCK + jnp.arange(Q_BLOCK)
        dc = qpos[:, None] - cend[None, :]
        s_c = jnp.einsum('bqgrd,bcgd->bgrqc', qb, k_c) * ATTN_SCALE
        p_c = masked_softmax(s_c.astype(jnp.float32) + head_bias(tab, dc, G, R), dc >= 0)
        o_c = jnp.einsum('bgrqc,bcgd->bqgrd', p_c.astype(v_c.dtype), v_c)
        imp = jnp.einsum('bgrqc,cs->bgqs', p_c, overlap)
        cur = qpos // NSA_SEL_LEN
        forced = (blk_ids[None, :] == 0) | (blk_ids[None, :] == cur[:, None]) | (blk_ids[None, :] == cur[:, None] - 1)
        future = sstart[None, :] > qpos[:, None]
        imp = jnp.where(future, -jnp.inf, jnp.where(forced, NSA_FORCE_SCORE, imp))
        top_val, top_idx = lax.top_k(imp, top_n)
        n_sel = top_n * NSA_SEL_LEN
        ks = take_blocks(ks_blk, top_idx).reshape(B, G, Q_BLOCK, n_sel, HEAD_DIM)
        vs = take_blocks(vs_blk, top_idx).reshape(B, G, Q_BLOCK, n_sel, HEAD_DIM)
        kpos_s = (top_idx[..., None] * NSA_SEL_LEN + jnp.arange(NSA_SEL_LEN)).reshape(B, G, Q_BLOCK, n_sel)
        ds = qpos[None, None, :, None] - kpos_s
        mask_s = jnp.repeat(jnp.isfinite(top_val), NSA_SEL_LEN, axis=-1) & (ds >= 0)
        bias_s = jnp.moveaxis(tab_g[g_ids, rel_bucket(ds)], -1, 2)
        s_s = jnp.einsum('bqgrd,bgqkd->bgrqk', qb, ks) * ATTN_SCALE
        p_s = masked_softmax(s_s.astype(jnp.float32) + bias_s, mask_s[:, :, None])
        o_s = jnp.einsum('bgrqk,bgqkd->bqgrd', p_s.astype(vs.dtype), vs)
        kw = lax.dynamic_slice_in_dim(kw_pad, i * Q_BLOCK, Q_BLOCK + NSA_WINDOW, axis=1)
        vw = lax.dynamic_slice_in_dim(vw_pad, i * Q_BLOCK, Q_BLOCK + NSA_WINDOW, axis=1)
        kpos_w = i * Q_BLOCK - NSA_WINDOW + jnp.arange(Q_BLOCK + NSA_WINDOW)
        dw = qpos[:, None] - kpos_w[None, :]
        mask_w = (dw >= 0) & (dw < NSA_WINDOW) & (kpos_w >= 0)[None, :]
        s_w = jnp.einsum('bqgrd,bkgd->bgrqk', qb, kw) * ATTN_SCALE
        p_w = masked_softmax(s_w.astype(jnp.float32) + head_bias(tab, dw, G, R), mask_w)
        o_w = jnp.einsum('bgrqk,bkgd->bqgrd', p_w.astype(vw.dtype), vw)
        g = jax.nn.sigmoid(gb.astype(jnp.float32)).astype(qb.dtype)
        o = g[..., 0:1] * o_c + g[..., 1:2] * o_s + g[..., 2:3] * o_w
        return o.reshape(B, Q_BLOCK, B_Q_HEADS * HEAD_DIM)

    out = lax.map(one_block, (jnp.arange(nb), q_blocks, g_blocks))
    return jnp.transpose(out, (1, 0, 2, 3)).reshape(B, T, B_Q_HEADS * HEAD_DIM)


def forgetting_attention(q, k, v, f_logit):
    B, T, H = q.shape[0], q.shape[1], q.shape[2]
    nb = T // Q_BLOCK
    c = jnp.moveaxis(jnp.cumsum(jax.nn.log_sigmoid(f_logit.astype(jnp.float32)), axis=1), 1, 2)
    q_blocks = jnp.transpose(q.reshape(B, nb, Q_BLOCK, H, HEAD_DIM), (1, 0, 2, 3, 4))
    c_blocks = jnp.transpose(c.reshape(B, H, nb, Q_BLOCK), (2, 0, 1, 3))
    kpos = jnp.arange(T)

    def one_block(args):
        i, qb, cq = args
        qpos = i * Q_BLOCK + jnp.arange(Q_BLOCK)
        s = jnp.einsum('bqhd,bkhd->bhqk', qb, k).astype(jnp.float32) * ATTN_SCALE
        s = s + cq[..., None] - c[:, :, None, :]
        p = masked_softmax(s, kpos[None, :] <= qpos[:, None])
        return jnp.einsum('bhqk,bkhd->bqhd', p.astype(v.dtype), v)

    out = lax.map(one_block, (jnp.arange(nb), q_blocks, c_blocks))
    return jnp.transpose(out, (1, 0, 2, 3, 4)).reshape(B, T, H * HEAD_DIM)


def even_mixer(h, w_in, w_out, sinks, pe_k, pe_v, ck_w1, ck_w2, cv_w1, cv_w2, rel_bias):
    B, T = h.shape[0], h.shape[1]
    splits = [int(s) for s in np.cumsum(EVEN_SIZES)[:-1]]
    qa, ka, va, qb, kc, vc, ksl, vsl, kwn, vwn, gt = jnp.split(h @ w_in, splits, axis=-1)
    heads = lambda t, n: t.reshape(B, T, n, HEAD_DIM)
    a_out = swa_sink_attention(heads(qa, A_Q_HEADS), heads(ka, A_KV_HEADS), heads(va, A_KV_HEADS), sinks, rel_bias[:, :A_Q_HEADS])
    k_cmp = nsa_compress(heads(kc, B_KV_HEADS), pe_k, ck_w1, ck_w2)
    v_cmp = nsa_compress(heads(vc, B_KV_HEADS), pe_v, cv_w1, cv_w2)
    b_out = nsa_attention(heads(qb, B_Q_HEADS), k_cmp, v_cmp, heads(ksl, B_KV_HEADS), heads(vsl, B_KV_HEADS), heads(kwn, B_KV_HEADS), heads(vwn, B_KV_HEADS), gt, rel_bias[:, A_Q_HEADS:])
    return jnp.concatenate([a_out, b_out], axis=-1) @ w_out


def odd_mixer(h, w_in, w_out, f_bias):
    B, T = h.shape[0], h.shape[1]
    q, k, v, f = jnp.split(h @ w_in, [C_MIX, 2 * C_MIX, 3 * C_MIX], axis=-1)
    heads = lambda t: t.reshape(B, T, C_HEADS, HEAD_DIM)
    return forgetting_attention(heads(q), heads(k), heads(v), f + f_bias) @ w_out


def setup_inputs(seed: int = 0) -> dict:
    key = jax.random.key(seed)
    ks = jax.random.split(key, 20)
    ne = (DEPTH + 1) // 2
    no = DEPTH // 2
    nrm = lambda k, shape, scale: jax.random.normal(k, shape, jnp.float32) * scale
    flat_cmp = NSA_CMP_LEN * HEAD_DIM
    return {
        'x': nrm(ks[0], (BATCH, SEQ, D_MODEL), 1.0),
        'rel_bias': nrm(ks[1], (REL_BUCKETS, REL_HEADS), 0.3),
        'norm_mix': 1.0 + nrm(ks[2], (DEPTH, D_MODEL), 0.05),
        'norm_ffn': 1.0 + nrm(ks[3], (DEPTH, D_MODEL), 0.05),
        'norm_final': 1.0 + nrm(ks[4], (D_MODEL,), 0.05),
        'w_in_even': nrm(ks[5], (ne, D_MODEL, EVEN_IN), D_MODEL ** -0.5),
        'w_out_even': nrm(ks[6], (ne, EVEN_MIX, D_MODEL), EVEN_MIX ** -0.5),
        'a_sinks': nrm(ks[7], (ne, A_Q_HEADS), 0.5),
        'nsa_pe_k': nrm(ks[8], (ne, NSA_CMP_LEN, HEAD_DIM), 0.1),
        'nsa_pe_v': nrm(ks[9], (ne, NSA_CMP_LEN, HEAD_DIM), 0.1),
        'nsa_cmp_k_w1': nrm(ks[10], (ne, flat_cmp, NSA_CMP_HIDDEN), flat_cmp ** -0.5),
        'nsa_cmp_k_w2': nrm(ks[11], (ne, NSA_CMP_HIDDEN, HEAD_DIM), NSA_CMP_HIDDEN ** -0.5),
        'nsa_cmp_v_w1': nrm(ks[12], (ne, flat_cmp, NSA_CMP_HIDDEN), flat_cmp ** -0.5),
        'nsa_cmp_v_w2': nrm(ks[13], (ne, NSA_CMP_HIDDEN, HEAD_DIM), NSA_CMP_HIDDEN ** -0.5),
        'w_in_odd': nrm(ks[14], (no, D_MODEL, ODD_IN), D_MODEL ** -0.5),
        'w_out_odd': nrm(ks[15], (no, C_MIX, D_MODEL), C_MIX ** -0.5),
        'fox_fgate_b': 3.0 + nrm(ks[16], (no, C_HEADS), 0.5),
        'w_ffn_up': nrm(ks[17], (DEPTH, D_MODEL, D_FF), D_MODEL ** -0.5),
        'w_ffn_down': nrm(ks[18], (DEPTH, D_FF, D_MODEL), D_FF ** -0.5),
    }


def reference(x, rel_bias, norm_mix, norm_ffn, norm_final, w_in_even, w_out_even, a_sinks, nsa_pe_k, nsa_pe_v, nsa_cmp_k_w1, nsa_cmp_k_w2, nsa_cmp_v_w1, nsa_cmp_v_w2, w_in_odd, w_out_odd, fox_fgate_b, w_ffn_up, w_ffn_down):
    for layer in range(DEPTH):
        h = rms_norm(x, norm_mix[layer])
        if layer % 2 == 0:
            e = layer // 2
            x = x + even_mixer(h, w_in_even[e], w_out_even[e], a_sinks[e], nsa_pe_k[e], nsa_pe_v[e], nsa_cmp_k_w1[e], nsa_cmp_k_w2[e], nsa_cmp_v_w1[e], nsa_cmp_v_w2[e], rel_bias)
        else:
            o = layer // 2
            x = x + odd_mixer(h, w_in_odd[o], w_out_odd[o], fox_fgate_b[o])
        h = rms_norm(x, norm_ffn[layer])
        u = jax.nn.relu(h @ w_ffn_up[layer])
        x = x + (u * u) @ w_ffn_down[layer]
    return rms_norm(x, norm_final)
```

```python
import functools
import math

import jax
import jax.numpy as jnp
import numpy as np
from jax import lax
from jax.experimental import pallas as pl
from jax.experimental.pallas import tpu as pltpu

F32 = jnp.float32
BF16 = jnp.bfloat16

D_MODEL = 2048
HEAD_DIM = 64
PAIR = 2 * HEAD_DIM
A_Q_HEADS = 16
A_KV_HEADS = 4
A_WINDOW = 128
B_Q_HEADS = 16
B_KV_HEADS = 2
B_GROUP = B_Q_HEADS // B_KV_HEADS
A_GROUP = A_Q_HEADS // A_KV_HEADS
NSA_CMP_LEN = 32
NSA_CMP_STRIDE = 16
NSA_CMP_HIDDEN = 4 * HEAD_DIM
NSA_SEL_LEN = 64
NSA_TOP_N = 16
NSA_WINDOW = 512
NSA_FORCE_SCORE = 1.0e4
C_HEADS = 32
D_FF = 4 * D_MODEL
REL_BUCKETS = 32
REL_MAX_DIST = 1024
Q_BLOCK = 128
RMS_EPS = 1e-6
ATTN_SCALE = HEAD_DIM ** -0.5
NEAR_TILES = 8
NEG_INF = float("-inf")

VMEM_LIMIT_BYTES = 56 * 1024 * 1024


def _cparams(*sem):
    return pltpu.CompilerParams(dimension_semantics=sem, vmem_limit_bytes=VMEM_LIMIT_BYTES)


def _nt(a, b):
    return lax.dot_general(a, b, (((1,), (1,)), ((), ())), preferred_element_type=F32)


def _split3(x):
    hi = x.astype(BF16)
    r1 = x - hi.astype(F32)
    mid = r1.astype(BF16)
    lo = (r1 - mid.astype(F32)).astype(BF16)
    return hi, mid, lo


def _rms(x, g):
    ms = jnp.mean(x * x, axis=-1, keepdims=True)
    return x * lax.rsqrt(ms + RMS_EPS) * g


def _online_update(s, v, m, l, acc):
    m_new = jnp.maximum(m, jnp.max(s, axis=-1, keepdims=True))
    m_safe = jnp.where(m_new == NEG_INF, 0.0, m_new)
    alpha = jnp.exp(m - m_safe)
    p = jnp.exp(s - m_safe)
    l = alpha * l + jnp.sum(p, axis=-1, keepdims=True)
    acc = alpha * acc + jnp.dot(p.astype(BF16), v, preferred_element_type=F32)
    return m_new, l, acc


def _stack_pairs(q_ref, first_pair, n_pairs):
    lane = lax.broadcasted_iota(jnp.int32, (Q_BLOCK, PAIR), 1)
    lo = lane < HEAD_DIM
    parts = []
    for p in range(n_pairs):
        qp = q_ref[:, (first_pair + p) * PAIR:(first_pair + p + 1) * PAIR]
        parts.append(jnp.where(lo, qp, jnp.zeros_like(qp)))
        parts.append(jnp.where(lo, jnp.zeros_like(qp), qp))
    return jnp.concatenate(parts, axis=0)


def _unstack_pairs(o, n_pairs, rows):
    lane = lax.broadcasted_iota(jnp.int32, (rows, PAIR), 1)
    lo = lane < HEAD_DIM
    return [jnp.where(lo, o[(2 * p) * rows:(2 * p + 1) * rows], o[(2 * p + 1) * rows:(2 * p + 2) * rows])
            for p in range(n_pairs)]


def _norm_mm_kernel(x_ref, g_ref, w_ref, o_ref, h_ref):
    @pl.when(pl.program_id(1) == 0)
    def _():
        h_ref[...] = _rms(x_ref[...], g_ref[...]).astype(BF16)

    o_ref[...] = jnp.dot(h_ref[...], w_ref[...], preferred_element_type=F32).astype(o_ref.dtype)


def norm_matmul(x, g, w, out_dtype, tm=512, tn=512):
    m, d = x.shape
    n = w.shape[1]
    tn = min(tn, n)
    return pl.pallas_call(
        _norm_mm_kernel,
        grid=(m // tm, n // tn),
        in_specs=[pl.BlockSpec((tm, d), lambda i, j: (i, 0)),
                  pl.BlockSpec((1, d), lambda i, j: (0, 0)),
                  pl.BlockSpec((d, tn), lambda i, j: (0, j))],
        out_specs=pl.BlockSpec((tm, tn), lambda i, j: (i, j)),
        out_shape=jax.ShapeDtypeStruct((m, n), out_dtype),
        scratch_shapes=[pltpu.VMEM((tm, d), BF16)],
        compiler_params=_cparams("parallel", "arbitrary"),
        name="norm_matmul",
    )(x, g, w)


def _out_proj_kernel(a1_ref, a2_ref, w1_ref, w2_ref, x_ref, o_ref):
    y = jnp.dot(a1_ref[...], w1_ref[...], preferred_element_type=F32)
    y = y + jnp.dot(a2_ref[...], w2_ref[...], preferred_element_type=F32)
    o_ref[...] = x_ref[...] + y


def out_proj_residual(a1, a1_blk, a2, a2_blk, w, x, tm=512, tn=1024):
    m, d = x.shape
    half = d // 2
    return pl.pallas_call(
        _out_proj_kernel,
        grid=(m // tm, d // tn),
        in_specs=[pl.BlockSpec((tm, half), lambda i, j: (i, a1_blk)),
                  pl.BlockSpec((tm, half), lambda i, j: (i, a2_blk)),
                  pl.BlockSpec((half, tn), lambda i, j: (0, j)),
                  pl.BlockSpec((half, tn), lambda i, j: (1, j)),
                  pl.BlockSpec((tm, tn), lambda i, j: (i, j))],
        out_specs=pl.BlockSpec((tm, tn), lambda i, j: (i, j)),
        out_shape=jax.ShapeDtypeStruct((m, d), F32),
        compiler_params=_cparams("parallel", "parallel"),
        name="out_proj",
    )(a1, a2, w, w, x)


def _ffn_kernel(x_ref, g_ref, wu_ref, wd_ref, gf_ref, o_ref, h_ref, acc_ref, *, final_norm):
    k = pl.program_id(1)

    @pl.when(k == 0)
    def _():
        h_ref[...] = _rms(x_ref[...], g_ref[...]).astype(BF16)
        acc_ref[...] = jnp.zeros_like(acc_ref)

    u = jnp.dot(h_ref[...], wu_ref[...], preferred_element_type=F32)
    u = jnp.maximum(u, 0.0)
    acc_ref[...] += jnp.dot((u * u).astype(BF16), wd_ref[...], preferred_element_type=F32)

    @pl.when(k == pl.num_programs(1) - 1)
    def _():
        y = x_ref[...] + acc_ref[...]
        if final_norm:
            y = _rms(y, gf_ref[...])
        o_ref[...] = y


def ffn_residual(x, g, w_up, w_down, g_final, final_norm, tm=512, tf=512):
    m, d = x.shape
    ff = w_up.shape[1]
    return pl.pallas_call(
        functools.partial(_ffn_kernel, final_norm=final_norm),
        grid=(m // tm, ff // tf),
        in_specs=[pl.BlockSpec((tm, d), lambda i, k: (i, 0)),
                  pl.BlockSpec((1, d), lambda i, k: (0, 0)),
                  pl.BlockSpec((d, tf), lambda i, k: (0, k)),
                  pl.BlockSpec((tf, d), lambda i, k: (k, 0)),
                  pl.BlockSpec((1, d), lambda i, k: (0, 0))],
        out_specs=pl.BlockSpec((tm, d), lambda i, k: (i, 0)),
        out_shape=jax.ShapeDtypeStruct((m, d), F32),
        scratch_shapes=[pltpu.VMEM((tm, d), BF16), pltpu.VMEM((tm, d), F32)],
        compiler_params=_cparams("parallel", "arbitrary"),
        name="ffn",
    )(x, g, w_up, w_down, g_final)


def _swa_kernel(q_ref, kp_ref, kc_ref, vp_ref, vc_ref, bias_ref, sink_ref, o_ref):
    i = pl.program_id(1)
    nk = 2 * Q_BLOCK
    ql = lax.broadcasted_iota(jnp.int32, (Q_BLOCK, nk), 0)
    kl = lax.broadcasted_iota(jnp.int32, (Q_BLOCK, nk), 1)
    dist = ql + Q_BLOCK - kl
    valid = (dist >= 0) & (dist < A_WINDOW) & ((kl >= Q_BLOCK) | (i > 0))
    rows = A_GROUP * Q_BLOCK
    for g in range(A_KV_HEADS):
        qs = _stack_pairs(q_ref, g * (A_GROUP // 2), A_GROUP // 2)
        k = jnp.concatenate([kp_ref[:, g * PAIR:(g + 1) * PAIR], kc_ref[:, g * PAIR:(g + 1) * PAIR]], axis=0)
        v = jnp.concatenate([vp_ref[:, g * PAIR:(g + 1) * PAIR], vc_ref[:, g * PAIR:(g + 1) * PAIR]], axis=0)
        s = _nt(qs, k).reshape(A_GROUP, Q_BLOCK, nk) + bias_ref[g * A_GROUP:(g + 1) * A_GROUP]
        s = jnp.where(valid[None], s, NEG_INF)
        sink = sink_ref[g * A_GROUP:(g + 1) * A_GROUP]
        m = jnp.maximum(jnp.max(s, axis=-1, keepdims=True), sink)
        e = jnp.exp(s - m)
        denom = jnp.sum(e, axis=-1, keepdims=True) + jnp.exp(sink - m)
        o = jnp.dot(e.reshape(rows, nk).astype(BF16), v, preferred_element_type=F32)
        o = o / denom.reshape(rows, 1)
        for p, blk in enumerate(_unstack_pairs(o, A_GROUP // 2, Q_BLOCK)):
            c0 = (g * (A_GROUP // 2) + p) * PAIR
            o_ref[:, c0:c0 + PAIR] = blk.astype(o_ref.dtype)


def swa_attention(qkv, bias_a, sinks, batch, seq):
    nb = seq // Q_BLOCK
    m = batch * seq
    qa_w = A_Q_HEADS * HEAD_DIM
    kv_w = A_KV_HEADS * PAIR
    k_blk = qa_w // kv_w
    v_blk = k_blk + 1
    row = lambda b, i: b * nb + i
    prev = lambda b, i: b * nb + jnp.maximum(i - 1, 0)
    return pl.pallas_call(
        _swa_kernel,
        grid=(batch, nb),
        in_specs=[pl.BlockSpec((Q_BLOCK, qa_w), lambda b, i: (row(b, i), 0)),
                  pl.BlockSpec((Q_BLOCK, kv_w), lambda b, i: (prev(b, i), k_blk)),
                  pl.BlockSpec((Q_BLOCK, kv_w), lambda b, i: (row(b, i), k_blk)),
                  pl.BlockSpec((Q_BLOCK, kv_w), lambda b, i: (prev(b, i), v_blk)),
                  pl.BlockSpec((Q_BLOCK, kv_w), lambda b, i: (row(b, i), v_blk)),
                  pl.BlockSpec((A_Q_HEADS, Q_BLOCK, 2 * Q_BLOCK), lambda b, i: (0, 0, 0)),
                  pl.BlockSpec((A_Q_HEADS, 1, 1), lambda b, i: (0, 0, 0))],
        out_specs=pl.BlockSpec((Q_BLOCK, qa_w), lambda b, i: (row(b, i), 0)),
        out_shape=jax.ShapeDtypeStruct((m, qa_w), BF16),
        compiler_params=_cparams("parallel", "parallel"),
        name="swa",
    )(qkv, qkv, qkv, qkv, qkv, bias_a, sinks)


def _compress_kernel(r_ref, pe_ref, w1_ref, w2_ref, o_ref):
    half = NSA_CMP_STRIDE * HEAD_DIM
    r = r_ref[0, 0]
    xa = (r + pe_ref[0, :, :half]).astype(BF16)
    xb = (r + pe_ref[0, :, half:]).astype(BF16)
    a = jnp.dot(xa, w1_ref[0, :half, :], preferred_element_type=F32)
    b = jnp.dot(xb, w1_ref[0, half:, :], preferred_element_type=F32)
    n = r.shape[0]
    hid = jax.nn.gelu(a + pltpu.roll(b, n - 1, 0))
    o_ref[0, 0] = jnp.dot(hid.astype(BF16), w2_ref[0], preferred_element_type=F32).astype(o_ref.dtype)


def nsa_compress(r, pe, w1, w2dup):
    batch, _, n, width = r.shape
    return pl.pallas_call(
        _compress_kernel,
        grid=(batch, 2 * B_KV_HEADS),
        in_specs=[pl.BlockSpec((1, 1, n, width), lambda b, w: (b, w, 0, 0)),
                  pl.BlockSpec((1, 1, 2 * width), lambda b, w: (w // B_KV_HEADS, 0, 0)),
                  pl.BlockSpec((1, 2 * width, NSA_CMP_HIDDEN), lambda b, w: (w // B_KV_HEADS, 0, 0)),
                  pl.BlockSpec((1, NSA_CMP_HIDDEN, PAIR), lambda b, w: (w // B_KV_HEADS, 0, 0))],
        out_specs=pl.BlockSpec((1, 1, n, PAIR), lambda b, w: (b, w, 0, 0)),
        out_shape=jax.ShapeDtypeStruct((batch, 2 * B_KV_HEADS, n, PAIR), BF16),
        compiler_params=_cparams("parallel", "parallel"),
        name="nsa_compress",
    )(r, pe, w1, w2dup)


def _nsa_kernel(q_ref, kcm_ref, vcm_ref, ks_ref, vs_ref, kw_ref, vw_ref, gate_ref,
                bias_c_ref, bias_t_ref, bias_far_ref, ovt_ref, o_ref, val_ref, *, n_cmp_pad):
    i = pl.program_id(2)
    rows = B_GROUP * Q_BLOCK
    n_sel = val_ref.shape[0]
    qs = _stack_pairs(q_ref, 0, B_GROUP // 2)

    ql_c = lax.broadcasted_iota(jnp.int32, (Q_BLOCK, n_cmp_pad), 0) + i * Q_BLOCK
    c_id = lax.broadcasted_iota(jnp.int32, (Q_BLOCK, n_cmp_pad), 1)
    valid_c = (c_id * NSA_CMP_STRIDE + (NSA_CMP_LEN - 1) <= ql_c) & (c_id < n_cmp_pad - 1)
    s_c = _nt(qs, kcm_ref[0, 0]).reshape(B_GROUP, Q_BLOCK, n_cmp_pad) + bias_c_ref[0]
    s_c = jnp.where(valid_c[None], s_c, NEG_INF)
    m_c = jnp.max(s_c, axis=-1, keepdims=True)
    m_c = jnp.where(m_c == NEG_INF, 0.0, m_c)
    e_c = jnp.exp(s_c - m_c)
    p_c = e_c / jnp.maximum(jnp.sum(e_c, axis=-1, keepdims=True), 1e-30)
    o_c = jnp.dot(p_c.reshape(rows, n_cmp_pad).astype(BF16), vcm_ref[0, 0], preferred_element_type=F32)

    p_sum = jnp.sum(p_c, axis=0)
    ovt = ovt_ref[...]
    imp = None
    for piece in _split3(p_sum):
        t = _nt(ovt, piece)
        imp = t if imp is None else imp + t
    blk = lax.broadcasted_iota(jnp.int32, (n_sel, Q_BLOCK), 0)
    qpos = lax.broadcasted_iota(jnp.int32, (n_sel, Q_BLOCK), 1) + i * Q_BLOCK
    cur = lax.shift_right_logical(qpos, int(math.log2(NSA_SEL_LEN)))
    forced = (blk == 0) | (blk == cur) | (blk == cur - 1)
    future = blk * NSA_SEL_LEN > qpos
    val = jnp.where(future, NEG_INF, jnp.where(forced, NSA_FORCE_SCORE, imp))
    val_ref[...] = val
    rank = jnp.zeros((n_sel, Q_BLOCK), F32)
    for s2 in range(n_sel):
        other = val_ref[s2:s2 + 1, :]
        rank = rank + jnp.where(blk > s2, jnp.where(other >= val, 1.0, 0.0), jnp.where(other > val, 1.0, 0.0))
    sel_t = jnp.where((rank < float(NSA_TOP_N)) & (val > NEG_INF), 1.0, 0.0).astype(BF16)
    eye = jnp.where(lax.broadcasted_iota(jnp.int32, (Q_BLOCK, Q_BLOCK), 0)
                    == lax.broadcasted_iota(jnp.int32, (Q_BLOCK, Q_BLOCK), 1), 1.0, 0.0).astype(BF16)
    sel_q = _nt(eye, sel_t).astype(BF16)

    ql = lax.broadcasted_iota(jnp.int32, (Q_BLOCK, Q_BLOCK), 0)
    kl = lax.broadcasted_iota(jnp.int32, (Q_BLOCK, Q_BLOCK), 1)
    e_row = lax.broadcasted_iota(jnp.int32, (n_sel, Q_BLOCK), 0)
    e_col = lax.shift_right_logical(lax.broadcasted_iota(jnp.int32, (n_sel, Q_BLOCK), 1),
                                    int(math.log2(NSA_SEL_LEN)))
    blocks_per_tile = Q_BLOCK // NSA_SEL_LEN

    def sel_tile(j, carry, bias):
        m, l, acc = carry
        expand = jnp.where(e_row == j * blocks_per_tile + e_col, 1.0, 0.0).astype(BF16)
        picked = jnp.dot(sel_q, expand, preferred_element_type=F32) > 0.5
        ok = picked & (kl + j * Q_BLOCK <= ql + i * Q_BLOCK)
        off = pl.multiple_of(j * Q_BLOCK, Q_BLOCK)
        s = _nt(qs, ks_ref[pl.ds(off, Q_BLOCK), :]).reshape(B_GROUP, Q_BLOCK, Q_BLOCK) + bias
        s = jnp.where(ok[None], s, NEG_INF).reshape(rows, Q_BLOCK)
        return _online_update(s, vs_ref[pl.ds(off, Q_BLOCK), :], m, l, acc)

    init = (jnp.full((rows, 1), NEG_INF, F32), jnp.zeros((rows, 1), F32), jnp.zeros((rows, PAIR), F32))
    n_far = jnp.maximum(i - (NEAR_TILES - 1), 0)
    bias_far = bias_far_ref[0]
    carry = lax.fori_loop(0, n_far, lambda j, c: sel_tile(j, c, bias_far), init)
    carry = lax.fori_loop(n_far, i + 1, lambda j, c: sel_tile(j, c, bias_t_ref[0, i - j]), carry)
    _, l_s, acc_s = carry
    o_s = acc_s / jnp.maximum(l_s, 1e-30)

    carry = init
    for d in range(NSA_WINDOW // Q_BLOCK + 1):
        j = jnp.maximum(i - d, 0)
        dist = d * Q_BLOCK + ql - kl
        ok = (dist >= 0) & (dist < NSA_WINDOW) & (i >= d)
        off = pl.multiple_of(j * Q_BLOCK, Q_BLOCK)
        s = _nt(qs, kw_ref[pl.ds(off, Q_BLOCK), :]).reshape(B_GROUP, Q_BLOCK, Q_BLOCK) + bias_t_ref[0, d]
        s = jnp.where(ok[None], s, NEG_INF).reshape(rows, Q_BLOCK)
        carry = _online_update(s, vw_ref[pl.ds(off, Q_BLOCK), :], *carry)
    _, l_w, acc_w = carry
    o_w = acc_w / jnp.maximum(l_w, 1e-30)

    gates = jax.nn.sigmoid(gate_ref[...])
    def gate_col(br):
        return jnp.concatenate([gates[:, 3 * r + br:3 * r + br + 1] for r in range(B_GROUP)], axis=0)
    o = gate_col(0) * o_c + gate_col(1) * o_s + gate_col(2) * o_w
    for p, blk_out in enumerate(_unstack_pairs(o, B_GROUP // 2, Q_BLOCK)):
        o_ref[:, p * PAIR:(p + 1) * PAIR] = blk_out.astype(o_ref.dtype)


def nsa_attention(qkv, cmp_kv, tail, bias_c, bias_t, bias_far, ovt, batch, seq):
    nb = seq // Q_BLOCK
    m = batch * seq
    n_cmp_pad = seq // NSA_CMP_STRIDE
    n_sel = seq // NSA_SEL_LEN
    grp_w = B_GROUP * HEAD_DIM
    q_blk0 = (A_Q_HEADS * HEAD_DIM + 2 * A_KV_HEADS * PAIR) // grp_w
    kv_blk0 = (A_Q_HEADS * HEAD_DIM + 2 * A_KV_HEADS * PAIR + B_Q_HEADS * HEAD_DIM) // PAIR
    kv_spec = lambda t: pl.BlockSpec((seq, PAIR), lambda b, g, i: (b, kv_blk0 + t * B_KV_HEADS + g))
    return pl.pallas_call(
        functools.partial(_nsa_kernel, n_cmp_pad=n_cmp_pad),
        grid=(batch, B_KV_HEADS, nb),
        in_specs=[pl.BlockSpec((Q_BLOCK, grp_w), lambda b, g, i: (b * nb + i, q_blk0 + g)),
                  pl.BlockSpec((1, 1, n_cmp_pad, PAIR), lambda b, g, i: (b, g, 0, 0)),
                  pl.BlockSpec((1, 1, n_cmp_pad, PAIR), lambda b, g, i: (b, B_KV_HEADS + g, 0, 0)),
                  kv_spec(0), kv_spec(1), kv_spec(2), kv_spec(3),
                  pl.BlockSpec((Q_BLOCK, PAIR), lambda b, g, i: (b * nb + i, 2 + g)),
                  pl.BlockSpec((1, B_GROUP, Q_BLOCK, n_cmp_pad), lambda b, g, i: (i, g, 0, 0)),
                  pl.BlockSpec((1, NEAR_TILES, B_GROUP, Q_BLOCK, Q_BLOCK), lambda b, g, i: (g, 0, 0, 0, 0)),
                  pl.BlockSpec((1, B_GROUP, 1, 1), lambda b, g, i: (g, 0, 0, 0)),
                  pl.BlockSpec((n_sel, n_cmp_pad), lambda b, g, i: (0, 0))],
        out_specs=pl.BlockSpec((Q_BLOCK, grp_w), lambda b, g, i: (b * nb + i, g)),
        out_shape=jax.ShapeDtypeStruct((m, B_Q_HEADS * HEAD_DIM), BF16),
        scratch_shapes=[pltpu.VMEM((n_sel, Q_BLOCK), F32)],
        compiler_params=_cparams("parallel", "parallel", "arbitrary"),
        name="nsa",
    )(qkv, cmp_kv, cmp_kv, qkv, qkv, qkv, qkv, tail, bias_c, bias_t, bias_far, ovt)


def _decay_kernel(f_ref, fb_ref, o_ref):
    n_chunks = o_ref.shape[1]
    n_heads = o_ref.shape[2]
    r_i = lax.broadcasted_iota(jnp.int32, (Q_BLOCK, Q_BLOCK), 0)
    c_i = lax.broadcasted_iota(jnp.int32, (Q_BLOCK, Q_BLOCK), 1)
    tri = jnp.where(c_i <= r_i, 1.0, 0.0).astype(BF16)

    def chunk(r, carry):
        x = f_ref[pl.ds(pl.multiple_of(r * Q_BLOCK, Q_BLOCK), Q_BLOCK), :] + fb_ref[...]
        ls = jax.nn.log_sigmoid(x)
        cs = carry
        for piece in _split3(ls):
            cs = cs + jnp.dot(tri, piece, preferred_element_type=F32)
        o_ref[0, r] = cs.T[:n_heads, :]
        return jnp.broadcast_to(cs[Q_BLOCK - 1:Q_BLOCK, :], cs.shape)

    lax.fori_loop(0, n_chunks, chunk, jnp.zeros((Q_BLOCK, PAIR), F32))


def fox_decay(f_tail, f_bias, batch, seq):
    n_chunks = seq // Q_BLOCK
    return pl.pallas_call(
        _decay_kernel,
        grid=(batch,),
        in_specs=[pl.BlockSpec((seq, PAIR), lambda b: (b, 0)),
                  pl.BlockSpec((1, PAIR), lambda b: (0, 0))],
        out_specs=pl.BlockSpec((1, n_chunks, C_HEADS, Q_BLOCK), lambda b: (b, 0, 0, 0)),
        out_shape=jax.ShapeDtypeStruct((batch, n_chunks, C_HEADS, Q_BLOCK), F32),
        compiler_params=_cparams("parallel"),
        name="fox_decay",
    )(f_tail, f_bias)


def _fox_kernel(q_ref, k_ref, v_ref, c_ref, o_ref, *, tq):
    hp = pl.program_id(1)
    t = pl.program_id(2)
    kw = tq
    chunks = kw // Q_BLOCK
    q = q_ref[...]
    lane = lax.broadcasted_iota(jnp.int32, (tq, PAIR), 1)
    lo = lane < HEAD_DIM
    zero = jnp.zeros_like(q)
    qs = jnp.concatenate([jnp.where(lo, q, zero), jnp.where(lo, zero, q)], axis=0)
    ql = lax.broadcasted_iota(jnp.int32, (tq, kw), 0)
    kl = lax.broadcasted_iota(jnp.int32, (tq, kw), 1)
    causal = kl <= ql

    def tile(j, carry, diagonal):
        off = pl.multiple_of(j * kw, kw)
        s = _nt(qs, k_ref[pl.ds(off, kw), :]).reshape(2, tq, kw)
        decay = []
        for hh in range(2):
            decay.append(jnp.concatenate(
                [c_ref[0, j * chunks + u, pl.ds(2 * hp + hh, 1), :] for u in range(chunks)], axis=-1))
        s = s - jnp.stack(decay, axis=0)
        if diagonal:
            s = jnp.where(causal[None], s, NEG_INF)
        return _online_update(s.reshape(2 * tq, kw), v_ref[pl.ds(off, kw), :], *carry)

    init = (jnp.full((2 * tq, 1), NEG_INF, F32), jnp.zeros((2 * tq, 1), F32), jnp.zeros((2 * tq, PAIR), F32))
    carry = lax.fori_loop(0, t, lambda j, c: tile(j, c, False), init)
    _, l, acc = tile(t, carry, True)
    o = acc / jnp.maximum(l, 1e-30)
    o_ref[...] = jnp.where(lo, o[:tq], o[tq:]).astype(o_ref.dtype)


def fox_attention(qkv, decay, batch, seq, tq=512):
    m = batch * seq
    n_pairs = C_HEADS // 2
    nt = seq // tq
    n_chunks = seq // Q_BLOCK
    return pl.pallas_call(
        functools.partial(_fox_kernel, tq=tq),
        grid=(batch, n_pairs, nt),
        in_specs=[pl.BlockSpec((tq, PAIR), lambda b, h, t: (b * nt + t, h)),
                  pl.BlockSpec((seq, PAIR), lambda b, h, t: (b, n_pairs + h)),
                  pl.BlockSpec((seq, PAIR), lambda b, h, t: (b, 2 * n_pairs + h)),
                  pl.BlockSpec((1, n_chunks, C_HEADS, Q_BLOCK), lambda b, h, t: (b, 0, 0, 0))],
        out_specs=pl.BlockSpec((tq, PAIR), lambda b, h, t: (b * nt + t, h)),
        out_shape=jax.ShapeDtypeStruct((m, C_HEADS * HEAD_DIM), BF16),
        compiler_params=_cparams("parallel", "parallel", "arbitrary"),
        name="fox",
    )(qkv, qkv, qkv, decay)


def _rel_bucket(dist):
    n = jnp.maximum(dist, 0)
    max_exact = REL_BUCKETS // 2
    nf = jnp.maximum(n, 1).astype(jnp.float32)
    large = max_exact + (jnp.log(nf / max_exact) / math.log(REL_MAX_DIST / max_exact)
                         * (REL_BUCKETS - max_exact)).astype(jnp.int32)
    return jnp.where(n < max_exact, n, jnp.minimum(large, REL_BUCKETS - 1))


def _bias_tables(rel_bias, seq):
    nb = seq // Q_BLOCK
    n_cmp_pad = seq // NSA_CMP_STRIDE
    ql = jnp.arange(Q_BLOCK)
    d = jnp.arange(NEAR_TILES)
    dist_t = d[:, None, None] * Q_BLOCK + ql[None, :, None] - ql[None, None, :]
    tt = jnp.moveaxis(rel_bias[_rel_bucket(dist_t)], -1, 0)
    bias_a = jnp.concatenate([tt[:A_Q_HEADS, 1], tt[:A_Q_HEADS, 0]], axis=-1)
    tb = tt[A_Q_HEADS:].reshape(B_KV_HEADS, B_GROUP, NEAR_TILES, Q_BLOCK, Q_BLOCK)
    bias_t = jnp.transpose(tb, (0, 2, 1, 3, 4))
    bias_far = rel_bias[REL_BUCKETS - 1, A_Q_HEADS:].reshape(B_KV_HEADS, B_GROUP, 1, 1)
    cend = jnp.arange(n_cmp_pad) * NSA_CMP_STRIDE + NSA_CMP_LEN - 1
    dist_c = (jnp.arange(nb)[:, None, None] * Q_BLOCK + ql[None, :, None]) - cend[None, None, :]
    bias_c = jnp.moveaxis(rel_bias[_rel_bucket(dist_c)][..., A_Q_HEADS:], -1, 1)
    return bias_a, bias_t, bias_far, bias_c


def _overlap_t(seq):
    n_cmp_pad = seq // NSA_CMP_STRIDE
    n_sel = seq // NSA_SEL_LEN
    cstart = np.arange(n_cmp_pad) * NSA_CMP_STRIDE
    sstart = np.arange(n_sel) * NSA_SEL_LEN
    ov = (cstart[None, :] < sstart[:, None] + NSA_SEL_LEN) & (cstart[None, :] + NSA_CMP_LEN > sstart[:, None])
    ov[:, n_cmp_pad - 1] = False
    return jnp.asarray(ov.astype(np.float32), dtype=BF16)


def _dup(w, n_heads):
    d = w.shape[0]
    w = w.reshape(d, n_heads, 1, HEAD_DIM)
    return jnp.broadcast_to(w, (d, n_heads, 2, HEAD_DIM)).reshape(d, n_heads * PAIR)


def _even_weights(w_in):
    sizes = (A_Q_HEADS * HEAD_DIM, A_KV_HEADS * HEAD_DIM, A_KV_HEADS * HEAD_DIM, B_Q_HEADS * HEAD_DIM) \
        + (B_KV_HEADS * HEAD_DIM,) * 6 + (3 * B_Q_HEADS,)
    qa, ka, va, qb, kc, vc, ksl, vsl, kwn, vwn, gt = jnp.split(w_in, np.cumsum(sizes)[:-1].tolist(), axis=-1)
    main = jnp.concatenate([qa * ATTN_SCALE, _dup(ka, A_KV_HEADS), _dup(va, A_KV_HEADS), qb * ATTN_SCALE,
                            _dup(ksl, B_KV_HEADS), _dup(vsl, B_KV_HEADS),
                            _dup(kwn, B_KV_HEADS), _dup(vwn, B_KV_HEADS)], axis=-1).astype(BF16)
    d = w_in.shape[0]
    per_group = 3 * B_GROUP
    gates = [jnp.pad(gt[:, g * per_group:(g + 1) * per_group], ((0, 0), (0, PAIR - per_group)))
             for g in range(B_KV_HEADS)]
    tail = jnp.concatenate([kc, vc] + gates, axis=-1).astype(BF16)
    return main, tail


def _odd_weights(w_in):
    c_mix = C_HEADS * HEAD_DIM
    main = jnp.concatenate([w_in[:, :c_mix] * ATTN_SCALE, w_in[:, c_mix:3 * c_mix]], axis=-1).astype(BF16)
    tail = jnp.pad(w_in[:, 3 * c_mix:], ((0, 0), (0, PAIR - C_HEADS))).astype(BF16)
    return main, tail


def kernel(x, rel_bias, norm_mix, norm_ffn, norm_final, w_in_even, w_out_even, a_sinks, nsa_pe_k, nsa_pe_v,
           nsa_cmp_k_w1, nsa_cmp_k_w2, nsa_cmp_v_w1, nsa_cmp_v_w2, w_in_odd, w_out_odd, fox_fgate_b,
           w_ffn_up, w_ffn_down):
    batch, seq, d = x.shape
    depth = norm_mix.shape[0]
    m = batch * seq
    xf = x.reshape(m, d)
    bias_a, bias_t, bias_far, bias_c = _bias_tables(rel_bias, seq)
    ovt = _overlap_t(seq)
    n_cmp_pad = seq // NSA_CMP_STRIDE
    g_final = norm_final.reshape(1, d)

    for layer in range(depth):
        g_mix = norm_mix[layer].reshape(1, d)
        if layer % 2 == 0:
            e = layer // 2
            w_main, w_tail = _even_weights(w_in_even[e])
            qkv = norm_matmul(xf, g_mix, w_main, BF16)
            tail = norm_matmul(xf, g_mix, w_tail, F32)
            a_out = swa_attention(qkv, bias_a, a_sinks[e].reshape(A_Q_HEADS, 1, 1), batch, seq)
            r = tail[:, :2 * B_KV_HEADS * HEAD_DIM].reshape(batch, seq, 2 * B_KV_HEADS, HEAD_DIM)
            r = jnp.transpose(r, (0, 2, 1, 3)).reshape(batch, 2 * B_KV_HEADS, n_cmp_pad, NSA_CMP_STRIDE * HEAD_DIM)
            pe = jnp.stack([nsa_pe_k[e].reshape(1, -1), nsa_pe_v[e].reshape(1, -1)])
            w1 = jnp.stack([nsa_cmp_k_w1[e], nsa_cmp_v_w1[e]]).astype(BF16)
            w2 = jnp.stack([_dup(nsa_cmp_k_w2[e], 1), _dup(nsa_cmp_v_w2[e], 1)]).astype(BF16)
            cmp_kv = nsa_compress(r, pe, w1, w2)
            b_out = nsa_attention(qkv, cmp_kv, tail, bias_c, bias_t, bias_far, ovt, batch, seq)
            xf = out_proj_residual(a_out, 0, b_out, 0, w_out_even[e].astype(BF16), xf)
        else:
            o = layer // 2
            w_main, w_tail = _odd_weights(w_in_odd[o])
            qkv = norm_matmul(xf, g_mix, w_main, BF16, tn=1024)
            f_tail = norm_matmul(xf, g_mix, w_tail, F32)
            f_bias = jnp.pad(fox_fgate_b[o], (0, PAIR - C_HEADS)).reshape(1, PAIR)
            decay = fox_decay(f_tail, f_bias, batch, seq)
            c_out = fox_attention(qkv, decay, batch, seq)
            xf = out_proj_residual(c_out, 0, c_out, 1, w_out_odd[o].astype(BF16), xf)
        xf = ffn_residual(xf, norm_ffn[layer].reshape(1, d), w_ffn_up[layer].astype(BF16),
                          w_ffn_down[layer].astype(BF16), g_final, layer == depth - 1)
    return xf.reshape(batch, seq, d)
```

```python
import functools
import math

import jax
import jax.numpy as jnp
import numpy as np
from jax import lax
from jax.experimental import pallas as pl
from jax.experimental.pallas import tpu as pltpu

F32 = jnp.float32
BF16 = jnp.bfloat16

D_MODEL = 2048
HEAD_DIM = 64
PAIR = 2 * HEAD_DIM
A_Q_HEADS = 16
A_KV_HEADS = 4
A_WINDOW = 128
B_Q_HEADS = 16
B_KV_HEADS = 2
B_GROUP = B_Q_HEADS // B_KV_HEADS
A_GROUP = A_Q_HEADS // A_KV_HEADS
NSA_CMP_LEN = 32
NSA_CMP_STRIDE = 16
NSA_CMP_HIDDEN = 4 * HEAD_DIM
NSA_SEL_LEN = 64
NSA_TOP_N = 16
NSA_WINDOW = 512
NSA_FORCE_SCORE = 1.0e4
C_HEADS = 32
D_FF = 4 * D_MODEL
REL_BUCKETS = 32
REL_MAX_DIST = 1024
Q_BLOCK = 128
RMS_EPS = 1e-6
ATTN_SCALE = HEAD_DIM ** -0.5
NEAR_TILES = 8
NEG_INF = float("-inf")
ROW_CHUNK = 32

VMEM_LIMIT_BYTES = 56 * 1024 * 1024


def _cparams(*sem):
    return pltpu.CompilerParams(dimension_semantics=sem, vmem_limit_bytes=VMEM_LIMIT_BYTES)


def _nt(a, b):
    return lax.dot_general(a, b, (((1,), (1,)), ((), ())), preferred_element_type=F32)


def _split3(x):
    hi = x.astype(BF16)
    r1 = x - hi.astype(F32)
    mid = r1.astype(BF16)
    lo = (r1 - mid.astype(F32)).astype(BF16)
    return hi, mid, lo


def _rms(x, g):
    ms = jnp.mean(x * x, axis=-1, keepdims=True)
    return x * lax.rsqrt(ms + RMS_EPS) * g


def _online_update(s, v, m, l, acc):
    m_new = jnp.maximum(m, jnp.max(s, axis=-1, keepdims=True))
    m_safe = jnp.where(m_new == NEG_INF, 0.0, m_new)
    alpha = jnp.exp(m - m_safe)
    p = jnp.exp(s - m_safe)
    l = alpha * l + jnp.sum(p, axis=-1, keepdims=True)
    acc = alpha * acc + jnp.dot(p.astype(BF16), v, preferred_element_type=F32)
    return m_new, l, acc


def _stack_pairs(q_ref, first_pair, n_pairs):
    lane = lax.broadcasted_iota(jnp.int32, (Q_BLOCK, PAIR), 1)
    lo = lane < HEAD_DIM
    parts = []
    for p in range(n_pairs):
        qp = q_ref[:, (first_pair + p) * PAIR:(first_pair + p + 1) * PAIR]
        parts.append(jnp.where(lo, qp, jnp.zeros_like(qp)))
        parts.append(jnp.where(lo, jnp.zeros_like(qp), qp))
    return jnp.concatenate(parts, axis=0)


def _unstack_pairs(o, n_pairs, rows):
    lane = lax.broadcasted_iota(jnp.int32, (rows, PAIR), 1)
    lo = lane < HEAD_DIM
    return [jnp.where(lo, o[(2 * p) * rows:(2 * p + 1) * rows], o[(2 * p + 1) * rows:(2 * p + 2) * rows])
            for p in range(n_pairs)]


def _norm_mm_kernel(x_ref, g_ref, w_ref, o_ref, h_ref):
    @pl.when(pl.program_id(1) == 0)
    def _():
        h_ref[...] = _rms(x_ref[...], g_ref[...]).astype(BF16)

    o_ref[...] = jnp.dot(h_ref[...], w_ref[...], preferred_element_type=F32).astype(o_ref.dtype)


def norm_matmul(x, g, w, out_dtype, tm=512, tn=512):
    m, d = x.shape
    n = w.shape[1]
    tn = min(tn, n)
    return pl.pallas_call(
        _norm_mm_kernel,
        grid=(m // tm, n // tn),
        in_specs=[pl.BlockSpec((tm, d), lambda i, j: (i, 0)),
                  pl.BlockSpec((1, d), lambda i, j: (0, 0)),
                  pl.BlockSpec((d, tn), lambda i, j: (0, j))],
        out_specs=pl.BlockSpec((tm, tn), lambda i, j: (i, j)),
        out_shape=jax.ShapeDtypeStruct((m, n), out_dtype),
        scratch_shapes=[pltpu.VMEM((tm, d), BF16)],
        compiler_params=_cparams("parallel", "arbitrary"),
        name="norm_matmul",
    )(x, g, w)


def _out_proj_kernel(a1_ref, a2_ref, w1_ref, w2_ref, x_ref, o_ref):
    y = jnp.dot(a1_ref[...], w1_ref[...], preferred_element_type=F32)
    y = y + jnp.dot(a2_ref[...], w2_ref[...], preferred_element_type=F32)
    o_ref[...] = x_ref[...] + y


def out_proj_residual(a1, a1_blk, a2, a2_blk, w, x, tm=512, tn=1024):
    m, d = x.shape
    half = d // 2
    return pl.pallas_call(
        _out_proj_kernel,
        grid=(m // tm, d // tn),
        in_specs=[pl.BlockSpec((tm, half), lambda i, j: (i, a1_blk)),
                  pl.BlockSpec((tm, half), lambda i, j: (i, a2_blk)),
                  pl.BlockSpec((half, tn), lambda i, j: (0, j)),
                  pl.BlockSpec((half, tn), lambda i, j: (1, j)),
                  pl.BlockSpec((tm, tn), lambda i, j: (i, j))],
        out_specs=pl.BlockSpec((tm, tn), lambda i, j: (i, j)),
        out_shape=jax.ShapeDtypeStruct((m, d), F32),
        compiler_params=_cparams("parallel", "parallel"),
        name="out_proj",
    )(a1, a2, w, w, x)


def _ffn_kernel(x_ref, g_ref, wu_ref, wd_ref, gf_ref, o_ref, h_ref, acc_ref, *, final_norm):
    k = pl.program_id(1)

    @pl.when(k == 0)
    def _():
        h_ref[...] = _rms(x_ref[...], g_ref[...]).astype(BF16)
        acc_ref[...] = jnp.zeros_like(acc_ref)

    u = jnp.dot(h_ref[...], wu_ref[...], preferred_element_type=F32)
    u = jnp.maximum(u, 0.0)
    acc_ref[...] += jnp.dot((u * u).astype(BF16), wd_ref[...], preferred_element_type=F32)

    @pl.when(k == pl.num_programs(1) - 1)
    def _():
        y = x_ref[...] + acc_ref[...]
        if final_norm:
            y = _rms(y, gf_ref[...])
        o_ref[...] = y


def ffn_residual(x, g, w_up, w_down, g_final, final_norm, tm=512, tf=512):
    m, d = x.shape
    ff = w_up.shape[1]
    return pl.pallas_call(
        functools.partial(_ffn_kernel, final_norm=final_norm),
        grid=(m // tm, ff // tf),
        in_specs=[pl.BlockSpec((tm, d), lambda i, k: (i, 0)),
                  pl.BlockSpec((1, d), lambda i, k: (0, 0)),
                  pl.BlockSpec((d, tf), lambda i, k: (0, k)),
                  pl.BlockSpec((tf, d), lambda i, k: (k, 0)),
                  pl.BlockSpec((1, d), lambda i, k: (0, 0))],
        out_specs=pl.BlockSpec((tm, d), lambda i, k: (i, 0)),
        out_shape=jax.ShapeDtypeStruct((m, d), F32),
        scratch_shapes=[pltpu.VMEM((tm, d), BF16), pltpu.VMEM((tm, d), F32)],
        compiler_params=_cparams("parallel", "arbitrary"),
        name="ffn",
    )(x, g, w_up, w_down, g_final)


def _swa_kernel(q_ref, kp_ref, kc_ref, vp_ref, vc_ref, bias_ref, sink_ref, o_ref):
    i = pl.program_id(1)
    nk = 2 * Q_BLOCK
    ql = lax.broadcasted_iota(jnp.int32, (Q_BLOCK, nk), 0)
    kl = lax.broadcasted_iota(jnp.int32, (Q_BLOCK, nk), 1)
    dist = ql + Q_BLOCK - kl
    valid = (dist >= 0) & (dist < A_WINDOW) & ((kl >= Q_BLOCK) | (i > 0))
    rows = A_GROUP * Q_BLOCK
    for g in range(A_KV_HEADS):
        qs = _stack_pairs(q_ref, g * (A_GROUP // 2), A_GROUP // 2)
        k = jnp.concatenate([kp_ref[:, g * PAIR:(g + 1) * PAIR], kc_ref[:, g * PAIR:(g + 1) * PAIR]], axis=0)
        v = jnp.concatenate([vp_ref[:, g * PAIR:(g + 1) * PAIR], vc_ref[:, g * PAIR:(g + 1) * PAIR]], axis=0)
        s = _nt(qs, k).reshape(A_GROUP, Q_BLOCK, nk) + bias_ref[g * A_GROUP:(g + 1) * A_GROUP]
        s = jnp.where(valid[None], s, NEG_INF)
        sink = sink_ref[g * A_GROUP:(g + 1) * A_GROUP]
        m = jnp.maximum(jnp.max(s, axis=-1, keepdims=True), sink)
        e = jnp.exp(s - m)
        denom = jnp.sum(e, axis=-1, keepdims=True) + jnp.exp(sink - m)
        o = jnp.dot(e.reshape(rows, nk).astype(BF16), v, preferred_element_type=F32)
        o = o / denom.reshape(rows, 1)
        for p, blk in enumerate(_unstack_pairs(o, A_GROUP // 2, Q_BLOCK)):
            c0 = (g * (A_GROUP // 2) + p) * PAIR
            o_ref[:, c0:c0 + PAIR] = blk.astype(o_ref.dtype)


def swa_attention(qkv, bias_a, sinks, batch, seq):
    nb = seq // Q_BLOCK
    m = batch * seq
    qa_w = A_Q_HEADS * HEAD_DIM
    kv_w = A_KV_HEADS * PAIR
    k_blk = qa_w // kv_w
    v_blk = k_blk + 1
    row = lambda b, i: b * nb + i
    prev = lambda b, i: b * nb + jnp.maximum(i - 1, 0)
    return pl.pallas_call(
        _swa_kernel,
        grid=(batch, nb),
        in_specs=[pl.BlockSpec((Q_BLOCK, qa_w), lambda b, i: (row(b, i), 0)),
                  pl.BlockSpec((Q_BLOCK, kv_w), lambda b, i: (prev(b, i), k_blk)),
                  pl.BlockSpec((Q_BLOCK, kv_w), lambda b, i: (row(b, i), k_blk)),
                  pl.BlockSpec((Q_BLOCK, kv_w), lambda b, i: (prev(b, i), v_blk)),
                  pl.BlockSpec((Q_BLOCK, kv_w), lambda b, i: (row(b, i), v_blk)),
                  pl.BlockSpec((A_Q_HEADS, Q_BLOCK, 2 * Q_BLOCK), lambda b, i: (0, 0, 0)),
                  pl.BlockSpec((A_Q_HEADS, 1, 1), lambda b, i: (0, 0, 0))],
        out_specs=pl.BlockSpec((Q_BLOCK, qa_w), lambda b, i: (row(b, i), 0)),
        out_shape=jax.ShapeDtypeStruct((m, qa_w), BF16),
        compiler_params=_cparams("parallel", "parallel"),
        name="swa",
    )(qkv, qkv, qkv, qkv, qkv, bias_a, sinks)


def _compress_kernel(r_ref, pe_ref, w1_ref, w2_ref, o_ref):
    half = NSA_CMP_STRIDE * HEAD_DIM
    r = r_ref[0, 0]
    xa = (r + pe_ref[0, :, :half]).astype(BF16)
    xb = (r + pe_ref[0, :, half:]).astype(BF16)
    a = jnp.dot(xa, w1_ref[0, :half, :], preferred_element_type=F32)
    b = jnp.dot(xb, w1_ref[0, half:, :], preferred_element_type=F32)
    n = r.shape[0]
    hid = jax.nn.gelu(a + pltpu.roll(b, n - 1, 0))
    o_ref[0, 0] = jnp.dot(hid.astype(BF16), w2_ref[0], preferred_element_type=F32).astype(o_ref.dtype)


def nsa_compress(r, pe, w1, w2dup):
    batch, _, n, width = r.shape
    return pl.pallas_call(
        _compress_kernel,
        grid=(batch, 2 * B_KV_HEADS),
        in_specs=[pl.BlockSpec((1, 1, n, width), lambda b, w: (b, w, 0, 0)),
                  pl.BlockSpec((1, 1, 2 * width), lambda b, w: (w // B_KV_HEADS, 0, 0)),
                  pl.BlockSpec((1, 2 * width, NSA_CMP_HIDDEN), lambda b, w: (w // B_KV_HEADS, 0, 0)),
                  pl.BlockSpec((1, NSA_CMP_HIDDEN, PAIR), lambda b, w: (w // B_KV_HEADS, 0, 0))],
        out_specs=pl.BlockSpec((1, 1, n, PAIR), lambda b, w: (b, w, 0, 0)),
        out_shape=jax.ShapeDtypeStruct((batch, 2 * B_KV_HEADS, n, PAIR), BF16),
        compiler_params=_cparams("parallel", "parallel"),
        name="nsa_compress",
    )(r, pe, w1, w2dup)


def _nsa_kernel(q_ref, kcm_ref, vcm_ref, ks_ref, vs_ref, kw_ref, vw_ref, gate_ref,
                bias_c_ref, bias_t_ref, bias_far_ref, ovt_ref, o_ref, val_ref, *, n_cmp_pad):
    i = pl.program_id(2)
    rows = B_GROUP * Q_BLOCK
    n_sel = val_ref.shape[0]
    qs = _stack_pairs(q_ref, 0, B_GROUP // 2)

    ql_c = lax.broadcasted_iota(jnp.int32, (Q_BLOCK, n_cmp_pad), 0) + i * Q_BLOCK
    c_id = lax.broadcasted_iota(jnp.int32, (Q_BLOCK, n_cmp_pad), 1)
    valid_c = (c_id * NSA_CMP_STRIDE + (NSA_CMP_LEN - 1) <= ql_c) & (c_id < n_cmp_pad - 1)
    s_c = _nt(qs, kcm_ref[0, 0]).reshape(B_GROUP, Q_BLOCK, n_cmp_pad) + bias_c_ref[0]
    s_c = jnp.where(valid_c[None], s_c, NEG_INF)
    m_c = jnp.max(s_c, axis=-1, keepdims=True)
    m_c = jnp.where(m_c == NEG_INF, 0.0, m_c)
    e_c = jnp.exp(s_c - m_c)
    p_c = e_c / jnp.maximum(jnp.sum(e_c, axis=-1, keepdims=True), 1e-30)
    o_c = jnp.dot(p_c.reshape(rows, n_cmp_pad).astype(BF16), vcm_ref[0, 0], preferred_element_type=F32)

    p_sum = jnp.sum(p_c, axis=0)
    ovt = ovt_ref[...]
    imp = None
    for piece in _split3(p_sum):
        t = _nt(ovt, piece)
        imp = t if imp is None else imp + t
    blk = lax.broadcasted_iota(jnp.int32, (n_sel, Q_BLOCK), 0)
    qpos = lax.broadcasted_iota(jnp.int32, (n_sel, Q_BLOCK), 1) + i * Q_BLOCK
    cur = lax.shift_right_logical(qpos, int(math.log2(NSA_SEL_LEN)))
    forced = (blk == 0) | (blk == cur) | (blk == cur - 1)
    future = blk * NSA_SEL_LEN > qpos
    val = jnp.where(future, NEG_INF, jnp.where(forced, NSA_FORCE_SCORE, imp))
    val_ref[...] = val
    rank = jnp.zeros((n_sel, Q_BLOCK), F32)
    for s2 in range(n_sel):
        other = val_ref[s2:s2 + 1, :]
        rank = rank + jnp.where(blk > s2, jnp.where(other >= val, 1.0, 0.0), jnp.where(other > val, 1.0, 0.0))
    sel_t = jnp.where((rank < float(NSA_TOP_N)) & (val > NEG_INF), 1.0, 0.0).astype(BF16)
    eye = jnp.where(lax.broadcasted_iota(jnp.int32, (Q_BLOCK, Q_BLOCK), 0)
                    == lax.broadcasted_iota(jnp.int32, (Q_BLOCK, Q_BLOCK), 1), 1.0, 0.0).astype(BF16)
    sel_q = _nt(eye, sel_t).astype(BF16)

    ql = lax.broadcasted_iota(jnp.int32, (Q_BLOCK, Q_BLOCK), 0)
    kl = lax.broadcasted_iota(jnp.int32, (Q_BLOCK, Q_BLOCK), 1)
    e_row = lax.broadcasted_iota(jnp.int32, (n_sel, Q_BLOCK), 0)
    e_col = lax.shift_right_logical(lax.broadcasted_iota(jnp.int32, (n_sel, Q_BLOCK), 1),
                                    int(math.log2(NSA_SEL_LEN)))
    blocks_per_tile = Q_BLOCK // NSA_SEL_LEN

    def sel_tile(j, carry, bias):
        m, l, acc = carry
        expand = jnp.where(e_row == j * blocks_per_tile + e_col, 1.0, 0.0).astype(BF16)
        picked = jnp.dot(sel_q, expand, preferred_element_type=F32) > 0.5
        ok = picked & (kl + j * Q_BLOCK <= ql + i * Q_BLOCK)
        off = pl.multiple_of(j * Q_BLOCK, Q_BLOCK)
        s = _nt(qs, ks_ref[pl.ds(off, Q_BLOCK), :]).reshape(B_GROUP, Q_BLOCK, Q_BLOCK) + bias
        s = jnp.where(ok[None], s, NEG_INF).reshape(rows, Q_BLOCK)
        return _online_update(s, vs_ref[pl.ds(off, Q_BLOCK), :], m, l, acc)

    init = (jnp.full((rows, 1), NEG_INF, F32), jnp.zeros((rows, 1), F32), jnp.zeros((rows, PAIR), F32))
    n_far = jnp.maximum(i - (NEAR_TILES - 1), 0)
    bias_far = bias_far_ref[0]
    carry = lax.fori_loop(0, n_far, lambda j, c: sel_tile(j, c, bias_far), init)
    carry = lax.fori_loop(n_far, i + 1, lambda j, c: sel_tile(j, c, bias_t_ref[0, i - j]), carry)
    _, l_s, acc_s = carry
    o_s = acc_s / jnp.maximum(l_s, 1e-30)

    carry = init
    for d in range(NSA_WINDOW // Q_BLOCK + 1):
        j = jnp.maximum(i - d, 0)
        dist = d * Q_BLOCK + ql - kl
        ok = (dist >= 0) & (dist < NSA_WINDOW) & (i >= d)
        off = pl.multiple_of(j * Q_BLOCK, Q_BLOCK)
        s = _nt(qs, kw_ref[pl.ds(off, Q_BLOCK), :]).reshape(B_GROUP, Q_BLOCK, Q_BLOCK) + bias_t_ref[0, d]
        s = jnp.where(ok[None], s, NEG_INF).reshape(rows, Q_BLOCK)
        carry = _online_update(s, vw_ref[pl.ds(off, Q_BLOCK), :], *carry)
    _, l_w, acc_w = carry
    o_w = acc_w / jnp.maximum(l_w, 1e-30)

    gates = jax.nn.sigmoid(gate_ref[...])
    def gate_col(br):
        return jnp.concatenate([gates[:, 3 * r + br:3 * r + br + 1] for r in range(B_GROUP)], axis=0)
    o = gate_col(0) * o_c + gate_col(1) * o_s + gate_col(2) * o_w
    for p, blk_out in enumerate(_unstack_pairs(o, B_GROUP // 2, Q_BLOCK)):
        o_ref[:, p * PAIR:(p + 1) * PAIR] = blk_out.astype(o_ref.dtype)


def nsa_attention(qkv, cmp_kv, tail, bias_c, bias_t, bias_far, ovt, batch, seq):
    nb = seq // Q_BLOCK
    m = batch * seq
    n_cmp_pad = seq // NSA_CMP_STRIDE
    n_sel = seq // NSA_SEL_LEN
    grp_w = B_GROUP * HEAD_DIM
    q_blk0 = (A_Q_HEADS * HEAD_DIM + 2 * A_KV_HEADS * PAIR) // grp_w
    kv_blk0 = (A_Q_HEADS * HEAD_DIM + 2 * A_KV_HEADS * PAIR + B_Q_HEADS * HEAD_DIM) // PAIR
    kv_spec = lambda t: pl.BlockSpec((seq, PAIR), lambda b, g, i: (b, kv_blk0 + t * B_KV_HEADS + g))
    return pl.pallas_call(
        functools.partial(_nsa_kernel, n_cmp_pad=n_cmp_pad),
        grid=(batch, B_KV_HEADS, nb),
        in_specs=[pl.BlockSpec((Q_BLOCK, grp_w), lambda b, g, i: (b * nb + i, q_blk0 + g)),
                  pl.BlockSpec((1, 1, n_cmp_pad, PAIR), lambda b, g, i: (b, g, 0, 0)),
                  pl.BlockSpec((1, 1, n_cmp_pad, PAIR), lambda b, g, i: (b, B_KV_HEADS + g, 0, 0)),
                  kv_spec(0), kv_spec(1), kv_spec(2), kv_spec(3),
                  pl.BlockSpec((Q_BLOCK, PAIR), lambda b, g, i: (b * nb + i, 2 + g)),
                  pl.BlockSpec((1, B_GROUP, Q_BLOCK, n_cmp_pad), lambda b, g, i: (i, g, 0, 0)),
                  pl.BlockSpec((1, NEAR_TILES, B_GROUP, Q_BLOCK, Q_BLOCK), lambda b, g, i: (g, 0, 0, 0, 0)),
                  pl.BlockSpec((1, B_GROUP, 1, 1), lambda b, g, i: (g, 0, 0, 0)),
                  pl.BlockSpec((n_sel, n_cmp_pad), lambda b, g, i: (0, 0))],
        out_specs=pl.BlockSpec((Q_BLOCK, grp_w), lambda b, g, i: (b * nb + i, g)),
        out_shape=jax.ShapeDtypeStruct((m, B_Q_HEADS * HEAD_DIM), BF16),
        scratch_shapes=[pltpu.VMEM((n_sel, Q_BLOCK), F32)],
        compiler_params=_cparams("parallel", "parallel", "arbitrary"),
        name="nsa",
    )(qkv, cmp_kv, cmp_kv, qkv, qkv, qkv, qkv, tail, bias_c, bias_t, bias_far, ovt)


DECAY_STEP = 512


def _decay_kernel(f_ref, fb_ref, place_ref, o_ref, carry_ref):
    r_i = lax.broadcasted_iota(jnp.int32, (Q_BLOCK, Q_BLOCK), 0)
    c_i = lax.broadcasted_iota(jnp.int32, (Q_BLOCK, Q_BLOCK), 1)
    tri = jnp.where(c_i <= r_i, 1.0, 0.0).astype(BF16)

    @pl.when(pl.program_id(1) == 0)
    def _():
        carry_ref[...] = jnp.zeros_like(carry_ref)

    carry = carry_ref[...]
    for r in range(DECAY_STEP // Q_BLOCK):
        x = f_ref[r * Q_BLOCK:(r + 1) * Q_BLOCK, :] + fb_ref[...]
        ls = jax.nn.log_sigmoid(x)
        cs = jnp.broadcast_to(carry, ls.shape)
        for piece in _split3(ls):
            cs = cs + jnp.dot(tri, piece, preferred_element_type=F32)
        feat = None
        for n, piece in enumerate(_split3(cs)):
            term = jnp.dot(piece, place_ref[n], preferred_element_type=F32)
            feat = term if feat is None else feat + term
        o_ref[r * Q_BLOCK:(r + 1) * Q_BLOCK, :] = feat.astype(o_ref.dtype)
        carry = cs[Q_BLOCK - 1:Q_BLOCK, :]
    carry_ref[...] = carry


def _decay_placement():
    place = np.zeros((3, PAIR, (C_HEADS // 2) * PAIR), np.float32)
    for h in range(C_HEADS):
        for n in range(3):
            place[n, h, (h // 2) * PAIR + 3 * (h % 2) + n] = -1.0
    return jnp.asarray(place, dtype=BF16)


def fox_decay(f_tail, f_bias, batch, seq):
    steps = seq // DECAY_STEP
    width = (C_HEADS // 2) * PAIR
    return pl.pallas_call(
        _decay_kernel,
        grid=(batch, steps),
        in_specs=[pl.BlockSpec((DECAY_STEP, PAIR), lambda b, s: (b * steps + s, 0)),
                  pl.BlockSpec((1, PAIR), lambda b, s: (0, 0)),
                  pl.BlockSpec((3, PAIR, width), lambda b, s: (0, 0, 0))],
        out_specs=pl.BlockSpec((DECAY_STEP, width), lambda b, s: (b * steps + s, 0)),
        out_shape=jax.ShapeDtypeStruct((batch * seq, width), BF16),
        scratch_shapes=[pltpu.VMEM((1, PAIR), F32)],
        compiler_params=_cparams("parallel", "arbitrary"),
        name="fox_decay",
    )(f_tail, f_bias, _decay_placement())


def _softmax_tile(lane_tiles, n_chunks, rc, n_lane_tiles, m_ref, l_ref, a_ref, pm_ref, p_ref):
    for c in range(n_chunks):
        pm = lane_tiles(c, 0)
        for u in range(1, n_lane_tiles):
            pm = jnp.maximum(pm, lane_tiles(c, u))
        pm_ref[c * rc:(c + 1) * rc, :] = pm
    m_old = m_ref[...]
    m_new = jnp.maximum(m_old, jnp.max(pm_ref[...], axis=-1, keepdims=True))
    a_ref[...] = jnp.exp(m_old - m_new)
    m_ref[...] = m_new
    for c in range(n_chunks):
        rows = slice(c * rc, (c + 1) * rc)
        mb = m_ref[rows, :]
        psum = None
        for u in range(n_lane_tiles):
            p = jnp.exp(lane_tiles(c, u) - mb)
            p_ref[rows, u * Q_BLOCK:(u + 1) * Q_BLOCK] = p.astype(BF16)
            psum = p if psum is None else psum + p
        l_ref[rows, :] = a_ref[rows, :] * l_ref[rows, :] + psum


def _fox_kernel(q_ref, k_ref, v_ref, e_ref, o_ref, s_ref, p_ref, m_ref, l_ref, a_ref, pm_ref, acc_ref, *, tq):
    t = pl.program_id(2)
    kw = tq
    n_lane_tiles = kw // Q_BLOCK
    n_chunks = tq // ROW_CHUNK
    q = q_ref[...]
    lane = lax.broadcasted_iota(jnp.int32, (tq, PAIR), 1)
    lo = lane < HEAD_DIM
    zero = jnp.zeros_like(q)
    qs = [jnp.concatenate([jnp.where(lo if hh == 0 else ~lo, q, zero),
                           jnp.where((lane >= 3 * hh) & (lane < 3 * hh + 3), 1.0, 0.0).astype(BF16)], axis=1)
          for hh in range(2)]
    m_ref[...] = jnp.full(m_ref.shape, NEG_INF, F32)
    l_ref[...] = jnp.zeros(l_ref.shape, F32)
    acc_ref[...] = jnp.zeros(acc_ref.shape, F32)
    col_minus_row = (lax.broadcasted_iota(jnp.int32, (ROW_CHUNK, Q_BLOCK), 1)
                     - lax.broadcasted_iota(jnp.int32, (ROW_CHUNK, Q_BLOCK), 0))

    def scores(hh, j):
        off = pl.multiple_of(j * kw, kw)
        k_aug = jnp.concatenate([k_ref[pl.ds(off, kw), :], e_ref[pl.ds(off, kw), :]], axis=1)
        s_ref[hh] = _nt(qs[hh], k_aug)

    def tile(j, diagonal):
        v = v_ref[pl.ds(pl.multiple_of(j * kw, kw), kw), :]
        for hh in range(2):
            def lane_tiles(c, u):
                s = s_ref[hh, c * ROW_CHUNK:(c + 1) * ROW_CHUNK, u * Q_BLOCK:(u + 1) * Q_BLOCK]
                if diagonal:
                    s = jnp.where(col_minus_row <= c * ROW_CHUNK - u * Q_BLOCK, s, NEG_INF)
                return s

            _softmax_tile(lane_tiles, n_chunks, ROW_CHUNK, n_lane_tiles, m_ref.at[hh], l_ref.at[hh], a_ref.at[hh],
                          pm_ref.at[hh], p_ref.at[hh])
            acc_ref[hh] = a_ref[hh] * acc_ref[hh] + jnp.dot(p_ref[hh], v, preferred_element_type=F32)
            if not diagonal:
                scores(hh, j + 1)

    def body(j, carry):
        tile(j, False)
        return carry

    for hh in range(2):
        scores(hh, 0)
    lax.fori_loop(0, t, body, 0)
    tile(t, True)
    o = [acc_ref[hh] / jnp.maximum(jnp.sum(l_ref[hh], axis=-1, keepdims=True), 1e-30) for hh in range(2)]
    o_ref[...] = jnp.where(lo, o[0], o[1]).astype(o_ref.dtype)


def fox_attention(qkv, decay, batch, seq, tq=512):
    m = batch * seq
    n_pairs = C_HEADS // 2
    nt = seq // tq
    return pl.pallas_call(
        functools.partial(_fox_kernel, tq=tq),
        grid=(batch, n_pairs, nt),
        in_specs=[pl.BlockSpec((tq, PAIR), lambda b, h, t: (b * nt + t, h)),
                  pl.BlockSpec((seq, PAIR), lambda b, h, t: (b, n_pairs + h)),
                  pl.BlockSpec((seq, PAIR), lambda b, h, t: (b, 2 * n_pairs + h)),
                  pl.BlockSpec((seq, PAIR), lambda b, h, t: (b, h))],
        out_specs=pl.BlockSpec((tq, PAIR), lambda b, h, t: (b * nt + t, h)),
        out_shape=jax.ShapeDtypeStruct((m, C_HEADS * HEAD_DIM), BF16),
        scratch_shapes=[pltpu.VMEM((2, tq, tq), F32), pltpu.VMEM((2, tq, tq), BF16)]
        + [pltpu.VMEM((2, tq, PAIR), F32)] * 5,
        compiler_params=_cparams("parallel", "parallel", "arbitrary"),
        name="fox",
    )(qkv, qkv, qkv, decay)


def _rel_bucket(dist):
    n = jnp.maximum(dist, 0)
    max_exact = REL_BUCKETS // 2
    nf = jnp.maximum(n, 1).astype(jnp.float32)
    large = max_exact + (jnp.log(nf / max_exact) / math.log(REL_MAX_DIST / max_exact)
                         * (REL_BUCKETS - max_exact)).astype(jnp.int32)
    return jnp.where(n < max_exact, n, jnp.minimum(large, REL_BUCKETS - 1))


def _lookup(table, idx):
    onehot = (idx[..., None] == jnp.arange(table.shape[0])).astype(F32)
    return jnp.einsum("...n,nh->...h", onehot, table, precision=lax.Precision.HIGHEST)


def _bias_tables(rel_bias, seq):
    nb = seq // Q_BLOCK
    n_cmp_pad = seq // NSA_CMP_STRIDE
    ql = jnp.arange(Q_BLOCK)
    d = jnp.arange(NEAR_TILES)
    dist_t = d[:, None, None] * Q_BLOCK + ql[None, :, None] - ql[None, None, :]
    tt = jnp.moveaxis(_lookup(rel_bias, _rel_bucket(dist_t)), -1, 0)
    bias_a = jnp.concatenate([tt[:A_Q_HEADS, 1], tt[:A_Q_HEADS, 0]], axis=-1)
    tb = tt[A_Q_HEADS:].reshape(B_KV_HEADS, B_GROUP, NEAR_TILES, Q_BLOCK, Q_BLOCK)
    bias_t = jnp.transpose(tb, (0, 2, 1, 3, 4))
    bias_far = rel_bias[REL_BUCKETS - 1, A_Q_HEADS:].reshape(B_KV_HEADS, B_GROUP, 1, 1)
    cend = jnp.arange(n_cmp_pad) * NSA_CMP_STRIDE + NSA_CMP_LEN - 1
    dist_c = (jnp.arange(nb)[:, None, None] * Q_BLOCK + ql[None, :, None]) - cend[None, None, :]
    bias_c = jnp.moveaxis(_lookup(rel_bias[:, A_Q_HEADS:], _rel_bucket(dist_c)), -1, 1)
    return bias_a, bias_t, bias_far, bias_c


def _overlap_t(seq):
    n_cmp_pad = seq // NSA_CMP_STRIDE
    n_sel = seq // NSA_SEL_LEN
    cstart = np.arange(n_cmp_pad) * NSA_CMP_STRIDE
    sstart = np.arange(n_sel) * NSA_SEL_LEN
    ov = (cstart[None, :] < sstart[:, None] + NSA_SEL_LEN) & (cstart[None, :] + NSA_CMP_LEN > sstart[:, None])
    ov[:, n_cmp_pad - 1] = False
    return jnp.asarray(ov.astype(np.float32), dtype=BF16)


def _dup(w, n_heads):
    d = w.shape[0]
    w = w.reshape(d, n_heads, 1, HEAD_DIM)
    return jnp.broadcast_to(w, (d, n_heads, 2, HEAD_DIM)).reshape(d, n_heads * PAIR)


def _even_weights(w_in):
    sizes = (A_Q_HEADS * HEAD_DIM, A_KV_HEADS * HEAD_DIM, A_KV_HEADS * HEAD_DIM, B_Q_HEADS * HEAD_DIM) \
        + (B_KV_HEADS * HEAD_DIM,) * 6 + (3 * B_Q_HEADS,)
    qa, ka, va, qb, kc, vc, ksl, vsl, kwn, vwn, gt = jnp.split(w_in, np.cumsum(sizes)[:-1].tolist(), axis=-1)
    main = jnp.concatenate([qa * ATTN_SCALE, _dup(ka, A_KV_HEADS), _dup(va, A_KV_HEADS), qb * ATTN_SCALE,
                            _dup(ksl, B_KV_HEADS), _dup(vsl, B_KV_HEADS),
                            _dup(kwn, B_KV_HEADS), _dup(vwn, B_KV_HEADS)], axis=-1).astype(BF16)
    d = w_in.shape[0]
    per_group = 3 * B_GROUP
    gates = [jnp.pad(gt[:, g * per_group:(g + 1) * per_group], ((0, 0), (0, PAIR - per_group)))
             for g in range(B_KV_HEADS)]
    tail = jnp.concatenate([kc, vc] + gates, axis=-1).astype(BF16)
    return main, tail


def _odd_weights(w_in):
    c_mix = C_HEADS * HEAD_DIM
    main = jnp.concatenate([w_in[:, :c_mix] * ATTN_SCALE, w_in[:, c_mix:3 * c_mix]], axis=-1).astype(BF16)
    tail = jnp.pad(w_in[:, 3 * c_mix:], ((0, 0), (0, PAIR - C_HEADS))).astype(BF16)
    return main, tail


def kernel(x, rel_bias, norm_mix, norm_ffn, norm_final, w_in_even, w_out_even, a_sinks, nsa_pe_k, nsa_pe_v,
           nsa_cmp_k_w1, nsa_cmp_k_w2, nsa_cmp_v_w1, nsa_cmp_v_w2, w_in_odd, w_out_odd, fox_fgate_b,
           w_ffn_up, w_ffn_down):
    batch, seq, d = x.shape
    depth = norm_mix.shape[0]
    m = batch * seq
    xf = x.reshape(m, d)
    bias_a, bias_t, bias_far, bias_c = _bias_tables(rel_bias, seq)
    ovt = _overlap_t(seq)
    n_cmp_pad = seq // NSA_CMP_STRIDE
    g_final = norm_final.reshape(1, d)

    for layer in range(depth):
        g_mix = norm_mix[layer].reshape(1, d)
        if layer % 2 == 0:
            e = layer // 2
            w_main, w_tail = _even_weights(w_in_even[e])
            qkv = norm_matmul(xf, g_mix, w_main, BF16)
            tail = norm_matmul(xf, g_mix, w_tail, F32)
            a_out = swa_attention(qkv, bias_a, a_sinks[e].reshape(A_Q_HEADS, 1, 1), batch, seq)
            r = tail[:, :2 * B_KV_HEADS * HEAD_DIM].reshape(batch, seq, 2 * B_KV_HEADS, HEAD_DIM)
            r = jnp.transpose(r, (0, 2, 1, 3)).reshape(batch, 2 * B_KV_HEADS, n_cmp_pad, NSA_CMP_STRIDE * HEAD_DIM)
            pe = jnp.stack([nsa_pe_k[e].reshape(1, -1), nsa_pe_v[e].reshape(1, -1)])
            w1 = jnp.stack([nsa_cmp_k_w1[e], nsa_cmp_v_w1[e]]).astype(BF16)
            w2 = jnp.stack([_dup(nsa_cmp_k_w2[e], 1), _dup(nsa_cmp_v_w2[e], 1)]).astype(BF16)
            cmp_kv = nsa_compress(r, pe, w1, w2)
            b_out = nsa_attention(qkv, cmp_kv, tail, bias_c, bias_t, bias_far, ovt, batch, seq)
            xf = out_proj_residual(a_out, 0, b_out, 0, w_out_even[e].astype(BF16), xf)
        else:
            o = layer // 2
            w_main, w_tail = _odd_weights(w_in_odd[o])
            qkv = norm_matmul(xf, g_mix, w_main, BF16, tn=1024)
            f_tail = norm_matmul(xf, g_mix, w_tail, F32)
            f_bias = jnp.pad(fox_fgate_b[o], (0, PAIR - C_HEADS)).reshape(1, PAIR)
            decay = fox_decay(f_tail, f_bias, batch, seq)
            c_out = fox_attention(qkv, decay, batch, seq)
            xf = out_proj_residual(c_out, 0, c_out, 1, w_out_odd[o].astype(BF16), xf)
        xf = ffn_residual(xf, norm_ffn[layer].reshape(1, d), w_ffn_up[layer].astype(BF16),
                          w_ffn_down[layer].astype(BF16), g_final, layer == depth - 1)
    return xf.reshape(batch, seq, d)
```

```python
import functools
import math

import jax
import jax.numpy as jnp
import numpy as np
from jax import lax
from jax.experimental import pallas as pl
from jax.experimental.pallas import tpu as pltpu

F32 = jnp.float32
BF16 = jnp.bfloat16

D_MODEL = 2048
HEAD_DIM = 64
PAIR = 2 * HEAD_DIM
A_Q_HEADS = 16
A_KV_HEADS = 4
A_WINDOW = 128
B_Q_HEADS = 16
B_KV_HEADS = 2
B_GROUP = B_Q_HEADS // B_KV_HEADS
A_GROUP = A_Q_HEADS // A_KV_HEADS
NSA_CMP_LEN = 32
NSA_CMP_STRIDE = 16
NSA_CMP_HIDDEN = 4 * HEAD_DIM
NSA_SEL_LEN = 64
NSA_TOP_N = 16
NSA_WINDOW = 512
NSA_FORCE_SCORE = 1.0e4
C_HEADS = 32
D_FF = 4 * D_MODEL
REL_BUCKETS = 32
REL_MAX_DIST = 1024
Q_BLOCK = 128
RMS_EPS = 1e-6
ATTN_SCALE = HEAD_DIM ** -0.5
NEAR_TILES = 8
NEG_INF = float("-inf")
ROW_CHUNK = 32

VMEM_LIMIT_BYTES = 56 * 1024 * 1024


def _cparams(*sem):
    return pltpu.CompilerParams(dimension_semantics=sem, vmem_limit_bytes=VMEM_LIMIT_BYTES)


def _nt(a, b):
    return lax.dot_general(a, b, (((1,), (1,)), ((), ())), preferred_element_type=F32)


def _split3(x):
    hi = x.astype(BF16)
    r1 = x - hi.astype(F32)
    mid = r1.astype(BF16)
    lo = (r1 - mid.astype(F32)).astype(BF16)
    return hi, mid, lo


def _rms(x, g):
    ms = jnp.mean(x * x, axis=-1, keepdims=True)
    return x * lax.rsqrt(ms + RMS_EPS) * g


def _stack_pairs(q_ref, first_pair, n_pairs):
    lane = lax.broadcasted_iota(jnp.int32, (Q_BLOCK, PAIR), 1)
    lo = lane < HEAD_DIM
    parts = []
    for p in range(n_pairs):
        qp = q_ref[:, (first_pair + p) * PAIR:(first_pair + p + 1) * PAIR]
        parts.append(jnp.where(lo, qp, jnp.zeros_like(qp)))
        parts.append(jnp.where(lo, jnp.zeros_like(qp), qp))
    return jnp.concatenate(parts, axis=0)


def _unstack_pairs(o, n_pairs, rows):
    lane = lax.broadcasted_iota(jnp.int32, (rows, PAIR), 1)
    lo = lane < HEAD_DIM
    return [jnp.where(lo, o[(2 * p) * rows:(2 * p + 1) * rows], o[(2 * p + 1) * rows:(2 * p + 2) * rows])
            for p in range(n_pairs)]


def _norm_mm_kernel(x_ref, g_ref, w_ref, o_ref, h_ref):
    @pl.when(pl.program_id(1) == 0)
    def _():
        h_ref[...] = _rms(x_ref[...], g_ref[...]).astype(BF16)

    o_ref[...] = jnp.dot(h_ref[...], w_ref[...], preferred_element_type=F32).astype(o_ref.dtype)


def norm_matmul(x, g, w, out_dtype, tm=512, tn=512):
    m, d = x.shape
    n = w.shape[1]
    tn = min(tn, n)
    return pl.pallas_call(
        _norm_mm_kernel,
        grid=(m // tm, n // tn),
        in_specs=[pl.BlockSpec((tm, d), lambda i, j: (i, 0)),
                  pl.BlockSpec((1, d), lambda i, j: (0, 0)),
                  pl.BlockSpec((d, tn), lambda i, j: (0, j))],
        out_specs=pl.BlockSpec((tm, tn), lambda i, j: (i, j)),
        out_shape=jax.ShapeDtypeStruct((m, n), out_dtype),
        scratch_shapes=[pltpu.VMEM((tm, d), BF16)],
        compiler_params=_cparams("parallel", "arbitrary"),
        name="norm_matmul",
    )(x, g, w)


def _out_proj_kernel(a1_ref, a2_ref, w1_ref, w2_ref, x_ref, o_ref):
    y = jnp.dot(a1_ref[...], w1_ref[...], preferred_element_type=F32)
    y = y + jnp.dot(a2_ref[...], w2_ref[...], preferred_element_type=F32)
    o_ref[...] = x_ref[...] + y


def out_proj_residual(a1, a1_blk, a2, a2_blk, w, x, tm=512, tn=1024):
    m, d = x.shape
    half = d // 2
    return pl.pallas_call(
        _out_proj_kernel,
        grid=(m // tm, d // tn),
        in_specs=[pl.BlockSpec((tm, half), lambda i, j: (i, a1_blk)),
                  pl.BlockSpec((tm, half), lambda i, j: (i, a2_blk)),
                  pl.BlockSpec((half, tn), lambda i, j: (0, j)),
                  pl.BlockSpec((half, tn), lambda i, j: (1, j)),
                  pl.BlockSpec((tm, tn), lambda i, j: (i, j))],
        out_specs=pl.BlockSpec((tm, tn), lambda i, j: (i, j)),
        out_shape=jax.ShapeDtypeStruct((m, d), F32),
        compiler_params=_cparams("parallel", "parallel"),
        name="out_proj",
    )(a1, a2, w, w, x)


def _ffn_kernel(x_ref, g_ref, wu_ref, wd_ref, gf_ref, o_ref, h_ref, acc_ref, *, final_norm):
    k = pl.program_id(1)

    @pl.when(k == 0)
    def _():
        h_ref[...] = _rms(x_ref[...], g_ref[...]).astype(BF16)
        acc_ref[...] = jnp.zeros_like(acc_ref)

    u = jnp.dot(h_ref[...], wu_ref[...], preferred_element_type=F32)
    u = jnp.maximum(u, 0.0)
    acc_ref[...] += jnp.dot((u * u).astype(BF16), wd_ref[...], preferred_element_type=F32)

    @pl.when(k == pl.num_programs(1) - 1)
    def _():
        y = x_ref[...] + acc_ref[...]
        if final_norm:
            y = _rms(y, gf_ref[...])
        o_ref[...] = y


def ffn_residual(x, g, w_up, w_down, g_final, final_norm, tm=512, tf=512):
    m, d = x.shape
    ff = w_up.shape[1]
    return pl.pallas_call(
        functools.partial(_ffn_kernel, final_norm=final_norm),
        grid=(m // tm, ff // tf),
        in_specs=[pl.BlockSpec((tm, d), lambda i, k: (i, 0)),
                  pl.BlockSpec((1, d), lambda i, k: (0, 0)),
                  pl.BlockSpec((d, tf), lambda i, k: (0, k)),
                  pl.BlockSpec((tf, d), lambda i, k: (k, 0)),
                  pl.BlockSpec((1, d), lambda i, k: (0, 0))],
        out_specs=pl.BlockSpec((tm, d), lambda i, k: (i, 0)),
        out_shape=jax.ShapeDtypeStruct((m, d), F32),
        scratch_shapes=[pltpu.VMEM((tm, d), BF16), pltpu.VMEM((tm, d), F32)],
        compiler_params=_cparams("parallel", "arbitrary"),
        name="ffn",
    )(x, g, w_up, w_down, g_final)


def _swa_kernel(q_ref, kp_ref, kc_ref, vp_ref, vc_ref, bias_ref, sink_ref, o_ref):
    i = pl.program_id(1)
    nk = 2 * Q_BLOCK
    ql = lax.broadcasted_iota(jnp.int32, (Q_BLOCK, nk), 0)
    kl = lax.broadcasted_iota(jnp.int32, (Q_BLOCK, nk), 1)
    dist = ql + Q_BLOCK - kl
    valid = (dist >= 0) & (dist < A_WINDOW) & ((kl >= Q_BLOCK) | (i > 0))
    rows = A_GROUP * Q_BLOCK
    for g in range(A_KV_HEADS):
        qs = _stack_pairs(q_ref, g * (A_GROUP // 2), A_GROUP // 2)
        k = jnp.concatenate([kp_ref[:, g * PAIR:(g + 1) * PAIR], kc_ref[:, g * PAIR:(g + 1) * PAIR]], axis=0)
        v = jnp.concatenate([vp_ref[:, g * PAIR:(g + 1) * PAIR], vc_ref[:, g * PAIR:(g + 1) * PAIR]], axis=0)
        s = _nt(qs, k).reshape(A_GROUP, Q_BLOCK, nk) + bias_ref[g * A_GROUP:(g + 1) * A_GROUP]
        s = jnp.where(valid[None], s, NEG_INF)
        sink = sink_ref[g * A_GROUP:(g + 1) * A_GROUP]
        m = jnp.maximum(jnp.max(s, axis=-1, keepdims=True), sink)
        e = jnp.exp(s - m)
        denom = jnp.sum(e, axis=-1, keepdims=True) + jnp.exp(sink - m)
        o = jnp.dot(e.reshape(rows, nk).astype(BF16), v, preferred_element_type=F32)
        o = o / denom.reshape(rows, 1)
        for p, blk in enumerate(_unstack_pairs(o, A_GROUP // 2, Q_BLOCK)):
            c0 = (g * (A_GROUP // 2) + p) * PAIR
            o_ref[:, c0:c0 + PAIR] = blk.astype(o_ref.dtype)


def swa_attention(qkv, bias_a, sinks, batch, seq):
    nb = seq // Q_BLOCK
    m = batch * seq
    qa_w = A_Q_HEADS * HEAD_DIM
    kv_w = A_KV_HEADS * PAIR
    k_blk = qa_w // kv_w
    v_blk = k_blk + 1
    row = lambda b, i: b * nb + i
    prev = lambda b, i: b * nb + jnp.maximum(i - 1, 0)
    return pl.pallas_call(
        _swa_kernel,
        grid=(batch, nb),
        in_specs=[pl.BlockSpec((Q_BLOCK, qa_w), lambda b, i: (row(b, i), 0)),
                  pl.BlockSpec((Q_BLOCK, kv_w), lambda b, i: (prev(b, i), k_blk)),
                  pl.BlockSpec((Q_BLOCK, kv_w), lambda b, i: (row(b, i), k_blk)),
                  pl.BlockSpec((Q_BLOCK, kv_w), lambda b, i: (prev(b, i), v_blk)),
                  pl.BlockSpec((Q_BLOCK, kv_w), lambda b, i: (row(b, i), v_blk)),
                  pl.BlockSpec((A_Q_HEADS, Q_BLOCK, 2 * Q_BLOCK), lambda b, i: (0, 0, 0)),
                  pl.BlockSpec((A_Q_HEADS, 1, 1), lambda b, i: (0, 0, 0))],
        out_specs=pl.BlockSpec((Q_BLOCK, qa_w), lambda b, i: (row(b, i), 0)),
        out_shape=jax.ShapeDtypeStruct((m, qa_w), BF16),
        compiler_params=_cparams("parallel", "parallel"),
        name="swa",
    )(qkv, qkv, qkv, qkv, qkv, bias_a, sinks)


def _compress_kernel(r_ref, pe_ref, w1_ref, w2_ref, o_ref):
    half = NSA_CMP_STRIDE * HEAD_DIM
    r = r_ref[0, 0]
    xa = (r + pe_ref[0, :, :half]).astype(BF16)
    xb = (r + pe_ref[0, :, half:]).astype(BF16)
    a = jnp.dot(xa, w1_ref[0, :half, :], preferred_element_type=F32)
    b = jnp.dot(xb, w1_ref[0, half:, :], preferred_element_type=F32)
    n = r.shape[0]
    hid = jax.nn.gelu(a + pltpu.roll(b, n - 1, 0))
    o_ref[0, 0] = jnp.dot(hid.astype(BF16), w2_ref[0], preferred_element_type=F32).astype(o_ref.dtype)


def nsa_compress(r, pe, w1, w2dup):
    batch, _, n, width = r.shape
    return pl.pallas_call(
        _compress_kernel,
        grid=(batch, 2 * B_KV_HEADS),
        in_specs=[pl.BlockSpec((1, 1, n, width), lambda b, w: (b, w, 0, 0)),
                  pl.BlockSpec((1, 1, 2 * width), lambda b, w: (w // B_KV_HEADS, 0, 0)),
                  pl.BlockSpec((1, 2 * width, NSA_CMP_HIDDEN), lambda b, w: (w // B_KV_HEADS, 0, 0)),
                  pl.BlockSpec((1, NSA_CMP_HIDDEN, PAIR), lambda b, w: (w // B_KV_HEADS, 0, 0))],
        out_specs=pl.BlockSpec((1, 1, n, PAIR), lambda b, w: (b, w, 0, 0)),
        out_shape=jax.ShapeDtypeStruct((batch, 2 * B_KV_HEADS, n, PAIR), BF16),
        compiler_params=_cparams("parallel", "parallel"),
        name="nsa_compress",
    )(r, pe, w1, w2dup)


NSA_KEY_TILE = 512
NSA_STREAMS = 2
WIN_TILES = NSA_WINDOW // Q_BLOCK + 1
TAB_FAR = NEAR_TILES
TAB_WIN_EDGE = NEAR_TILES + 1
TAB_NONE = NEAR_TILES + 2
TAB_ENTRIES = NEAR_TILES + 3
UNSELECTED = -1.0e30


def _nsa_kernel(q_ref, kcm_ref, vcm_ref, ks_ref, vs_ref, kw_ref, vw_ref, gate_ref, bias_c_ref, tab_ref, ovt_ref,
                o_ref, val_ref, s_ref, p_ref, m_ref, l_ref, a_ref, pm_ref, acc_ref, *, n_cmp_pad):
    i = pl.program_id(2)
    rows = B_GROUP * Q_BLOCK
    srows = rows // NSA_STREAMS
    heads_per_stream = B_GROUP // NSA_STREAMS
    chunks_per_head = Q_BLOCK // ROW_CHUNK
    n_sel = val_ref.shape[0]
    sel_shift = int(math.log2(NSA_SEL_LEN))
    qs = _stack_pairs(q_ref, 0, B_GROUP // 2)

    ql_c = lax.broadcasted_iota(jnp.int32, (Q_BLOCK, n_cmp_pad), 0) + i * Q_BLOCK
    c_id = lax.broadcasted_iota(jnp.int32, (Q_BLOCK, n_cmp_pad), 1)
    valid_c = (c_id * NSA_CMP_STRIDE + (NSA_CMP_LEN - 1) <= ql_c) & (c_id < n_cmp_pad - 1)
    s_c = _nt(qs, kcm_ref[0, 0]).reshape(B_GROUP, Q_BLOCK, n_cmp_pad) + bias_c_ref[0]
    s_c = jnp.where(valid_c[None], s_c, NEG_INF)
    m_c = jnp.max(s_c, axis=-1, keepdims=True)
    m_c = jnp.where(m_c == NEG_INF, 0.0, m_c)
    e_c = jnp.exp(s_c - m_c)
    p_c = e_c / jnp.maximum(jnp.sum(e_c, axis=-1, keepdims=True), 1e-30)
    o_c = jnp.dot(p_c.reshape(rows, n_cmp_pad).astype(BF16), vcm_ref[0, 0], preferred_element_type=F32)

    p_sum = jnp.sum(p_c, axis=0)
    ovt = ovt_ref[...]
    imp = None
    for piece in _split3(p_sum):
        t = _nt(ovt, piece)
        imp = t if imp is None else imp + t
    blk = lax.broadcasted_iota(jnp.int32, (n_sel, Q_BLOCK), 0)
    qpos = lax.broadcasted_iota(jnp.int32, (n_sel, Q_BLOCK), 1) + i * Q_BLOCK
    cur = lax.shift_right_logical(qpos, sel_shift)
    forced = (blk == 0) | (blk == cur) | (blk == cur - 1)
    future = blk * NSA_SEL_LEN > qpos
    val = jnp.where(future, NEG_INF, jnp.where(forced, NSA_FORCE_SCORE, imp))
    val_ref[...] = val
    rank = jnp.zeros((n_sel, Q_BLOCK), F32)
    for s2 in range(n_sel):
        other = val_ref[s2:s2 + 1, :]
        rank = rank + jnp.where(blk > s2, jnp.where(other >= val, 1.0, 0.0), jnp.where(other > val, 1.0, 0.0))
    sel_t = jnp.where((rank < float(NSA_TOP_N)) & (val > NEG_INF), 1.0, 0.0).astype(BF16)
    if n_sel < Q_BLOCK:
        sel_t = jnp.concatenate([sel_t, jnp.zeros((Q_BLOCK - n_sel, Q_BLOCK), BF16)], axis=0)
    eye = jnp.where(lax.broadcasted_iota(jnp.int32, (Q_BLOCK, Q_BLOCK), 0)
                    == lax.broadcasted_iota(jnp.int32, (Q_BLOCK, Q_BLOCK), 1), 1.0, 0.0).astype(BF16)
    unsel = jnp.where(_nt(eye, sel_t) > 0.5, 0.0, UNSELECTED).astype(BF16)
    unsel_rows = jnp.concatenate([unsel] * heads_per_stream, axis=0)
    qs_st = [qs[st * srows:(st + 1) * srows] for st in range(NSA_STREAMS)]
    qs_sel = [jnp.concatenate([qs_st[st], unsel_rows], axis=1) for st in range(NSA_STREAMS)]

    key_lane = lax.broadcasted_iota(jnp.int32, (NSA_KEY_TILE, PAIR), 1)
    key_blk = lax.shift_right_logical(lax.broadcasted_iota(jnp.int32, (NSA_KEY_TILE, PAIR), 0), sel_shift)
    lane_minus_blk = key_lane - key_blk

    def reset():
        m_ref[...] = jnp.full(m_ref.shape, NEG_INF, F32)
        l_ref[...] = jnp.zeros(l_ref.shape, F32)
        acc_ref[...] = jnp.zeros(acc_ref.shape, F32)

    def soft_pv(st, tab_idx, v):
        width = len(tab_idx) * Q_BLOCK

        def lane_tiles(c, u):
            head = st * heads_per_stream + c // chunks_per_head
            q0 = (c % chunks_per_head) * ROW_CHUNK
            return (s_ref[st, c * ROW_CHUNK:(c + 1) * ROW_CHUNK, u * Q_BLOCK:(u + 1) * Q_BLOCK]
                    + tab_ref[0, tab_idx[u], head, q0:q0 + ROW_CHUNK, :])

        _softmax_tile(lane_tiles, srows // ROW_CHUNK, ROW_CHUNK, len(tab_idx), m_ref.at[st], l_ref.at[st],
                      a_ref.at[st], pm_ref.at[st], p_ref.at[st])
        acc_ref[st] = a_ref[st] * acc_ref[st] + jnp.dot(p_ref[st, :, :width], v, preferred_element_type=F32)

    def finish():
        return jnp.concatenate(
            [acc_ref[st] / jnp.maximum(jnp.sum(l_ref[st], axis=-1, keepdims=True), 1e-30)
             for st in range(NSA_STREAMS)], axis=0)

    blocks_per_tile = NSA_KEY_TILE // NSA_SEL_LEN
    lane_tiles_per_tile = NSA_KEY_TILE // Q_BLOCK

    def sel_scores(st, jt):
        off = pl.multiple_of(jt * NSA_KEY_TILE, NSA_KEY_TILE)
        block_onehot = jnp.where(lane_minus_blk == jt * blocks_per_tile, 1.0, 0.0).astype(BF16)
        k_aug = jnp.concatenate([ks_ref[pl.ds(off, NSA_KEY_TILE), :], block_onehot], axis=1)
        s_ref[st, :, :NSA_KEY_TILE] = _nt(qs_sel[st], k_aug)

    def sel_tile(jt, last):
        v = vs_ref[pl.ds(pl.multiple_of(jt * NSA_KEY_TILE, NSA_KEY_TILE), NSA_KEY_TILE), :]
        tab_idx = []
        for u in range(lane_tiles_per_tile):
            d = i - (jt * lane_tiles_per_tile + u)
            tab_idx.append(jnp.where(d < 0, TAB_NONE, jnp.minimum(d, TAB_FAR)))
        for st in range(NSA_STREAMS):
            soft_pv(st, tab_idx, v)
            if not last:
                sel_scores(st, jt + 1)

    def sel_body(jt, carry):
        sel_tile(jt, False)
        return carry

    n_tiles = lax.shift_right_logical(i + lane_tiles_per_tile, int(math.log2(lane_tiles_per_tile)))
    reset()
    for st in range(NSA_STREAMS):
        sel_scores(st, 0)
    lax.fori_loop(0, n_tiles - 1, sel_body, 0)
    sel_tile(n_tiles - 1, True)
    o_s = finish()

    first_blk = jnp.maximum(i - (WIN_TILES - 1), 0)
    off = pl.multiple_of(first_blk * Q_BLOCK, Q_BLOCK)
    k_win = kw_ref[pl.ds(off, WIN_TILES * Q_BLOCK), :]
    v_win = vw_ref[pl.ds(off, WIN_TILES * Q_BLOCK), :]
    tab_idx = []
    for u in range(WIN_TILES):
        d = i - (first_blk + u)
        tab_idx.append(jnp.where(d < 0, TAB_NONE, jnp.where(d == WIN_TILES - 1, TAB_WIN_EDGE, d)))
    reset()
    for st in range(NSA_STREAMS):
        s_ref[st] = _nt(qs_st[st], k_win)
    for st in range(NSA_STREAMS):
        soft_pv(st, tab_idx, v_win)
    o_w = finish()

    gates = jax.nn.sigmoid(gate_ref[...])

    def gate_col(br):
        return jnp.concatenate([gates[:, 3 * r + br:3 * r + br + 1] for r in range(B_GROUP)], axis=0)

    o = gate_col(0) * o_c + gate_col(1) * o_s + gate_col(2) * o_w
    for p, blk_out in enumerate(_unstack_pairs(o, B_GROUP // 2, Q_BLOCK)):
        o_ref[:, p * PAIR:(p + 1) * PAIR] = blk_out.astype(o_ref.dtype)


def nsa_attention(qkv, cmp_kv, tail, bias_c, bias_tab, ovt, batch, seq):
    nb = seq // Q_BLOCK
    m = batch * seq
    n_cmp_pad = seq // NSA_CMP_STRIDE
    n_sel = seq // NSA_SEL_LEN
    assert n_sel <= Q_BLOCK and seq % NSA_KEY_TILE == 0 and seq >= WIN_TILES * Q_BLOCK
    grp_w = B_GROUP * HEAD_DIM
    q_blk0 = (A_Q_HEADS * HEAD_DIM + 2 * A_KV_HEADS * PAIR) // grp_w
    kv_blk0 = (A_Q_HEADS * HEAD_DIM + 2 * A_KV_HEADS * PAIR + B_Q_HEADS * HEAD_DIM) // PAIR
    kv_spec = lambda t: pl.BlockSpec((seq, PAIR), lambda b, g, i: (b, kv_blk0 + t * B_KV_HEADS + g))
    srows = B_GROUP * Q_BLOCK // NSA_STREAMS
    s_width = WIN_TILES * Q_BLOCK
    return pl.pallas_call(
        functools.partial(_nsa_kernel, n_cmp_pad=n_cmp_pad),
        grid=(batch, B_KV_HEADS, nb),
        in_specs=[pl.BlockSpec((Q_BLOCK, grp_w), lambda b, g, i: (b * nb + i, q_blk0 + g)),
                  pl.BlockSpec((1, 1, n_cmp_pad, PAIR), lambda b, g, i: (b, g, 0, 0)),
                  pl.BlockSpec((1, 1, n_cmp_pad, PAIR), lambda b, g, i: (b, B_KV_HEADS + g, 0, 0)),
                  kv_spec(0), kv_spec(1), kv_spec(2), kv_spec(3),
                  pl.BlockSpec((Q_BLOCK, PAIR), lambda b, g, i: (b * nb + i, 2 + g)),
                  pl.BlockSpec((1, B_GROUP, Q_BLOCK, n_cmp_pad), lambda b, g, i: (i, g, 0, 0)),
                  pl.BlockSpec((1, TAB_ENTRIES, B_GROUP, Q_BLOCK, Q_BLOCK), lambda b, g, i: (g, 0, 0, 0, 0)),
                  pl.BlockSpec((n_sel, n_cmp_pad), lambda b, g, i: (0, 0))],
        out_specs=pl.BlockSpec((Q_BLOCK, grp_w), lambda b, g, i: (b * nb + i, g)),
        out_shape=jax.ShapeDtypeStruct((m, B_Q_HEADS * HEAD_DIM), BF16),
        scratch_shapes=[pltpu.VMEM((n_sel, Q_BLOCK), F32),
                        pltpu.VMEM((NSA_STREAMS, srows, s_width), F32),
                        pltpu.VMEM((NSA_STREAMS, srows, s_width), BF16)]
        + [pltpu.VMEM((NSA_STREAMS, srows, PAIR), F32)] * 5,
        compiler_params=_cparams("parallel", "parallel", "arbitrary"),
        name="nsa",
    )(qkv, cmp_kv, cmp_kv, qkv, qkv, qkv, qkv, tail, bias_c, bias_tab, ovt)


DECAY_STEP = 512


def _decay_kernel(f_ref, fb_ref, place_ref, o_ref, carry_ref):
    r_i = lax.broadcasted_iota(jnp.int32, (Q_BLOCK, Q_BLOCK), 0)
    c_i = lax.broadcasted_iota(jnp.int32, (Q_BLOCK, Q_BLOCK), 1)
    tri = jnp.where(c_i <= r_i, 1.0, 0.0).astype(BF16)

    @pl.when(pl.program_id(1) == 0)
    def _():
        carry_ref[...] = jnp.zeros_like(carry_ref)

    carry = carry_ref[...]
    for r in range(DECAY_STEP // Q_BLOCK):
        x = f_ref[r * Q_BLOCK:(r + 1) * Q_BLOCK, :] + fb_ref[...]
        ls = jax.nn.log_sigmoid(x)
        cs = jnp.broadcast_to(carry, ls.shape)
        for piece in _split3(ls):
            cs = cs + jnp.dot(tri, piece, preferred_element_type=F32)
        feat = None
        for n, piece in enumerate(_split3(cs)):
            term = jnp.dot(piece, place_ref[n], preferred_element_type=F32)
            feat = term if feat is None else feat + term
        o_ref[r * Q_BLOCK:(r + 1) * Q_BLOCK, :] = feat.astype(o_ref.dtype)
        carry = cs[Q_BLOCK - 1:Q_BLOCK, :]
    carry_ref[...] = carry


def _decay_placement():
    place = np.zeros((3, PAIR, (C_HEADS // 2) * PAIR), np.float32)
    for h in range(C_HEADS):
        for n in range(3):
            place[n, h, (h // 2) * PAIR + 3 * (h % 2) + n] = -1.0
    return jnp.asarray(place, dtype=BF16)


def fox_decay(f_tail, f_bias, batch, seq):
    steps = seq // DECAY_STEP
    width = (C_HEADS // 2) * PAIR
    return pl.pallas_call(
        _decay_kernel,
        grid=(batch, steps),
        in_specs=[pl.BlockSpec((DECAY_STEP, PAIR), lambda b, s: (b * steps + s, 0)),
                  pl.BlockSpec((1, PAIR), lambda b, s: (0, 0)),
                  pl.BlockSpec((3, PAIR, width), lambda b, s: (0, 0, 0))],
        out_specs=pl.BlockSpec((DECAY_STEP, width), lambda b, s: (b * steps + s, 0)),
        out_shape=jax.ShapeDtypeStruct((batch * seq, width), BF16),
        scratch_shapes=[pltpu.VMEM((1, PAIR), F32)],
        compiler_params=_cparams("parallel", "arbitrary"),
        name="fox_decay",
    )(f_tail, f_bias, _decay_placement())


def _softmax_tile(lane_tiles, n_chunks, rc, n_lane_tiles, m_ref, l_ref, a_ref, pm_ref, p_ref):
    for c in range(n_chunks):
        pm = lane_tiles(c, 0)
        for u in range(1, n_lane_tiles):
            pm = jnp.maximum(pm, lane_tiles(c, u))
        pm_ref[c * rc:(c + 1) * rc, :] = pm
    m_old = m_ref[...]
    m_new = jnp.maximum(m_old, jnp.max(pm_ref[...], axis=-1, keepdims=True))
    a_ref[...] = jnp.exp(m_old - m_new)
    m_ref[...] = m_new
    for c in range(n_chunks):
        rows = slice(c * rc, (c + 1) * rc)
        mb = m_ref[rows, :]
        psum = None
        for u in range(n_lane_tiles):
            p = jnp.exp(lane_tiles(c, u) - mb)
            p_ref[rows, u * Q_BLOCK:(u + 1) * Q_BLOCK] = p.astype(BF16)
            psum = p if psum is None else psum + p
        l_ref[rows, :] = a_ref[rows, :] * l_ref[rows, :] + psum


def _fox_kernel(q_ref, k_ref, v_ref, e_ref, o_ref, s_ref, p_ref, m_ref, l_ref, a_ref, pm_ref, acc_ref, *, tq):
    t = pl.program_id(2)
    kw = tq
    n_lane_tiles = kw // Q_BLOCK
    n_chunks = tq // ROW_CHUNK
    q = q_ref[...]
    lane = lax.broadcasted_iota(jnp.int32, (tq, PAIR), 1)
    lo = lane < HEAD_DIM
    zero = jnp.zeros_like(q)
    qs = [jnp.concatenate([jnp.where(lo if hh == 0 else ~lo, q, zero),
                           jnp.where((lane >= 3 * hh) & (lane < 3 * hh + 3), 1.0, 0.0).astype(BF16)], axis=1)
          for hh in range(2)]
    m_ref[...] = jnp.full(m_ref.shape, NEG_INF, F32)
    l_ref[...] = jnp.zeros(l_ref.shape, F32)
    acc_ref[...] = jnp.zeros(acc_ref.shape, F32)
    col_minus_row = (lax.broadcasted_iota(jnp.int32, (ROW_CHUNK, Q_BLOCK), 1)
                     - lax.broadcasted_iota(jnp.int32, (ROW_CHUNK, Q_BLOCK), 0))

    def scores(hh, j):
        off = pl.multiple_of(j * kw, kw)
        k_aug = jnp.concatenate([k_ref[pl.ds(off, kw), :], e_ref[pl.ds(off, kw), :]], axis=1)
        s_ref[hh] = _nt(qs[hh], k_aug)

    def tile(j, diagonal):
        v = v_ref[pl.ds(pl.multiple_of(j * kw, kw), kw), :]
        for hh in range(2):
            def lane_tiles(c, u):
                s = s_ref[hh, c * ROW_CHUNK:(c + 1) * ROW_CHUNK, u * Q_BLOCK:(u + 1) * Q_BLOCK]
                if diagonal:
                    s = jnp.where(col_minus_row <= c * ROW_CHUNK - u * Q_BLOCK, s, NEG_INF)
                return s

            _softmax_tile(lane_tiles, n_chunks, ROW_CHUNK, n_lane_tiles, m_ref.at[hh], l_ref.at[hh], a_ref.at[hh],
                          pm_ref.at[hh], p_ref.at[hh])
            acc_ref[hh] = a_ref[hh] * acc_ref[hh] + jnp.dot(p_ref[hh], v, preferred_element_type=F32)
            if not diagonal:
                scores(hh, j + 1)

    def body(j, carry):
        tile(j, False)
        return carry

    for hh in range(2):
        scores(hh, 0)
    lax.fori_loop(0, t, body, 0)
    tile(t, True)
    o = [acc_ref[hh] / jnp.maximum(jnp.sum(l_ref[hh], axis=-1, keepdims=True), 1e-30) for hh in range(2)]
    o_ref[...] = jnp.where(lo, o[0], o[1]).astype(o_ref.dtype)


def fox_attention(qkv, decay, batch, seq, tq=512):
    m = batch * seq
    n_pairs = C_HEADS // 2
    nt = seq // tq
    return pl.pallas_call(
        functools.partial(_fox_kernel, tq=tq),
        grid=(batch, n_pairs, nt),
        in_specs=[pl.BlockSpec((tq, PAIR), lambda b, h, t: (b * nt + t, h)),
                  pl.BlockSpec((seq, PAIR), lambda b, h, t: (b, n_pairs + h)),
                  pl.BlockSpec((seq, PAIR), lambda b, h, t: (b, 2 * n_pairs + h)),
                  pl.BlockSpec((seq, PAIR), lambda b, h, t: (b, h))],
        out_specs=pl.BlockSpec((tq, PAIR), lambda b, h, t: (b * nt + t, h)),
        out_shape=jax.ShapeDtypeStruct((m, C_HEADS * HEAD_DIM), BF16),
        scratch_shapes=[pltpu.VMEM((2, tq, tq), F32), pltpu.VMEM((2, tq, tq), BF16)]
        + [pltpu.VMEM((2, tq, PAIR), F32)] * 5,
        compiler_params=_cparams("parallel", "parallel", "arbitrary"),
        name="fox",
    )(qkv, qkv, qkv, decay)


def _rel_bucket(dist):
    n = jnp.maximum(dist, 0)
    max_exact = REL_BUCKETS // 2
    nf = jnp.maximum(n, 1).astype(jnp.float32)
    large = max_exact + (jnp.log(nf / max_exact) / math.log(REL_MAX_DIST / max_exact)
                         * (REL_BUCKETS - max_exact)).astype(jnp.int32)
    return jnp.where(n < max_exact, n, jnp.minimum(large, REL_BUCKETS - 1))


def _lookup(table, idx):
    onehot = (idx[..., None] == jnp.arange(table.shape[0])).astype(F32)
    return jnp.einsum("...n,nh->...h", onehot, table, precision=lax.Precision.HIGHEST)


def _bias_tables(rel_bias, seq):
    nb = seq // Q_BLOCK
    n_cmp_pad = seq // NSA_CMP_STRIDE
    ql = jnp.arange(Q_BLOCK)
    d = jnp.arange(NEAR_TILES)
    dist_t = d[:, None, None] * Q_BLOCK + ql[None, :, None] - ql[None, None, :]
    tt = jnp.moveaxis(_lookup(rel_bias, _rel_bucket(dist_t)), -1, 0)
    bias_a = jnp.concatenate([tt[:A_Q_HEADS, 1], tt[:A_Q_HEADS, 0]], axis=-1)
    tb = tt[A_Q_HEADS:]
    upper = (ql[None, :] > ql[:, None])[None]
    far = jnp.broadcast_to(rel_bias[REL_BUCKETS - 1, A_Q_HEADS:][:, None, None], tb[:, 0].shape)
    entries = ([jnp.where(upper, NEG_INF, tb[:, 0])] + [tb[:, d] for d in range(1, NEAR_TILES)]
               + [far, jnp.where(upper, tb[:, WIN_TILES - 1], NEG_INF), jnp.full_like(far, NEG_INF)])
    bias_tab = jnp.stack(entries, axis=1).reshape(B_KV_HEADS, B_GROUP, TAB_ENTRIES, Q_BLOCK, Q_BLOCK)
    bias_tab = jnp.transpose(bias_tab, (0, 2, 1, 3, 4))
    cend = jnp.arange(n_cmp_pad) * NSA_CMP_STRIDE + NSA_CMP_LEN - 1
    dist_c = (jnp.arange(nb)[:, None, None] * Q_BLOCK + ql[None, :, None]) - cend[None, None, :]
    bias_c = jnp.moveaxis(_lookup(rel_bias[:, A_Q_HEADS:], _rel_bucket(dist_c)), -1, 1)
    return bias_a, bias_tab, bias_c


def _overlap_t(seq):
    n_cmp_pad = seq // NSA_CMP_STRIDE
    n_sel = seq // NSA_SEL_LEN
    cstart = np.arange(n_cmp_pad) * NSA_CMP_STRIDE
    sstart = np.arange(n_sel) * NSA_SEL_LEN
    ov = (cstart[None, :] < sstart[:, None] + NSA_SEL_LEN) & (cstart[None, :] + NSA_CMP_LEN > sstart[:, None])
    ov[:, n_cmp_pad - 1] = False
    return jnp.asarray(ov.astype(np.float32), dtype=BF16)


def _dup(w, n_heads):
    d = w.shape[0]
    w = w.reshape(d, n_heads, 1, HEAD_DIM)
    return jnp.broadcast_to(w, (d, n_heads, 2, HEAD_DIM)).reshape(d, n_heads * PAIR)


def _even_weights(w_in):
    sizes = (A_Q_HEADS * HEAD_DIM, A_KV_HEADS * HEAD_DIM, A_KV_HEADS * HEAD_DIM, B_Q_HEADS * HEAD_DIM) \
        + (B_KV_HEADS * HEAD_DIM,) * 6 + (3 * B_Q_HEADS,)
    qa, ka, va, qb, kc, vc, ksl, vsl, kwn, vwn, gt = jnp.split(w_in, np.cumsum(sizes)[:-1].tolist(), axis=-1)
    main = jnp.concatenate([qa * ATTN_SCALE, _dup(ka, A_KV_HEADS), _dup(va, A_KV_HEADS), qb * ATTN_SCALE,
                            _dup(ksl, B_KV_HEADS), _dup(vsl, B_KV_HEADS),
                            _dup(kwn, B_KV_HEADS), _dup(vwn, B_KV_HEADS)], axis=-1).astype(BF16)
    d = w_in.shape[0]
    per_group = 3 * B_GROUP
    gates = [jnp.pad(gt[:, g * per_group:(g + 1) * per_group], ((0, 0), (0, PAIR - per_group)))
             for g in range(B_KV_HEADS)]
    tail = jnp.concatenate([kc, vc] + gates, axis=-1).astype(BF16)
    return main, tail


def _odd_weights(w_in):
    c_mix = C_HEADS * HEAD_DIM
    main = jnp.concatenate([w_in[:, :c_mix] * ATTN_SCALE, w_in[:, c_mix:3 * c_mix]], axis=-1).astype(BF16)
    tail = jnp.pad(w_in[:, 3 * c_mix:], ((0, 0), (0, PAIR - C_HEADS))).astype(BF16)
    return main, tail


def kernel(x, rel_bias, norm_mix, norm_ffn, norm_final, w_in_even, w_out_even, a_sinks, nsa_pe_k, nsa_pe_v,
           nsa_cmp_k_w1, nsa_cmp_k_w2, nsa_cmp_v_w1, nsa_cmp_v_w2, w_in_odd, w_out_odd, fox_fgate_b,
           w_ffn_up, w_ffn_down):
    batch, seq, d = x.shape
    depth = norm_mix.shape[0]
    m = batch * seq
    xf = x.reshape(m, d)
    bias_a, bias_tab, bias_c = _bias_tables(rel_bias, seq)
    ovt = _overlap_t(seq)
    n_cmp_pad = seq // NSA_CMP_STRIDE
    g_final = norm_final.reshape(1, d)

    for layer in range(depth):
        g_mix = norm_mix[layer].reshape(1, d)
        if layer % 2 == 0:
            e = layer // 2
            w_main, w_tail = _even_weights(w_in_even[e])
            qkv = norm_matmul(xf, g_mix, w_main, BF16)
            tail = norm_matmul(xf, g_mix, w_tail, F32)
            a_out = swa_attention(qkv, bias_a, a_sinks[e].reshape(A_Q_HEADS, 1, 1), batch, seq)
            r = tail[:, :2 * B_KV_HEADS * HEAD_DIM].reshape(batch, seq, 2 * B_KV_HEADS, HEAD_DIM)
            r = jnp.transpose(r, (0, 2, 1, 3)).reshape(batch, 2 * B_KV_HEADS, n_cmp_pad, NSA_CMP_STRIDE * HEAD_DIM)
            pe = jnp.stack([nsa_pe_k[e].reshape(1, -1), nsa_pe_v[e].reshape(1, -1)])
            w1 = jnp.stack([nsa_cmp_k_w1[e], nsa_cmp_v_w1[e]]).astype(BF16)
            w2 = jnp.stack([_dup(nsa_cmp_k_w2[e], 1), _dup(nsa_cmp_v_w2[e], 1)]).astype(BF16)
            cmp_kv = nsa_compress(r, pe, w1, w2)
            b_out = nsa_attention(qkv, cmp_kv, tail, bias_c, bias_tab, ovt, batch, seq)
            xf = out_proj_residual(a_out, 0, b_out, 0, w_out_even[e].astype(BF16), xf)
        else:
            o = layer // 2
            w_main, w_tail = _odd_weights(w_in_odd[o])
            qkv = norm_matmul(xf, g_mix, w_main, BF16, tn=1024)
            f_tail = norm_matmul(xf, g_mix, w_tail, F32)
            f_bias = jnp.pad(fox_fgate_b[o], (0, PAIR - C_HEADS)).reshape(1, PAIR)
            decay = fox_decay(f_tail, f_bias, batch, seq)
            c_out = fox_attention(qkv, decay, batch, seq)
            xf = out_proj_residual(c_out, 0, c_out, 1, w_out_odd[o].astype(BF16), xf)
        xf = ffn_residual(xf, norm_ffn[layer].reshape(1, d), w_ffn_up[layer].astype(BF16),
                          w_ffn_down[layer].astype(BF16), g_final, layer == depth - 1)
    return xf.reshape(batch, seq, d)
```

```python
import functools
import math

import jax
import jax.numpy as jnp
import numpy as np
from jax import lax
from jax.experimental import pallas as pl
from jax.experimental.pallas import tpu as pltpu

F32 = jnp.float32
BF16 = jnp.bfloat16

D_MODEL = 2048
HEAD_DIM = 64
PAIR = 2 * HEAD_DIM
A_Q_HEADS = 16
A_KV_HEADS = 4
A_WINDOW = 128
B_Q_HEADS = 16
B_KV_HEADS = 2
B_GROUP = B_Q_HEADS // B_KV_HEADS
A_GROUP = A_Q_HEADS // A_KV_HEADS
NSA_CMP_LEN = 32
NSA_CMP_STRIDE = 16
NSA_CMP_HIDDEN = 4 * HEAD_DIM
NSA_SEL_LEN = 64
NSA_TOP_N = 16
NSA_WINDOW = 512
NSA_FORCE_SCORE = 1.0e4
C_HEADS = 32
D_FF = 4 * D_MODEL
REL_BUCKETS = 32
REL_MAX_DIST = 1024
Q_BLOCK = 128
RMS_EPS = 1e-6
ATTN_SCALE = HEAD_DIM ** -0.5
LOG2E = 1.4426950408889634
NEAR_TILES = 8
NEG_INF = float("-inf")
ROW_CHUNK = 32

VMEM_LIMIT_BYTES = 56 * 1024 * 1024


def _cparams(*sem):
    return pltpu.CompilerParams(dimension_semantics=sem, vmem_limit_bytes=VMEM_LIMIT_BYTES)


def _nt(a, b):
    return lax.dot_general(a, b, (((1,), (1,)), ((), ())), preferred_element_type=F32)


def _split3(x):
    hi = x.astype(BF16)
    r1 = x - hi.astype(F32)
    mid = r1.astype(BF16)
    lo = (r1 - mid.astype(F32)).astype(BF16)
    return hi, mid, lo


def _rms(x, g):
    ms = jnp.mean(x * x, axis=-1, keepdims=True)
    return x * lax.rsqrt(ms + RMS_EPS) * g


def _stack_pairs(q_ref, first_pair, n_pairs):
    lane = lax.broadcasted_iota(jnp.int32, (Q_BLOCK, PAIR), 1)
    lo = lane < HEAD_DIM
    parts = []
    for p in range(n_pairs):
        qp = q_ref[:, (first_pair + p) * PAIR:(first_pair + p + 1) * PAIR]
        parts.append(jnp.where(lo, qp, jnp.zeros_like(qp)))
        parts.append(jnp.where(lo, jnp.zeros_like(qp), qp))
    return jnp.concatenate(parts, axis=0)


def _unstack_pairs(o, n_pairs, rows):
    lane = lax.broadcasted_iota(jnp.int32, (rows, PAIR), 1)
    lo = lane < HEAD_DIM
    return [jnp.where(lo, o[(2 * p) * rows:(2 * p + 1) * rows], o[(2 * p + 1) * rows:(2 * p + 2) * rows])
            for p in range(n_pairs)]


def _norm_mm_kernel(x_ref, g_ref, w_ref, o_ref, h_ref):
    @pl.when(pl.program_id(1) == 0)
    def _():
        h_ref[...] = _rms(x_ref[...], g_ref[...]).astype(BF16)

    o_ref[...] = jnp.dot(h_ref[...], w_ref[...], preferred_element_type=F32).astype(o_ref.dtype)


def norm_matmul(x, g, w, out_dtype, tm=1024, tn=1024):
    m, d = x.shape
    n = w.shape[1]
    tn = min(tn, n)
    return pl.pallas_call(
        _norm_mm_kernel,
        grid=(m // tm, n // tn),
        in_specs=[pl.BlockSpec((tm, d), lambda i, j: (i, 0)),
                  pl.BlockSpec((1, d), lambda i, j: (0, 0)),
                  pl.BlockSpec((d, tn), lambda i, j: (0, j))],
        out_specs=pl.BlockSpec((tm, tn), lambda i, j: (i, j)),
        out_shape=jax.ShapeDtypeStruct((m, n), out_dtype),
        scratch_shapes=[pltpu.VMEM((tm, d), BF16)],
        compiler_params=_cparams("parallel", "arbitrary"),
        name="norm_matmul",
    )(x, g, w)


def _out_proj_kernel(a1_ref, a2_ref, w1_ref, w2_ref, x_ref, o_ref):
    y = jnp.dot(a1_ref[...], w1_ref[...], preferred_element_type=F32)
    y = y + jnp.dot(a2_ref[...], w2_ref[...], preferred_element_type=F32)
    o_ref[...] = x_ref[...] + y


def out_proj_residual(a1, a1_blk, a2, a2_blk, w, x, tm=512, tn=1024):
    m, d = x.shape
    half = d // 2
    return pl.pallas_call(
        _out_proj_kernel,
        grid=(m // tm, d // tn),
        in_specs=[pl.BlockSpec((tm, half), lambda i, j: (i, a1_blk)),
                  pl.BlockSpec((tm, half), lambda i, j: (i, a2_blk)),
                  pl.BlockSpec((half, tn), lambda i, j: (0, j)),
                  pl.BlockSpec((half, tn), lambda i, j: (1, j)),
                  pl.BlockSpec((tm, tn), lambda i, j: (i, j))],
        out_specs=pl.BlockSpec((tm, tn), lambda i, j: (i, j)),
        out_shape=jax.ShapeDtypeStruct((m, d), F32),
        compiler_params=_cparams("parallel", "parallel"),
        name="out_proj",
    )(a1, a2, w, w, x)


def _ffn_kernel(x_ref, g_ref, wu_ref, wd_ref, gf_ref, o_ref, h_ref, acc_ref, *, final_norm):
    k = pl.program_id(1)

    @pl.when(k == 0)
    def _():
        h_ref[...] = _rms(x_ref[...], g_ref[...]).astype(BF16)
        acc_ref[...] = jnp.zeros_like(acc_ref)

    u = jnp.dot(h_ref[...], wu_ref[...], preferred_element_type=F32)
    u = jnp.maximum(u, 0.0)
    acc_ref[...] += jnp.dot((u * u).astype(BF16), wd_ref[...], preferred_element_type=F32)

    @pl.when(k == pl.num_programs(1) - 1)
    def _():
        y = x_ref[...] + acc_ref[...]
        if final_norm:
            y = _rms(y, gf_ref[...])
        o_ref[...] = y


def ffn_residual(x, g, w_up, w_down, g_final, final_norm, tm=512, tf=512):
    m, d = x.shape
    ff = w_up.shape[1]
    return pl.pallas_call(
        functools.partial(_ffn_kernel, final_norm=final_norm),
        grid=(m // tm, ff // tf),
        in_specs=[pl.BlockSpec((tm, d), lambda i, k: (i, 0)),
                  pl.BlockSpec((1, d), lambda i, k: (0, 0)),
                  pl.BlockSpec((d, tf), lambda i, k: (0, k)),
                  pl.BlockSpec((tf, d), lambda i, k: (k, 0)),
                  pl.BlockSpec((1, d), lambda i, k: (0, 0))],
        out_specs=pl.BlockSpec((tm, d), lambda i, k: (i, 0)),
        out_shape=jax.ShapeDtypeStruct((m, d), F32),
        scratch_shapes=[pltpu.VMEM((tm, d), BF16), pltpu.VMEM((tm, d), F32)],
        compiler_params=_cparams("parallel", "arbitrary"),
        name="ffn",
    )(x, g, w_up, w_down, g_final)


def _swa_kernel(q_ref, kp_ref, kc_ref, vp_ref, vc_ref, bias_ref, sink_ref, o_ref):
    i = pl.program_id(1)
    nk = 2 * Q_BLOCK
    ql = lax.broadcasted_iota(jnp.int32, (Q_BLOCK, nk), 0)
    kl = lax.broadcasted_iota(jnp.int32, (Q_BLOCK, nk), 1)
    dist = ql + Q_BLOCK - kl
    valid = (dist >= 0) & (dist < A_WINDOW) & ((kl >= Q_BLOCK) | (i > 0))
    rows = A_GROUP * Q_BLOCK
    for g in range(A_KV_HEADS):
        qs = _stack_pairs(q_ref, g * (A_GROUP // 2), A_GROUP // 2)
        k = jnp.concatenate([kp_ref[:, g * PAIR:(g + 1) * PAIR], kc_ref[:, g * PAIR:(g + 1) * PAIR]], axis=0)
        v = jnp.concatenate([vp_ref[:, g * PAIR:(g + 1) * PAIR], vc_ref[:, g * PAIR:(g + 1) * PAIR]], axis=0)
        s = _nt(qs, k).reshape(A_GROUP, Q_BLOCK, nk) + bias_ref[g * A_GROUP:(g + 1) * A_GROUP]
        s = jnp.where(valid[None], s, NEG_INF)
        sink = sink_ref[g * A_GROUP:(g + 1) * A_GROUP]
        m = jnp.maximum(jnp.max(s, axis=-1, keepdims=True), sink)
        e = jnp.exp(s - m)
        denom = jnp.sum(e, axis=-1, keepdims=True) + jnp.exp(sink - m)
        o = jnp.dot(e.reshape(rows, nk).astype(BF16), v, preferred_element_type=F32)
        o = o / denom.reshape(rows, 1)
        for p, blk in enumerate(_unstack_pairs(o, A_GROUP // 2, Q_BLOCK)):
            c0 = (g * (A_GROUP // 2) + p) * PAIR
            o_ref[:, c0:c0 + PAIR] = blk.astype(o_ref.dtype)


def swa_attention(qkv, bias_a, sinks, batch, seq):
    nb = seq // Q_BLOCK
    m = batch * seq
    qa_w = A_Q_HEADS * HEAD_DIM
    kv_w = A_KV_HEADS * PAIR
    k_blk = qa_w // kv_w
    v_blk = k_blk + 1
    row = lambda b, i: b * nb + i
    prev = lambda b, i: b * nb + jnp.maximum(i - 1, 0)
    return pl.pallas_call(
        _swa_kernel,
        grid=(batch, nb),
        in_specs=[pl.BlockSpec((Q_BLOCK, qa_w), lambda b, i: (row(b, i), 0)),
                  pl.BlockSpec((Q_BLOCK, kv_w), lambda b, i: (prev(b, i), k_blk)),
                  pl.BlockSpec((Q_BLOCK, kv_w), lambda b, i: (row(b, i), k_blk)),
                  pl.BlockSpec((Q_BLOCK, kv_w), lambda b, i: (prev(b, i), v_blk)),
                  pl.BlockSpec((Q_BLOCK, kv_w), lambda b, i: (row(b, i), v_blk)),
                  pl.BlockSpec((A_Q_HEADS, Q_BLOCK, 2 * Q_BLOCK), lambda b, i: (0, 0, 0)),
                  pl.BlockSpec((A_Q_HEADS, 1, 1), lambda b, i: (0, 0, 0))],
        out_specs=pl.BlockSpec((Q_BLOCK, qa_w), lambda b, i: (row(b, i), 0)),
        out_shape=jax.ShapeDtypeStruct((m, qa_w), BF16),
        compiler_params=_cparams("parallel", "parallel"),
        name="swa",
    )(qkv, qkv, qkv, qkv, qkv, bias_a, sinks)


def _compress_kernel(r_ref, pe_ref, w1_ref, w2_ref, o_ref):
    half = NSA_CMP_STRIDE * HEAD_DIM
    r = r_ref[0, 0]
    xa = (r + pe_ref[0, :, :half]).astype(BF16)
    xb = (r + pe_ref[0, :, half:]).astype(BF16)
    a = jnp.dot(xa, w1_ref[0, :half, :], preferred_element_type=F32)
    b = jnp.dot(xb, w1_ref[0, half:, :], preferred_element_type=F32)
    n = r.shape[0]
    hid = jax.nn.gelu(a + pltpu.roll(b, n - 1, 0))
    o_ref[0, 0] = jnp.dot(hid.astype(BF16), w2_ref[0], preferred_element_type=F32).astype(o_ref.dtype)


def nsa_compress(r, pe, w1, w2dup):
    batch, _, n, width = r.shape
    return pl.pallas_call(
        _compress_kernel,
        grid=(batch, 2 * B_KV_HEADS),
        in_specs=[pl.BlockSpec((1, 1, n, width), lambda b, w: (b, w, 0, 0)),
                  pl.BlockSpec((1, 1, 2 * width), lambda b, w: (w // B_KV_HEADS, 0, 0)),
                  pl.BlockSpec((1, 2 * width, NSA_CMP_HIDDEN), lambda b, w: (w // B_KV_HEADS, 0, 0)),
                  pl.BlockSpec((1, NSA_CMP_HIDDEN, PAIR), lambda b, w: (w // B_KV_HEADS, 0, 0))],
        out_specs=pl.BlockSpec((1, 1, n, PAIR), lambda b, w: (b, w, 0, 0)),
        out_shape=jax.ShapeDtypeStruct((batch, 2 * B_KV_HEADS, n, PAIR), BF16),
        compiler_params=_cparams("parallel", "parallel"),
        name="nsa_compress",
    )(r, pe, w1, w2dup)


NSA_KEY_TILE = 512
NSA_STREAMS = 2
WIN_TILES = NSA_WINDOW // Q_BLOCK + 1
TAB_FAR = NEAR_TILES
TAB_WIN_EDGE = NEAR_TILES + 1
TAB_NONE = NEAR_TILES + 2
TAB_ENTRIES = NEAR_TILES + 3
UNSELECTED = -1.0e30


def _nsa_kernel(q_ref, kcm_ref, vcm_ref, ks_ref, vs_ref, kw_ref, vw_ref, gate_ref, bias_c_ref, tab_ref, ovt_ref,
                o_ref, val_ref, s_ref, p_ref, m_ref, l_ref, a_ref, pm_ref, acc_ref, *, n_cmp_pad):
    i = pl.program_id(2)
    rows = B_GROUP * Q_BLOCK
    srows = rows // NSA_STREAMS
    heads_per_stream = B_GROUP // NSA_STREAMS
    chunks_per_head = Q_BLOCK // ROW_CHUNK
    n_sel = val_ref.shape[0]
    sel_shift = int(math.log2(NSA_SEL_LEN))
    qs = _stack_pairs(q_ref, 0, B_GROUP // 2)

    ql_c = lax.broadcasted_iota(jnp.int32, (Q_BLOCK, n_cmp_pad), 0) + i * Q_BLOCK
    c_id = lax.broadcasted_iota(jnp.int32, (Q_BLOCK, n_cmp_pad), 1)
    valid_c = (c_id * NSA_CMP_STRIDE + (NSA_CMP_LEN - 1) <= ql_c) & (c_id < n_cmp_pad - 1)
    s_c = _nt(qs, kcm_ref[0, 0]).reshape(B_GROUP, Q_BLOCK, n_cmp_pad) + bias_c_ref[0]
    s_c = jnp.where(valid_c[None], s_c, NEG_INF)
    m_c = jnp.max(s_c, axis=-1, keepdims=True)
    m_c = jnp.where(m_c == NEG_INF, 0.0, m_c)
    e_c = jnp.exp2(s_c - m_c)
    p_c = e_c / jnp.maximum(jnp.sum(e_c, axis=-1, keepdims=True), 1e-30)
    o_c = jnp.dot(p_c.reshape(rows, n_cmp_pad).astype(BF16), vcm_ref[0, 0], preferred_element_type=F32)

    p_sum = jnp.sum(p_c, axis=0)
    ovt = ovt_ref[...]
    imp = None
    for piece in _split3(p_sum):
        t = _nt(ovt, piece)
        imp = t if imp is None else imp + t
    blk = lax.broadcasted_iota(jnp.int32, (n_sel, Q_BLOCK), 0)
    qpos = lax.broadcasted_iota(jnp.int32, (n_sel, Q_BLOCK), 1) + i * Q_BLOCK
    cur = lax.shift_right_logical(qpos, sel_shift)
    forced = (blk == 0) | (blk == cur) | (blk == cur - 1)
    future = blk * NSA_SEL_LEN > qpos
    val = jnp.where(future, NEG_INF, jnp.where(forced, NSA_FORCE_SCORE, imp))
    val_ref[...] = val
    rank = jnp.zeros((n_sel, Q_BLOCK), F32)
    for s2 in range(n_sel):
        other = val_ref[s2:s2 + 1, :]
        rank = rank + jnp.where(blk > s2, jnp.where(other >= val, 1.0, 0.0), jnp.where(other > val, 1.0, 0.0))
    sel_t = jnp.where((rank < float(NSA_TOP_N)) & (val > NEG_INF), 1.0, 0.0).astype(BF16)
    if n_sel < Q_BLOCK:
        sel_t = jnp.concatenate([sel_t, jnp.zeros((Q_BLOCK - n_sel, Q_BLOCK), BF16)], axis=0)
    eye = jnp.where(lax.broadcasted_iota(jnp.int32, (Q_BLOCK, Q_BLOCK), 0)
                    == lax.broadcasted_iota(jnp.int32, (Q_BLOCK, Q_BLOCK), 1), 1.0, 0.0).astype(BF16)
    unsel = jnp.where(_nt(eye, sel_t) > 0.5, 0.0, UNSELECTED).astype(BF16)
    unsel_rows = jnp.concatenate([unsel] * heads_per_stream, axis=0)
    qs_st = [qs[st * srows:(st + 1) * srows] for st in range(NSA_STREAMS)]
    qs_sel = [jnp.concatenate([qs_st[st], unsel_rows], axis=1) for st in range(NSA_STREAMS)]

    key_lane = lax.broadcasted_iota(jnp.int32, (NSA_KEY_TILE, PAIR), 1)
    key_blk = lax.shift_right_logical(lax.broadcasted_iota(jnp.int32, (NSA_KEY_TILE, PAIR), 0), sel_shift)
    lane_minus_blk = key_lane - key_blk

    def reset():
        m_ref[...] = jnp.full(m_ref.shape, NEG_INF, F32)
        l_ref[...] = jnp.zeros(l_ref.shape, F32)
        acc_ref[...] = jnp.zeros(acc_ref.shape, F32)

    def soft_pv(st, tab_idx, v):
        width = len(tab_idx) * Q_BLOCK

        def raw(c, u):
            return s_ref[st, c * ROW_CHUNK:(c + 1) * ROW_CHUNK, u * Q_BLOCK:(u + 1) * Q_BLOCK]

        def biased(c, u):
            head = st * heads_per_stream + c // chunks_per_head
            q0 = (c % chunks_per_head) * ROW_CHUNK
            return raw(c, u) + tab_ref[0, tab_idx[u], head, q0:q0 + ROW_CHUNK, :]

        def keep(c, u, s):
            s_ref[st, c * ROW_CHUNK:(c + 1) * ROW_CHUNK, u * Q_BLOCK:(u + 1) * Q_BLOCK] = s

        _softmax_tile(biased, srows // ROW_CHUNK, ROW_CHUNK, len(tab_idx), m_ref.at[st], l_ref.at[st],
                      a_ref.at[st], pm_ref.at[st], p_ref.at[st], keep=keep, kept_tiles=raw)
        acc_ref[st] = a_ref[st] * acc_ref[st] + jnp.dot(p_ref[st, :, :width], v, preferred_element_type=F32)

    def finish():
        return jnp.concatenate(
            [acc_ref[st] / jnp.maximum(jnp.sum(l_ref[st], axis=-1, keepdims=True), 1e-30)
             for st in range(NSA_STREAMS)], axis=0)

    blocks_per_tile = NSA_KEY_TILE // NSA_SEL_LEN
    lane_tiles_per_tile = NSA_KEY_TILE // Q_BLOCK

    def sel_scores(st, jt):
        off = pl.multiple_of(jt * NSA_KEY_TILE, NSA_KEY_TILE)
        block_onehot = jnp.where(lane_minus_blk == jt * blocks_per_tile, 1.0, 0.0).astype(BF16)
        k_aug = jnp.concatenate([ks_ref[pl.ds(off, NSA_KEY_TILE), :], block_onehot], axis=1)
        s_ref[st, :, :NSA_KEY_TILE] = _nt(qs_sel[st], k_aug)

    def sel_tile(jt, last):
        v = vs_ref[pl.ds(pl.multiple_of(jt * NSA_KEY_TILE, NSA_KEY_TILE), NSA_KEY_TILE), :]
        tab_idx = []
        for u in range(lane_tiles_per_tile):
            d = i - (jt * lane_tiles_per_tile + u)
            tab_idx.append(jnp.where(d < 0, TAB_NONE, jnp.minimum(d, TAB_FAR)))
        for st in range(NSA_STREAMS):
            soft_pv(st, tab_idx, v)
            if not last:
                sel_scores(st, jt + 1)

    def sel_body(jt, carry):
        sel_tile(jt, False)
        return carry

    n_tiles = lax.shift_right_logical(i + lane_tiles_per_tile, int(math.log2(lane_tiles_per_tile)))
    reset()
    for st in range(NSA_STREAMS):
        sel_scores(st, 0)
    lax.fori_loop(0, n_tiles - 1, sel_body, 0)
    sel_tile(n_tiles - 1, True)
    o_s = finish()

    first_blk = jnp.maximum(i - (WIN_TILES - 1), 0)
    off = pl.multiple_of(first_blk * Q_BLOCK, Q_BLOCK)
    k_win = kw_ref[pl.ds(off, WIN_TILES * Q_BLOCK), :]
    v_win = vw_ref[pl.ds(off, WIN_TILES * Q_BLOCK), :]
    tab_idx = []
    for u in range(WIN_TILES):
        d = i - (first_blk + u)
        tab_idx.append(jnp.where(d < 0, TAB_NONE, jnp.where(d == WIN_TILES - 1, TAB_WIN_EDGE, d)))
    reset()
    for st in range(NSA_STREAMS):
        s_ref[st] = _nt(qs_st[st], k_win)
    for st in range(NSA_STREAMS):
        soft_pv(st, tab_idx, v_win)
    o_w = finish()

    gates = jax.nn.sigmoid(gate_ref[...])

    def gate_col(br):
        return jnp.concatenate([gates[:, 3 * r + br:3 * r + br + 1] for r in range(B_GROUP)], axis=0)

    o = gate_col(0) * o_c + gate_col(1) * o_s + gate_col(2) * o_w
    for p, blk_out in enumerate(_unstack_pairs(o, B_GROUP // 2, Q_BLOCK)):
        o_ref[:, p * PAIR:(p + 1) * PAIR] = blk_out.astype(o_ref.dtype)


def nsa_attention(qkv, cmp_kv, tail, bias_c, bias_tab, ovt, batch, seq):
    nb = seq // Q_BLOCK
    m = batch * seq
    n_cmp_pad = seq // NSA_CMP_STRIDE
    n_sel = seq // NSA_SEL_LEN
    assert n_sel <= Q_BLOCK and seq % NSA_KEY_TILE == 0 and seq >= WIN_TILES * Q_BLOCK
    grp_w = B_GROUP * HEAD_DIM
    q_blk0 = (A_Q_HEADS * HEAD_DIM + 2 * A_KV_HEADS * PAIR) // grp_w
    kv_blk0 = (A_Q_HEADS * HEAD_DIM + 2 * A_KV_HEADS * PAIR + B_Q_HEADS * HEAD_DIM) // PAIR
    kv_spec = lambda t: pl.BlockSpec((seq, PAIR), lambda b, g, i: (b, kv_blk0 + t * B_KV_HEADS + g))
    srows = B_GROUP * Q_BLOCK // NSA_STREAMS
    s_width = WIN_TILES * Q_BLOCK
    return pl.pallas_call(
        functools.partial(_nsa_kernel, n_cmp_pad=n_cmp_pad),
        grid=(batch, B_KV_HEADS, nb),
        in_specs=[pl.BlockSpec((Q_BLOCK, grp_w), lambda b, g, i: (b * nb + i, q_blk0 + g)),
                  pl.BlockSpec((1, 1, n_cmp_pad, PAIR), lambda b, g, i: (b, g, 0, 0)),
                  pl.BlockSpec((1, 1, n_cmp_pad, PAIR), lambda b, g, i: (b, B_KV_HEADS + g, 0, 0)),
                  kv_spec(0), kv_spec(1), kv_spec(2), kv_spec(3),
                  pl.BlockSpec((Q_BLOCK, PAIR), lambda b, g, i: (b * nb + i, 2 + g)),
                  pl.BlockSpec((1, B_GROUP, Q_BLOCK, n_cmp_pad), lambda b, g, i: (i, g, 0, 0)),
                  pl.BlockSpec((1, TAB_ENTRIES, B_GROUP, Q_BLOCK, Q_BLOCK), lambda b, g, i: (g, 0, 0, 0, 0)),
                  pl.BlockSpec((n_sel, n_cmp_pad), lambda b, g, i: (0, 0))],
        out_specs=pl.BlockSpec((Q_BLOCK, grp_w), lambda b, g, i: (b * nb + i, g)),
        out_shape=jax.ShapeDtypeStruct((m, B_Q_HEADS * HEAD_DIM), BF16),
        scratch_shapes=[pltpu.VMEM((n_sel, Q_BLOCK), F32),
                        pltpu.VMEM((NSA_STREAMS, srows, s_width), F32),
                        pltpu.VMEM((NSA_STREAMS, srows, s_width), BF16)]
        + [pltpu.VMEM((NSA_STREAMS, srows, PAIR), F32)] * 5,
        compiler_params=_cparams("parallel", "parallel", "arbitrary"),
        name="nsa",
    )(qkv, cmp_kv, cmp_kv, qkv, qkv, qkv, qkv, tail, bias_c, bias_tab, ovt)


DECAY_STEP = 512
FOX_SPLIT = 1


def _decay_kernel(f_ref, fb_ref, place_ref, o_ref, carry_ref):
    r_i = lax.broadcasted_iota(jnp.int32, (Q_BLOCK, Q_BLOCK), 0)
    c_i = lax.broadcasted_iota(jnp.int32, (Q_BLOCK, Q_BLOCK), 1)
    tri = jnp.where(c_i <= r_i, 1.0, 0.0).astype(BF16)

    @pl.when(pl.program_id(1) == 0)
    def _():
        carry_ref[...] = jnp.zeros_like(carry_ref)

    carry = carry_ref[...]
    for r in range(DECAY_STEP // Q_BLOCK):
        x = f_ref[r * Q_BLOCK:(r + 1) * Q_BLOCK, :] + fb_ref[...]
        ls = jax.nn.log_sigmoid(x)
        cs = jnp.broadcast_to(carry, ls.shape)
        for piece in _split3(ls):
            cs = cs + jnp.dot(tri, piece, preferred_element_type=F32)
        feat = None
        for n, piece in enumerate(_split3(cs * LOG2E)):
            term = jnp.dot(piece, place_ref[n], preferred_element_type=F32)
            feat = term if feat is None else feat + term
        o_ref[r * Q_BLOCK:(r + 1) * Q_BLOCK, :] = feat.astype(o_ref.dtype)
        carry = cs[Q_BLOCK - 1:Q_BLOCK, :]
    carry_ref[...] = carry


def _decay_placement():
    place = np.zeros((3, PAIR, (C_HEADS // 2) * PAIR), np.float32)
    for h in range(C_HEADS):
        for n in range(3):
            place[n, h, (h // 2) * PAIR + 3 * (h % 2) + n] = -1.0
    return jnp.asarray(place, dtype=BF16)


def fox_decay(f_tail, f_bias, batch, seq):
    steps = seq // DECAY_STEP
    width = (C_HEADS // 2) * PAIR
    return pl.pallas_call(
        _decay_kernel,
        grid=(batch, steps),
        in_specs=[pl.BlockSpec((DECAY_STEP, PAIR), lambda b, s: (b * steps + s, 0)),
                  pl.BlockSpec((1, PAIR), lambda b, s: (0, 0)),
                  pl.BlockSpec((3, PAIR, width), lambda b, s: (0, 0, 0))],
        out_specs=pl.BlockSpec((DECAY_STEP, width), lambda b, s: (b * steps + s, 0)),
        out_shape=jax.ShapeDtypeStruct((batch * seq, width), BF16),
        scratch_shapes=[pltpu.VMEM((1, PAIR), F32)],
        compiler_params=_cparams("parallel", "arbitrary"),
        name="fox_decay",
    )(f_tail, f_bias, _decay_placement())


def _softmax_tile(lane_tiles, n_chunks, rc, n_lane_tiles, m_ref, l_ref, a_ref, pm_ref, p_ref, keep=None,
                  kept_tiles=None):
    second = lane_tiles if kept_tiles is None else kept_tiles
    for c in range(n_chunks):
        pm = None
        for u in range(n_lane_tiles):
            s = lane_tiles(c, u)
            if keep is not None:
                keep(c, u, s)
            pm = s if pm is None else jnp.maximum(pm, s)
        pm_ref[c * rc:(c + 1) * rc, :] = pm
    m_old = m_ref[...]
    m_new = jnp.maximum(m_old, jnp.max(pm_ref[...], axis=-1, keepdims=True))
    a_ref[...] = jnp.exp2(m_old - m_new)
    m_ref[...] = m_new
    for c in range(n_chunks):
        rows = slice(c * rc, (c + 1) * rc)
        mb = m_ref[rows, :]
        psum = None
        for u in range(n_lane_tiles):
            p = jnp.exp2(second(c, u) - mb)
            p_ref[rows, u * Q_BLOCK:(u + 1) * Q_BLOCK] = p.astype(BF16)
            psum = p if psum is None else psum + p
        l_ref[rows, :] = a_ref[rows, :] * l_ref[rows, :] + psum


def _fox_kernel(q_ref, k_ref, v_ref, e_ref, o_ref, s_ref, p_ref, m_ref, l_ref, a_ref, pm_ref, acc_ref, *, tq):
    t = pl.program_id(2)
    kw = tq
    n_lane_tiles = kw // Q_BLOCK
    srows = tq // FOX_SPLIT
    n_chunks = srows // ROW_CHUNK
    n_streams = 2 * FOX_SPLIT
    q = q_ref[...]
    lane = lax.broadcasted_iota(jnp.int32, (tq, PAIR), 1)
    lo = lane < HEAD_DIM
    zero = jnp.zeros_like(q)
    q_aug = [jnp.concatenate([jnp.where(lo if hh == 0 else ~lo, q, zero),
                              jnp.where((lane >= 3 * hh) & (lane < 3 * hh + 3), 1.0, 0.0).astype(BF16)], axis=1)
             for hh in range(2)]
    qs = [q_aug[st // FOX_SPLIT][(st % FOX_SPLIT) * srows:(st % FOX_SPLIT + 1) * srows] for st in range(n_streams)]
    m_ref[...] = jnp.full(m_ref.shape, NEG_INF, F32)
    l_ref[...] = jnp.zeros(l_ref.shape, F32)
    acc_ref[...] = jnp.zeros(acc_ref.shape, F32)
    col_minus_row = (lax.broadcasted_iota(jnp.int32, (ROW_CHUNK, Q_BLOCK), 1)
                     - lax.broadcasted_iota(jnp.int32, (ROW_CHUNK, Q_BLOCK), 0))

    def scores(st, j):
        off = pl.multiple_of(j * kw, kw)
        k_aug = jnp.concatenate([k_ref[pl.ds(off, kw), :], e_ref[pl.ds(off, kw), :]], axis=1)
        s_ref[st] = _nt(qs[st], k_aug)

    def tile(j, diagonal):
        v = v_ref[pl.ds(pl.multiple_of(j * kw, kw), kw), :]
        for st in range(n_streams):
            row0 = (st % FOX_SPLIT) * srows

            def lane_tiles(c, u):
                s = s_ref[st, c * ROW_CHUNK:(c + 1) * ROW_CHUNK, u * Q_BLOCK:(u + 1) * Q_BLOCK]
                if diagonal:
                    s = jnp.where(col_minus_row <= row0 + c * ROW_CHUNK - u * Q_BLOCK, s, NEG_INF)
                return s

            _softmax_tile(lane_tiles, n_chunks, ROW_CHUNK, n_lane_tiles, m_ref.at[st], l_ref.at[st], a_ref.at[st],
                          pm_ref.at[st], p_ref.at[st])
            acc_ref[st] = a_ref[st] * acc_ref[st] + jnp.dot(p_ref[st], v, preferred_element_type=F32)
            if not diagonal:
                scores(st, j + 1)

    def body(j, carry):
        tile(j, False)
        return carry

    for st in range(n_streams):
        scores(st, 0)
    lax.fori_loop(0, t, body, 0)
    tile(t, True)
    o = [jnp.concatenate([acc_ref[st] / jnp.maximum(jnp.sum(l_ref[st], axis=-1, keepdims=True), 1e-30)
                          for st in range(hh * FOX_SPLIT, (hh + 1) * FOX_SPLIT)], axis=0) for hh in range(2)]
    o_ref[...] = jnp.where(lo, o[0], o[1]).astype(o_ref.dtype)


def fox_attention(qkv, decay, batch, seq, tq=512):
    m = batch * seq
    n_pairs = C_HEADS // 2
    nt = seq // tq
    return pl.pallas_call(
        functools.partial(_fox_kernel, tq=tq),
        grid=(batch, n_pairs, nt),
        in_specs=[pl.BlockSpec((tq, PAIR), lambda b, h, t: (b * nt + t, h)),
                  pl.BlockSpec((seq, PAIR), lambda b, h, t: (b, n_pairs + h)),
                  pl.BlockSpec((seq, PAIR), lambda b, h, t: (b, 2 * n_pairs + h)),
                  pl.BlockSpec((seq, PAIR), lambda b, h, t: (b, h))],
        out_specs=pl.BlockSpec((tq, PAIR), lambda b, h, t: (b * nt + t, h)),
        out_shape=jax.ShapeDtypeStruct((m, C_HEADS * HEAD_DIM), BF16),
        scratch_shapes=[pltpu.VMEM((2 * FOX_SPLIT, tq // FOX_SPLIT, tq), F32),
                        pltpu.VMEM((2 * FOX_SPLIT, tq // FOX_SPLIT, tq), BF16)]
        + [pltpu.VMEM((2 * FOX_SPLIT, tq // FOX_SPLIT, PAIR), F32)] * 5,
        compiler_params=_cparams("parallel", "parallel", "arbitrary"),
        name="fox",
    )(qkv, qkv, qkv, decay)


def _rel_bucket(dist):
    n = jnp.maximum(dist, 0)
    max_exact = REL_BUCKETS // 2
    nf = jnp.maximum(n, 1).astype(jnp.float32)
    large = max_exact + (jnp.log(nf / max_exact) / math.log(REL_MAX_DIST / max_exact)
                         * (REL_BUCKETS - max_exact)).astype(jnp.int32)
    return jnp.where(n < max_exact, n, jnp.minimum(large, REL_BUCKETS - 1))


def _lookup(table, idx):
    onehot = (idx[..., None] == jnp.arange(table.shape[0])).astype(F32)
    return jnp.einsum("...n,nh->...h", onehot, table, precision=lax.Precision.HIGHEST)


def _bias_tables(rel_bias, seq):
    nb = seq // Q_BLOCK
    n_cmp_pad = seq // NSA_CMP_STRIDE
    ql = jnp.arange(Q_BLOCK)
    d = jnp.arange(NEAR_TILES)
    dist_t = d[:, None, None] * Q_BLOCK + ql[None, :, None] - ql[None, None, :]
    tt = jnp.moveaxis(_lookup(rel_bias, _rel_bucket(dist_t)), -1, 0)
    bias_a = jnp.concatenate([tt[:A_Q_HEADS, 1], tt[:A_Q_HEADS, 0]], axis=-1)
    tb = tt[A_Q_HEADS:]
    upper = (ql[None, :] > ql[:, None])[None]
    far = jnp.broadcast_to(rel_bias[REL_BUCKETS - 1, A_Q_HEADS:][:, None, None], tb[:, 0].shape)
    entries = ([jnp.where(upper, NEG_INF, tb[:, 0])] + [tb[:, d] for d in range(1, NEAR_TILES)]
               + [far, jnp.where(upper, tb[:, WIN_TILES - 1], NEG_INF), jnp.full_like(far, NEG_INF)])
    bias_tab = jnp.stack(entries, axis=1).reshape(B_KV_HEADS, B_GROUP, TAB_ENTRIES, Q_BLOCK, Q_BLOCK)
    bias_tab = jnp.transpose(bias_tab, (0, 2, 1, 3, 4)) * LOG2E
    cend = jnp.arange(n_cmp_pad) * NSA_CMP_STRIDE + NSA_CMP_LEN - 1
    dist_c = (jnp.arange(nb)[:, None, None] * Q_BLOCK + ql[None, :, None]) - cend[None, None, :]
    bias_c = jnp.moveaxis(_lookup(rel_bias[:, A_Q_HEADS:] * LOG2E, _rel_bucket(dist_c)), -1, 1)
    return bias_a, bias_tab, bias_c


def _overlap_t(seq):
    n_cmp_pad = seq // NSA_CMP_STRIDE
    n_sel = seq // NSA_SEL_LEN
    cstart = np.arange(n_cmp_pad) * NSA_CMP_STRIDE
    sstart = np.arange(n_sel) * NSA_SEL_LEN
    ov = (cstart[None, :] < sstart[:, None] + NSA_SEL_LEN) & (cstart[None, :] + NSA_CMP_LEN > sstart[:, None])
    ov[:, n_cmp_pad - 1] = False
    return jnp.asarray(ov.astype(np.float32), dtype=BF16)


def _dup(w, n_heads):
    d = w.shape[0]
    w = w.reshape(d, n_heads, 1, HEAD_DIM)
    return jnp.broadcast_to(w, (d, n_heads, 2, HEAD_DIM)).reshape(d, n_heads * PAIR)


def _even_weights(w_in):
    sizes = (A_Q_HEADS * HEAD_DIM, A_KV_HEADS * HEAD_DIM, A_KV_HEADS * HEAD_DIM, B_Q_HEADS * HEAD_DIM) \
        + (B_KV_HEADS * HEAD_DIM,) * 6 + (3 * B_Q_HEADS,)
    qa, ka, va, qb, kc, vc, ksl, vsl, kwn, vwn, gt = jnp.split(w_in, np.cumsum(sizes)[:-1].tolist(), axis=-1)
    main = jnp.concatenate([qa * ATTN_SCALE, _dup(ka, A_KV_HEADS), _dup(va, A_KV_HEADS), qb * (ATTN_SCALE * LOG2E),
                            _dup(ksl, B_KV_HEADS), _dup(vsl, B_KV_HEADS),
                            _dup(kwn, B_KV_HEADS), _dup(vwn, B_KV_HEADS)], axis=-1).astype(BF16)
    d = w_in.shape[0]
    per_group = 3 * B_GROUP
    gates = [jnp.pad(gt[:, g * per_group:(g + 1) * per_group], ((0, 0), (0, PAIR - per_group)))
             for g in range(B_KV_HEADS)]
    tail = jnp.concatenate([kc, vc] + gates, axis=-1).astype(BF16)
    return main, tail


def _odd_weights(w_in):
    c_mix = C_HEADS * HEAD_DIM
    main = jnp.concatenate([w_in[:, :c_mix] * (ATTN_SCALE * LOG2E), w_in[:, c_mix:3 * c_mix]],
                           axis=-1).astype(BF16)
    tail = jnp.pad(w_in[:, 3 * c_mix:], ((0, 0), (0, PAIR - C_HEADS))).astype(BF16)
    return main, tail


def kernel(x, rel_bias, norm_mix, norm_ffn, norm_final, w_in_even, w_out_even, a_sinks, nsa_pe_k, nsa_pe_v,
           nsa_cmp_k_w1, nsa_cmp_k_w2, nsa_cmp_v_w1, nsa_cmp_v_w2, w_in_odd, w_out_odd, fox_fgate_b,
           w_ffn_up, w_ffn_down):
    batch, seq, d = x.shape
    depth = norm_mix.shape[0]
    m = batch * seq
    xf = x.reshape(m, d)
    bias_a, bias_tab, bias_c = _bias_tables(rel_bias, seq)
    ovt = _overlap_t(seq)
    n_cmp_pad = seq // NSA_CMP_STRIDE
    g_final = norm_final.reshape(1, d)

    for layer in range(depth):
        g_mix = norm_mix[layer].reshape(1, d)
        if layer % 2 == 0:
            e = layer // 2
            w_main, w_tail = _even_weights(w_in_even[e])
            qkv = norm_matmul(xf, g_mix, w_main, BF16)
            tail = norm_matmul(xf, g_mix, w_tail, F32)
            a_out = swa_attention(qkv, bias_a, a_sinks[e].reshape(A_Q_HEADS, 1, 1), batch, seq)
            r = tail[:, :2 * B_KV_HEADS * HEAD_DIM].reshape(batch, seq, 2 * B_KV_HEADS, HEAD_DIM)
            r = jnp.transpose(r, (0, 2, 1, 3)).reshape(batch, 2 * B_KV_HEADS, n_cmp_pad, NSA_CMP_STRIDE * HEAD_DIM)
            pe = jnp.stack([nsa_pe_k[e].reshape(1, -1), nsa_pe_v[e].reshape(1, -1)])
            w1 = jnp.stack([nsa_cmp_k_w1[e], nsa_cmp_v_w1[e]]).astype(BF16)
            w2 = jnp.stack([_dup(nsa_cmp_k_w2[e], 1), _dup(nsa_cmp_v_w2[e], 1)]).astype(BF16)
            cmp_kv = nsa_compress(r, pe, w1, w2)
            b_out = nsa_attention(qkv, cmp_kv, tail, bias_c, bias_tab, ovt, batch, seq)
            xf = out_proj_residual(a_out, 0, b_out, 0, w_out_even[e].astype(BF16), xf)
        else:
            o = layer // 2
            w_main, w_tail = _odd_weights(w_in_odd[o])
            qkv = norm_matmul(xf, g_mix, w_main, BF16)
            f_tail = norm_matmul(xf, g_mix, w_tail, F32)
            f_bias = jnp.pad(fox_fgate_b[o], (0, PAIR - C_HEADS)).reshape(1, PAIR)
            decay = fox_decay(f_tail, f_bias, batch, seq)
            c_out = fox_attention(qkv, decay, batch, seq)
            xf = out_proj_residual(c_out, 0, c_out, 1, w_out_odd[o].astype(BF16), xf)
        xf = ffn_residual(xf, norm_ffn[layer].reshape(1, d), w_ffn_up[layer].astype(BF16),
                          w_ffn_down[layer].astype(BF16), g_final, layer == depth - 1)
    return xf.reshape(batch, seq, d)
```

```python
import functools
import math

import jax
import jax.numpy as jnp
import numpy as np
from jax import lax
from jax.experimental import pallas as pl
from jax.experimental.pallas import tpu as pltpu

F32 = jnp.float32
BF16 = jnp.bfloat16

D_MODEL = 2048
HEAD_DIM = 64
PAIR = 2 * HEAD_DIM
A_Q_HEADS = 16
A_KV_HEADS = 4
A_WINDOW = 128
B_Q_HEADS = 16
B_KV_HEADS = 2
B_GROUP = B_Q_HEADS // B_KV_HEADS
A_GROUP = A_Q_HEADS // A_KV_HEADS
NSA_CMP_LEN = 32
NSA_CMP_STRIDE = 16
NSA_CMP_HIDDEN = 4 * HEAD_DIM
NSA_SEL_LEN = 64
NSA_TOP_N = 16
NSA_WINDOW = 512
NSA_FORCE_SCORE = 1.0e4
C_HEADS = 32
D_FF = 4 * D_MODEL
REL_BUCKETS = 32
REL_MAX_DIST = 1024
Q_BLOCK = 128
RMS_EPS = 1e-6
ATTN_SCALE = HEAD_DIM ** -0.5
LOG2E = 1.4426950408889634
NEAR_TILES = 8
NEG_INF = float("-inf")
ROW_CHUNK = 32

VMEM_LIMIT_BYTES = 56 * 1024 * 1024


def _cparams(*sem):
    return pltpu.CompilerParams(dimension_semantics=sem, vmem_limit_bytes=VMEM_LIMIT_BYTES)


def _nt(a, b):
    return lax.dot_general(a, b, (((1,), (1,)), ((), ())), preferred_element_type=F32)


def _split3(x):
    hi = x.astype(BF16)
    r1 = x - hi.astype(F32)
    mid = r1.astype(BF16)
    lo = (r1 - mid.astype(F32)).astype(BF16)
    return hi, mid, lo


def _rms(x, g):
    ms = jnp.mean(x * x, axis=-1, keepdims=True)
    return x * lax.rsqrt(ms + RMS_EPS) * g


def _stack_pairs(q_ref, first_pair, n_pairs):
    lane = lax.broadcasted_iota(jnp.int32, (Q_BLOCK, PAIR), 1)
    lo = lane < HEAD_DIM
    parts = []
    for p in range(n_pairs):
        qp = q_ref[:, (first_pair + p) * PAIR:(first_pair + p + 1) * PAIR]
        parts.append(jnp.where(lo, qp, jnp.zeros_like(qp)))
        parts.append(jnp.where(lo, jnp.zeros_like(qp), qp))
    return jnp.concatenate(parts, axis=0)


def _unstack_pairs(o, n_pairs, rows):
    lane = lax.broadcasted_iota(jnp.int32, (rows, PAIR), 1)
    lo = lane < HEAD_DIM
    return [jnp.where(lo, o[(2 * p) * rows:(2 * p + 1) * rows], o[(2 * p + 1) * rows:(2 * p + 2) * rows])
            for p in range(n_pairs)]


def _norm_mm_kernel(x_ref, g_ref, w_ref, o_ref, h_ref):
    @pl.when(pl.program_id(1) == 0)
    def _():
        h_ref[...] = _rms(x_ref[...], g_ref[...]).astype(BF16)

    o_ref[...] = jnp.dot(h_ref[...], w_ref[...], preferred_element_type=F32).astype(o_ref.dtype)


def norm_matmul(x, g, w, out_dtype, tm=1024, tn=1024):
    m, d = x.shape
    n = w.shape[1]
    tn = min(tn, n)
    return pl.pallas_call(
        _norm_mm_kernel,
        grid=(m // tm, n // tn),
        in_specs=[pl.BlockSpec((tm, d), lambda i, j: (i, 0)),
                  pl.BlockSpec((1, d), lambda i, j: (0, 0)),
                  pl.BlockSpec((d, tn), lambda i, j: (0, j))],
        out_specs=pl.BlockSpec((tm, tn), lambda i, j: (i, j)),
        out_shape=jax.ShapeDtypeStruct((m, n), out_dtype),
        scratch_shapes=[pltpu.VMEM((tm, d), BF16)],
        compiler_params=_cparams("parallel", "arbitrary"),
        name="norm_matmul",
    )(x, g, w)


def _mix_ffn_kernel(x_ref, a1_ref, a2_ref, wo1_ref, wo2_ref, g_ref, wu_ref, wd_ref, gf_ref, o_ref, h_ref, acc_ref,
                    *, final_norm):
    k = pl.program_id(1)

    @pl.when(k == 0)
    def _():
        y = jnp.dot(a1_ref[...], wo1_ref[...], preferred_element_type=F32)
        y = y + jnp.dot(a2_ref[...], wo2_ref[...], preferred_element_type=F32)
        x1 = x_ref[...] + y
        acc_ref[...] = x1
        h_ref[...] = _rms(x1, g_ref[...]).astype(BF16)

    u = jnp.dot(h_ref[...], wu_ref[...], preferred_element_type=F32)
    u = jnp.maximum(u, 0.0)
    acc_ref[...] += jnp.dot((u * u).astype(BF16), wd_ref[...], preferred_element_type=F32)

    @pl.when(k == pl.num_programs(1) - 1)
    def _():
        y = acc_ref[...]
        if final_norm:
            y = _rms(y, gf_ref[...])
        o_ref[...] = y


def mix_out_ffn(x, a1, a1_blk, a2, a2_blk, w_out, g, w_up, w_down, g_final, final_norm, tm=512, tf=1024):
    m, d = x.shape
    ff = w_up.shape[1]
    half = d // 2
    return pl.pallas_call(
        functools.partial(_mix_ffn_kernel, final_norm=final_norm),
        grid=(m // tm, ff // tf),
        in_specs=[pl.BlockSpec((tm, d), lambda i, k: (i, 0)),
                  pl.BlockSpec((tm, half), lambda i, k: (i, a1_blk)),
                  pl.BlockSpec((tm, half), lambda i, k: (i, a2_blk)),
                  pl.BlockSpec((half, d), lambda i, k: (0, 0), pipeline_mode=pl.Buffered(1)),
                  pl.BlockSpec((half, d), lambda i, k: (1, 0), pipeline_mode=pl.Buffered(1)),
                  pl.BlockSpec((1, d), lambda i, k: (0, 0)),
                  pl.BlockSpec((d, tf), lambda i, k: (0, k)),
                  pl.BlockSpec((tf, d), lambda i, k: (k, 0)),
                  pl.BlockSpec((1, d), lambda i, k: (0, 0))],
        out_specs=pl.BlockSpec((tm, d), lambda i, k: (i, 0)),
        out_shape=jax.ShapeDtypeStruct((m, d), F32),
        scratch_shapes=[pltpu.VMEM((tm, d), BF16), pltpu.VMEM((tm, d), F32)],
        compiler_params=_cparams("parallel", "arbitrary"),
        name="mix_ffn",
    )(x, a1, a2, w_out, w_out, g, w_up, w_down, g_final)


def _swa_kernel(q_ref, kp_ref, kc_ref, vp_ref, vc_ref, bias_ref, sink_ref, o_ref,
                s_ref, p_ref, m_ref, l_ref, a_ref, pm_ref):
    first_block = jnp.where(pl.program_id(1) == 0, 1, 0)
    rows = A_GROUP * Q_BLOCK
    chunks_per_head = Q_BLOCK // ROW_CHUNK
    lane = lax.broadcasted_iota(jnp.int32, (Q_BLOCK, PAIR), 1)
    for g in range(A_KV_HEADS):
        qs = _stack_pairs(q_ref, g * (A_GROUP // 2), A_GROUP // 2)
        k = jnp.concatenate([kp_ref[:, g * PAIR:(g + 1) * PAIR], kc_ref[:, g * PAIR:(g + 1) * PAIR]], axis=0)
        v = jnp.concatenate([vp_ref[:, g * PAIR:(g + 1) * PAIR], vc_ref[:, g * PAIR:(g + 1) * PAIR]], axis=0)
        s_ref[g] = _nt(qs, k)
        m_ref[g] = jnp.concatenate([jnp.broadcast_to(sink_ref[g * A_GROUP + r], (Q_BLOCK, PAIR))
                                    for r in range(A_GROUP)], axis=0)
        l_ref[g] = jnp.concatenate([jnp.where(lane == 0, 1.0, 0.0)] * A_GROUP, axis=0)

        def lane_tiles(c, u):
            head = g * A_GROUP + c // chunks_per_head
            q0 = (c % chunks_per_head) * ROW_CHUNK
            return (s_ref[g, c * ROW_CHUNK:(c + 1) * ROW_CHUNK, u * Q_BLOCK:(u + 1) * Q_BLOCK]
                    + bias_ref[first_block, head, q0:q0 + ROW_CHUNK, u * Q_BLOCK:(u + 1) * Q_BLOCK])

        _softmax_tile(lane_tiles, rows // ROW_CHUNK, ROW_CHUNK, 2, m_ref.at[g], l_ref.at[g], a_ref.at[g],
                      pm_ref.at[g], p_ref.at[g])
        o = jnp.dot(p_ref[g], v, preferred_element_type=F32)
        o = o / jnp.sum(l_ref[g], axis=-1, keepdims=True)
        for p, blk in enumerate(_unstack_pairs(o, A_GROUP // 2, Q_BLOCK)):
            c0 = (g * (A_GROUP // 2) + p) * PAIR
            o_ref[:, c0:c0 + PAIR] = blk.astype(o_ref.dtype)


def swa_attention(qkv, bias_a, sinks, batch, seq):
    nb = seq // Q_BLOCK
    m = batch * seq
    qa_w = A_Q_HEADS * HEAD_DIM
    kv_w = A_KV_HEADS * PAIR
    k_blk = qa_w // kv_w
    v_blk = k_blk + 1
    row = lambda b, i: b * nb + i
    prev = lambda b, i: b * nb + jnp.maximum(i - 1, 0)
    return pl.pallas_call(
        _swa_kernel,
        grid=(batch, nb),
        in_specs=[pl.BlockSpec((Q_BLOCK, qa_w), lambda b, i: (row(b, i), 0)),
                  pl.BlockSpec((Q_BLOCK, kv_w), lambda b, i: (prev(b, i), k_blk)),
                  pl.BlockSpec((Q_BLOCK, kv_w), lambda b, i: (row(b, i), k_blk)),
                  pl.BlockSpec((Q_BLOCK, kv_w), lambda b, i: (prev(b, i), v_blk)),
                  pl.BlockSpec((Q_BLOCK, kv_w), lambda b, i: (row(b, i), v_blk)),
                  pl.BlockSpec((2, A_Q_HEADS, Q_BLOCK, 2 * Q_BLOCK), lambda b, i: (0, 0, 0, 0)),
                  pl.BlockSpec((A_Q_HEADS, 1, PAIR), lambda b, i: (0, 0, 0))],
        out_specs=pl.BlockSpec((Q_BLOCK, qa_w), lambda b, i: (row(b, i), 0)),
        out_shape=jax.ShapeDtypeStruct((m, qa_w), BF16),
        scratch_shapes=[pltpu.VMEM((A_KV_HEADS, A_GROUP * Q_BLOCK, 2 * Q_BLOCK), F32),
                        pltpu.VMEM((A_KV_HEADS, A_GROUP * Q_BLOCK, 2 * Q_BLOCK), BF16)]
        + [pltpu.VMEM((A_KV_HEADS, A_GROUP * Q_BLOCK, PAIR), F32)] * 4,
        compiler_params=_cparams("parallel", "parallel"),
        name="swa",
    )(qkv, qkv, qkv, qkv, qkv, bias_a, sinks)


def _compress_kernel(r_ref, pe_ref, w1_ref, w2_ref, o_ref):
    half = NSA_CMP_STRIDE * HEAD_DIM
    r = r_ref[0, 0]
    xa = (r + pe_ref[0, :, :half]).astype(BF16)
    xb = (r + pe_ref[0, :, half:]).astype(BF16)
    a = jnp.dot(xa, w1_ref[0, :half, :], preferred_element_type=F32)
    b = jnp.dot(xb, w1_ref[0, half:, :], preferred_element_type=F32)
    n = r.shape[0]
    hid = jax.nn.gelu(a + pltpu.roll(b, n - 1, 0))
    o_ref[0, 0] = jnp.dot(hid.astype(BF16), w2_ref[0], preferred_element_type=F32).astype(o_ref.dtype)


def nsa_compress(r, pe, w1, w2dup):
    batch, _, n, width = r.shape
    return pl.pallas_call(
        _compress_kernel,
        grid=(batch, 2 * B_KV_HEADS),
        in_specs=[pl.BlockSpec((1, 1, n, width), lambda b, w: (b, w, 0, 0)),
                  pl.BlockSpec((1, 1, 2 * width), lambda b, w: (w // B_KV_HEADS, 0, 0)),
                  pl.BlockSpec((1, 2 * width, NSA_CMP_HIDDEN), lambda b, w: (w // B_KV_HEADS, 0, 0)),
                  pl.BlockSpec((1, NSA_CMP_HIDDEN, PAIR), lambda b, w: (w // B_KV_HEADS, 0, 0))],
        out_specs=pl.BlockSpec((1, 1, n, PAIR), lambda b, w: (b, w, 0, 0)),
        out_shape=jax.ShapeDtypeStruct((batch, 2 * B_KV_HEADS, n, PAIR), BF16),
        compiler_params=_cparams("parallel", "parallel"),
        name="nsa_compress",
    )(r, pe, w1, w2dup)


NSA_KEY_TILE = 512
NSA_STREAMS = 2
WIN_TILES = NSA_WINDOW // Q_BLOCK + 1
TAB_FAR = NEAR_TILES
TAB_WIN_EDGE = NEAR_TILES + 1
TAB_NONE = NEAR_TILES + 2
TAB_ENTRIES = NEAR_TILES + 3
UNSELECTED = -1.0e30


def _nsa_kernel(q_ref, kcm_ref, vcm_ref, ks_ref, vs_ref, kw_ref, vw_ref, gate_ref, bias_c_ref, tab_ref, ovt_ref,
                o_ref, val_ref, s_ref, p_ref, m_ref, l_ref, a_ref, pm_ref, acc_ref, *, n_cmp_pad):
    i = pl.program_id(2)
    rows = B_GROUP * Q_BLOCK
    srows = rows // NSA_STREAMS
    heads_per_stream = B_GROUP // NSA_STREAMS
    chunks_per_head = Q_BLOCK // ROW_CHUNK
    n_sel = val_ref.shape[0]
    sel_shift = int(math.log2(NSA_SEL_LEN))
    qs = _stack_pairs(q_ref, 0, B_GROUP // 2)

    ql_c = lax.broadcasted_iota(jnp.int32, (Q_BLOCK, n_cmp_pad), 0) + i * Q_BLOCK
    c_id = lax.broadcasted_iota(jnp.int32, (Q_BLOCK, n_cmp_pad), 1)
    valid_c = (c_id * NSA_CMP_STRIDE + (NSA_CMP_LEN - 1) <= ql_c) & (c_id < n_cmp_pad - 1)
    s_c = _nt(qs, kcm_ref[0, 0]).reshape(B_GROUP, Q_BLOCK, n_cmp_pad) + bias_c_ref[0]
    s_c = jnp.where(valid_c[None], s_c, NEG_INF)
    m_c = jnp.max(s_c, axis=-1, keepdims=True)
    m_c = jnp.where(m_c == NEG_INF, 0.0, m_c)
    e_c = jnp.exp2(s_c - m_c)
    p_c = e_c / jnp.maximum(jnp.sum(e_c, axis=-1, keepdims=True), 1e-30)
    o_c = jnp.dot(p_c.reshape(rows, n_cmp_pad).astype(BF16), vcm_ref[0, 0], preferred_element_type=F32)

    p_sum = jnp.sum(p_c, axis=0)
    ovt = ovt_ref[...]
    imp = None
    for piece in _split3(p_sum):
        t = _nt(ovt, piece)
        imp = t if imp is None else imp + t
    blk = lax.broadcasted_iota(jnp.int32, (n_sel, Q_BLOCK), 0)
    qpos = lax.broadcasted_iota(jnp.int32, (n_sel, Q_BLOCK), 1) + i * Q_BLOCK
    cur = lax.shift_right_logical(qpos, sel_shift)
    forced = (blk == 0) | (blk == cur) | (blk == cur - 1)
    future = blk * NSA_SEL_LEN > qpos
    val = jnp.where(future, NEG_INF, jnp.where(forced, NSA_FORCE_SCORE, imp))
    val_ref[...] = val
    rank = jnp.zeros((n_sel, Q_BLOCK), F32)
    for s2 in range(n_sel):
        other = val_ref[s2:s2 + 1, :]
        rank = rank + jnp.where(blk > s2, jnp.where(other >= val, 1.0, 0.0), jnp.where(other > val, 1.0, 0.0))
    sel_t = jnp.where((rank < float(NSA_TOP_N)) & (val > NEG_INF), 1.0, 0.0).astype(BF16)
    if n_sel < Q_BLOCK:
        sel_t = jnp.concatenate([sel_t, jnp.zeros((Q_BLOCK - n_sel, Q_BLOCK), BF16)], axis=0)
    eye = jnp.where(lax.broadcasted_iota(jnp.int32, (Q_BLOCK, Q_BLOCK), 0)
                    == lax.broadcasted_iota(jnp.int32, (Q_BLOCK, Q_BLOCK), 1), 1.0, 0.0).astype(BF16)
    unsel = jnp.where(_nt(eye, sel_t) > 0.5, 0.0, UNSELECTED).astype(BF16)
    unsel_rows = jnp.concatenate([unsel] * heads_per_stream, axis=0)
    qs_st = [qs[st * srows:(st + 1) * srows] for st in range(NSA_STREAMS)]
    qs_sel = [jnp.concatenate([qs_st[st], unsel_rows], axis=1) for st in range(NSA_STREAMS)]

    key_lane = lax.broadcasted_iota(jnp.int32, (NSA_KEY_TILE, PAIR), 1)
    key_blk = lax.shift_right_logical(lax.broadcasted_iota(jnp.int32, (NSA_KEY_TILE, PAIR), 0), sel_shift)
    lane_minus_blk = key_lane - key_blk

    def reset():
        m_ref[...] = jnp.full(m_ref.shape, NEG_INF, F32)
        l_ref[...] = jnp.zeros(l_ref.shape, F32)
        acc_ref[...] = jnp.zeros(acc_ref.shape, F32)

    def soft_pv(st, tab_idx, v):
        width = len(tab_idx) * Q_BLOCK

        def raw(c, u):
            return s_ref[st, c * ROW_CHUNK:(c + 1) * ROW_CHUNK, u * Q_BLOCK:(u + 1) * Q_BLOCK]

        def biased(c, u):
            head = st * heads_per_stream + c // chunks_per_head
            q0 = (c % chunks_per_head) * ROW_CHUNK
            return raw(c, u) + tab_ref[0, tab_idx[u], head, q0:q0 + ROW_CHUNK, :]

        def keep(c, u, s):
            s_ref[st, c * ROW_CHUNK:(c + 1) * ROW_CHUNK, u * Q_BLOCK:(u + 1) * Q_BLOCK] = s

        _softmax_tile(biased, srows // ROW_CHUNK, ROW_CHUNK, len(tab_idx), m_ref.at[st], l_ref.at[st],
                      a_ref.at[st], pm_ref.at[st], p_ref.at[st], keep=keep, kept_tiles=raw)
        acc_ref[st] = a_ref[st] * acc_ref[st] + jnp.dot(p_ref[st, :, :width], v, preferred_element_type=F32)

    def finish():
        return jnp.concatenate(
            [acc_ref[st] / jnp.maximum(jnp.sum(l_ref[st], axis=-1, keepdims=True), 1e-30)
             for st in range(NSA_STREAMS)], axis=0)

    blocks_per_tile = NSA_KEY_TILE // NSA_SEL_LEN
    lane_tiles_per_tile = NSA_KEY_TILE // Q_BLOCK

    def sel_scores(st, jt):
        off = pl.multiple_of(jt * NSA_KEY_TILE, NSA_KEY_TILE)
        block_onehot = jnp.where(lane_minus_blk == jt * blocks_per_tile, 1.0, 0.0).astype(BF16)
        k_aug = jnp.concatenate([ks_ref[pl.ds(off, NSA_KEY_TILE), :], block_onehot], axis=1)
        s_ref[st, :, :NSA_KEY_TILE] = _nt(qs_sel[st], k_aug)

    def sel_tile(jt, last):
        v = vs_ref[pl.ds(pl.multiple_of(jt * NSA_KEY_TILE, NSA_KEY_TILE), NSA_KEY_TILE), :]
        tab_idx = []
        for u in range(lane_tiles_per_tile):
            d = i - (jt * lane_tiles_per_tile + u)
            tab_idx.append(jnp.where(d < 0, TAB_NONE, jnp.minimum(d, TAB_FAR)))
        for st in range(NSA_STREAMS):
            soft_pv(st, tab_idx, v)
            if not last:
                sel_scores(st, jt + 1)

    def sel_body(jt, carry):
        sel_tile(jt, False)
        return carry

    n_tiles = lax.shift_right_logical(i + lane_tiles_per_tile, int(math.log2(lane_tiles_per_tile)))
    reset()
    for st in range(NSA_STREAMS):
        sel_scores(st, 0)
    lax.fori_loop(0, n_tiles - 1, sel_body, 0)
    sel_tile(n_tiles - 1, True)
    o_s = finish()

    first_blk = jnp.maximum(i - (WIN_TILES - 1), 0)
    off = pl.multiple_of(first_blk * Q_BLOCK, Q_BLOCK)
    k_win = kw_ref[pl.ds(off, WIN_TILES * Q_BLOCK), :]
    v_win = vw_ref[pl.ds(off, WIN_TILES * Q_BLOCK), :]
    tab_idx = []
    for u in range(WIN_TILES):
        d = i - (first_blk + u)
        tab_idx.append(jnp.where(d < 0, TAB_NONE, jnp.where(d == WIN_TILES - 1, TAB_WIN_EDGE, d)))
    reset()
    for st in range(NSA_STREAMS):
        s_ref[st] = _nt(qs_st[st], k_win)
    for st in range(NSA_STREAMS):
        soft_pv(st, tab_idx, v_win)
    o_w = finish()

    gates = jax.nn.sigmoid(gate_ref[...])

    def gate_col(br):
        return jnp.concatenate([gates[:, 3 * r + br:3 * r + br + 1] for r in range(B_GROUP)], axis=0)

    o = gate_col(0) * o_c + gate_col(1) * o_s + gate_col(2) * o_w
    for p, blk_out in enumerate(_unstack_pairs(o, B_GROUP // 2, Q_BLOCK)):
        o_ref[:, p * PAIR:(p + 1) * PAIR] = blk_out.astype(o_ref.dtype)


def nsa_attention(qkv, cmp_kv, tail, bias_c, bias_tab, ovt, batch, seq):
    nb = seq // Q_BLOCK
    m = batch * seq
    n_cmp_pad = seq // NSA_CMP_STRIDE
    n_sel = seq // NSA_SEL_LEN
    assert n_sel <= Q_BLOCK and seq % NSA_KEY_TILE == 0 and seq >= WIN_TILES * Q_BLOCK
    grp_w = B_GROUP * HEAD_DIM
    q_blk0 = (A_Q_HEADS * HEAD_DIM + 2 * A_KV_HEADS * PAIR) // grp_w
    kv_blk0 = (A_Q_HEADS * HEAD_DIM + 2 * A_KV_HEADS * PAIR + B_Q_HEADS * HEAD_DIM) // PAIR
    kv_spec = lambda t: pl.BlockSpec((seq, PAIR), lambda b, g, i: (b, kv_blk0 + t * B_KV_HEADS + g))
    srows = B_GROUP * Q_BLOCK // NSA_STREAMS
    s_width = WIN_TILES * Q_BLOCK
    return pl.pallas_call(
        functools.partial(_nsa_kernel, n_cmp_pad=n_cmp_pad),
        grid=(batch, B_KV_HEADS, nb),
        in_specs=[pl.BlockSpec((Q_BLOCK, grp_w), lambda b, g, i: (b * nb + i, q_blk0 + g)),
                  pl.BlockSpec((1, 1, n_cmp_pad, PAIR), lambda b, g, i: (b, g, 0, 0)),
                  pl.BlockSpec((1, 1, n_cmp_pad, PAIR), lambda b, g, i: (b, B_KV_HEADS + g, 0, 0)),
                  kv_spec(0), kv_spec(1), kv_spec(2), kv_spec(3),
                  pl.BlockSpec((Q_BLOCK, PAIR), lambda b, g, i: (b * nb + i, 2 + g)),
                  pl.BlockSpec((1, B_GROUP, Q_BLOCK, n_cmp_pad), lambda b, g, i: (i, g, 0, 0)),
                  pl.BlockSpec((1, TAB_ENTRIES, B_GROUP, Q_BLOCK, Q_BLOCK), lambda b, g, i: (g, 0, 0, 0, 0)),
                  pl.BlockSpec((n_sel, n_cmp_pad), lambda b, g, i: (0, 0))],
        out_specs=pl.BlockSpec((Q_BLOCK, grp_w), lambda b, g, i: (b * nb + i, g)),
        out_shape=jax.ShapeDtypeStruct((m, B_Q_HEADS * HEAD_DIM), BF16),
        scratch_shapes=[pltpu.VMEM((n_sel, Q_BLOCK), F32),
                        pltpu.VMEM((NSA_STREAMS, srows, s_width), F32),
                        pltpu.VMEM((NSA_STREAMS, srows, s_width), BF16)]
        + [pltpu.VMEM((NSA_STREAMS, srows, PAIR), F32)] * 5,
        compiler_params=_cparams("parallel", "parallel", "arbitrary"),
        name="nsa",
    )(qkv, cmp_kv, cmp_kv, qkv, qkv, qkv, qkv, tail, bias_c, bias_tab, ovt)


DECAY_STEP = 512
FOX_SPLIT = 1


def _decay_kernel(f_ref, fb_ref, place_ref, o_ref, carry_ref):
    r_i = lax.broadcasted_iota(jnp.int32, (Q_BLOCK, Q_BLOCK), 0)
    c_i = lax.broadcasted_iota(jnp.int32, (Q_BLOCK, Q_BLOCK), 1)
    tri = jnp.where(c_i <= r_i, 1.0, 0.0).astype(BF16)

    @pl.when(pl.program_id(1) == 0)
    def _():
        carry_ref[...] = jnp.zeros_like(carry_ref)

    carry = carry_ref[...]
    for r in range(DECAY_STEP // Q_BLOCK):
        x = f_ref[r * Q_BLOCK:(r + 1) * Q_BLOCK, :] + fb_ref[...]
        ls = jax.nn.log_sigmoid(x)
        cs = jnp.broadcast_to(carry, ls.shape)
        for piece in _split3(ls):
            cs = cs + jnp.dot(tri, piece, preferred_element_type=F32)
        feat = None
        for n, piece in enumerate(_split3(cs * LOG2E)):
            term = jnp.dot(piece, place_ref[n], preferred_element_type=F32)
            feat = term if feat is None else feat + term
        o_ref[r * Q_BLOCK:(r + 1) * Q_BLOCK, :] = feat.astype(o_ref.dtype)
        carry = cs[Q_BLOCK - 1:Q_BLOCK, :]
    carry_ref[...] = carry


def _decay_placement():
    place = np.zeros((3, PAIR, (C_HEADS // 2) * PAIR), np.float32)
    for h in range(C_HEADS):
        for n in range(3):
            place[n, h, (h // 2) * PAIR + 3 * (h % 2) + n] = -1.0
    return jnp.asarray(place, dtype=BF16)


def fox_decay(f_tail, f_bias, batch, seq):
    steps = seq // DECAY_STEP
    width = (C_HEADS // 2) * PAIR
    return pl.pallas_call(
        _decay_kernel,
        grid=(batch, steps),
        in_specs=[pl.BlockSpec((DECAY_STEP, PAIR), lambda b, s: (b * steps + s, 0)),
                  pl.BlockSpec((1, PAIR), lambda b, s: (0, 0)),
                  pl.BlockSpec((3, PAIR, width), lambda b, s: (0, 0, 0))],
        out_specs=pl.BlockSpec((DECAY_STEP, width), lambda b, s: (b * steps + s, 0)),
        out_shape=jax.ShapeDtypeStruct((batch * seq, width), BF16),
        scratch_shapes=[pltpu.VMEM((1, PAIR), F32)],
        compiler_params=_cparams("parallel", "arbitrary"),
        name="fox_decay",
    )(f_tail, f_bias, _decay_placement())


def _softmax_tile(lane_tiles, n_chunks, rc, n_lane_tiles, m_ref, l_ref, a_ref, pm_ref, p_ref, keep=None,
                  kept_tiles=None):
    second = lane_tiles if kept_tiles is None else kept_tiles
    for c in range(n_chunks):
        pm = None
        for u in range(n_lane_tiles):
            s = lane_tiles(c, u)
            if s is None:
                continue
            if keep is not None:
                keep(c, u, s)
            pm = s if pm is None else jnp.maximum(pm, s)
        pm_ref[c * rc:(c + 1) * rc, :] = pm
    m_old = m_ref[...]
    m_new = jnp.maximum(m_old, jnp.max(pm_ref[...], axis=-1, keepdims=True))
    a_ref[...] = jnp.exp2(m_old - m_new)
    m_ref[...] = m_new
    for c in range(n_chunks):
        rows = slice(c * rc, (c + 1) * rc)
        mb = m_ref[rows, :]
        psum = None
        for u in range(n_lane_tiles):
            s = second(c, u)
            if s is None:
                p_ref[rows, u * Q_BLOCK:(u + 1) * Q_BLOCK] = jnp.zeros((rc, Q_BLOCK), BF16)
                continue
            p = jnp.exp2(s - mb)
            p_ref[rows, u * Q_BLOCK:(u + 1) * Q_BLOCK] = p.astype(BF16)
            psum = p if psum is None else psum + p
        l_ref[rows, :] = a_ref[rows, :] * l_ref[rows, :] + psum


def _fox_kernel(q_ref, k_ref, v_ref, e_ref, o_ref, s_ref, p_ref, m_ref, l_ref, a_ref, pm_ref, acc_ref, *, tq):
    t = pl.program_id(2)
    kw = tq
    n_lane_tiles = kw // Q_BLOCK
    srows = tq // FOX_SPLIT
    n_chunks = srows // ROW_CHUNK
    n_streams = 2 * FOX_SPLIT
    q = q_ref[...]
    lane = lax.broadcasted_iota(jnp.int32, (tq, PAIR), 1)
    lo = lane < HEAD_DIM
    zero = jnp.zeros_like(q)
    q_aug = [jnp.concatenate([jnp.where(lo if hh == 0 else ~lo, q, zero),
                              jnp.where((lane >= 3 * hh) & (lane < 3 * hh + 3), 1.0, 0.0).astype(BF16)], axis=1)
             for hh in range(2)]
    qs = [q_aug[st // FOX_SPLIT][(st % FOX_SPLIT) * srows:(st % FOX_SPLIT + 1) * srows] for st in range(n_streams)]
    m_ref[...] = jnp.full(m_ref.shape, NEG_INF, F32)
    l_ref[...] = jnp.zeros(l_ref.shape, F32)
    acc_ref[...] = jnp.zeros(acc_ref.shape, F32)
    col_minus_row = (lax.broadcasted_iota(jnp.int32, (ROW_CHUNK, Q_BLOCK), 1)
                     - lax.broadcasted_iota(jnp.int32, (ROW_CHUNK, Q_BLOCK), 0))

    def scores(st, j):
        off = pl.multiple_of(j * kw, kw)
        k_aug = jnp.concatenate([k_ref[pl.ds(off, kw), :], e_ref[pl.ds(off, kw), :]], axis=1)
        s_ref[st] = _nt(qs[st], k_aug)

    def tile(j, diagonal):
        v = v_ref[pl.ds(pl.multiple_of(j * kw, kw), kw), :]
        for st in range(n_streams):
            row0 = (st % FOX_SPLIT) * srows

            def lane_tiles(c, u):
                first_row, first_key = row0 + c * ROW_CHUNK, u * Q_BLOCK
                if diagonal and first_key > first_row + ROW_CHUNK - 1:
                    return None
                s = s_ref[st, c * ROW_CHUNK:(c + 1) * ROW_CHUNK, u * Q_BLOCK:(u + 1) * Q_BLOCK]
                if diagonal and first_key + Q_BLOCK - 1 > first_row:
                    s = jnp.where(col_minus_row <= first_row - first_key, s, NEG_INF)
                return s

            _softmax_tile(lane_tiles, n_chunks, ROW_CHUNK, n_lane_tiles, m_ref.at[st], l_ref.at[st], a_ref.at[st],
                          pm_ref.at[st], p_ref.at[st])
            acc_ref[st] = a_ref[st] * acc_ref[st] + jnp.dot(p_ref[st], v, preferred_element_type=F32)
            if not diagonal:
                scores(st, j + 1)

    def body(j, carry):
        tile(j, False)
        return carry

    for st in range(n_streams):
        scores(st, 0)
    lax.fori_loop(0, t, body, 0)
    tile(t, True)
    o = [jnp.concatenate([acc_ref[st] / jnp.maximum(jnp.sum(l_ref[st], axis=-1, keepdims=True), 1e-30)
                          for st in range(hh * FOX_SPLIT, (hh + 1) * FOX_SPLIT)], axis=0) for hh in range(2)]
    o_ref[...] = jnp.where(lo, o[0], o[1]).astype(o_ref.dtype)


def fox_attention(qkv, decay, batch, seq, tq=512):
    m = batch * seq
    n_pairs = C_HEADS // 2
    nt = seq // tq
    return pl.pallas_call(
        functools.partial(_fox_kernel, tq=tq),
        grid=(batch, n_pairs, nt),
        in_specs=[pl.BlockSpec((tq, PAIR), lambda b, h, t: (b * nt + t, h)),
                  pl.BlockSpec((seq, PAIR), lambda b, h, t: (b, n_pairs + h)),
                  pl.BlockSpec((seq, PAIR), lambda b, h, t: (b, 2 * n_pairs + h)),
                  pl.BlockSpec((seq, PAIR), lambda b, h, t: (b, h))],
        out_specs=pl.BlockSpec((tq, PAIR), lambda b, h, t: (b * nt + t, h)),
        out_shape=jax.ShapeDtypeStruct((m, C_HEADS * HEAD_DIM), BF16),
        scratch_shapes=[pltpu.VMEM((2 * FOX_SPLIT, tq // FOX_SPLIT, tq), F32),
                        pltpu.VMEM((2 * FOX_SPLIT, tq // FOX_SPLIT, tq), BF16)]
        + [pltpu.VMEM((2 * FOX_SPLIT, tq // FOX_SPLIT, PAIR), F32)] * 5,
        compiler_params=_cparams("parallel", "parallel", "arbitrary"),
        name="fox",
    )(qkv, qkv, qkv, decay)


def _rel_bucket(dist):
    n = jnp.maximum(dist, 0)
    max_exact = REL_BUCKETS // 2
    nf = jnp.maximum(n, 1).astype(jnp.float32)
    large = max_exact + (jnp.log(nf / max_exact) / math.log(REL_MAX_DIST / max_exact)
                         * (REL_BUCKETS - max_exact)).astype(jnp.int32)
    return jnp.where(n < max_exact, n, jnp.minimum(large, REL_BUCKETS - 1))


def _lookup(table, idx):
    onehot = (idx[..., None] == jnp.arange(table.shape[0])).astype(F32)
    return jnp.einsum("...n,nh->...h", onehot, table, precision=lax.Precision.HIGHEST)


def _bias_tables(rel_bias, seq):
    nb = seq // Q_BLOCK
    n_cmp_pad = seq // NSA_CMP_STRIDE
    ql = jnp.arange(Q_BLOCK)
    d = jnp.arange(NEAR_TILES)
    dist_t = d[:, None, None] * Q_BLOCK + ql[None, :, None] - ql[None, None, :]
    tt = jnp.moveaxis(_lookup(rel_bias, _rel_bucket(dist_t)), -1, 0)
    dist_a = ql[:, None] + Q_BLOCK - jnp.arange(2 * Q_BLOCK)[None, :]
    seen = (dist_a >= 0) & (dist_a < A_WINDOW)
    own = jnp.arange(2 * Q_BLOCK)[None, :] >= Q_BLOCK
    bias_a = jnp.concatenate([tt[:A_Q_HEADS, 1], tt[:A_Q_HEADS, 0]], axis=-1) * LOG2E
    bias_a = jnp.stack([jnp.where(seen, bias_a, NEG_INF), jnp.where(seen & own, bias_a, NEG_INF)])
    tb = tt[A_Q_HEADS:]
    upper = (ql[None, :] > ql[:, None])[None]
    far = jnp.broadcast_to(rel_bias[REL_BUCKETS - 1, A_Q_HEADS:][:, None, None], tb[:, 0].shape)
    entries = ([jnp.where(upper, NEG_INF, tb[:, 0])] + [tb[:, d] for d in range(1, NEAR_TILES)]
               + [far, jnp.where(upper, tb[:, WIN_TILES - 1], NEG_INF), jnp.full_like(far, NEG_INF)])
    bias_tab = jnp.stack(entries, axis=1).reshape(B_KV_HEADS, B_GROUP, TAB_ENTRIES, Q_BLOCK, Q_BLOCK)
    bias_tab = jnp.transpose(bias_tab, (0, 2, 1, 3, 4)) * LOG2E
    cend = jnp.arange(n_cmp_pad) * NSA_CMP_STRIDE + NSA_CMP_LEN - 1
    dist_c = (jnp.arange(nb)[:, None, None] * Q_BLOCK + ql[None, :, None]) - cend[None, None, :]
    bias_c = jnp.moveaxis(_lookup(rel_bias[:, A_Q_HEADS:] * LOG2E, _rel_bucket(dist_c)), -1, 1)
    return bias_a, bias_tab, bias_c


def _overlap_t(seq):
    n_cmp_pad = seq // NSA_CMP_STRIDE
    n_sel = seq // NSA_SEL_LEN
    cstart = np.arange(n_cmp_pad) * NSA_CMP_STRIDE
    sstart = np.arange(n_sel) * NSA_SEL_LEN
    ov = (cstart[None, :] < sstart[:, None] + NSA_SEL_LEN) & (cstart[None, :] + NSA_CMP_LEN > sstart[:, None])
    ov[:, n_cmp_pad - 1] = False
    return jnp.asarray(ov.astype(np.float32), dtype=BF16)


def _dup(w, n_heads):
    d = w.shape[0]
    w = w.reshape(d, n_heads, 1, HEAD_DIM)
    return jnp.broadcast_to(w, (d, n_heads, 2, HEAD_DIM)).reshape(d, n_heads * PAIR)


def _even_weights(w_in):
    sizes = (A_Q_HEADS * HEAD_DIM, A_KV_HEADS * HEAD_DIM, A_KV_HEADS * HEAD_DIM, B_Q_HEADS * HEAD_DIM) \
        + (B_KV_HEADS * HEAD_DIM,) * 6 + (3 * B_Q_HEADS,)
    qa, ka, va, qb, kc, vc, ksl, vsl, kwn, vwn, gt = jnp.split(w_in, np.cumsum(sizes)[:-1].tolist(), axis=-1)
    log2_scale = ATTN_SCALE * LOG2E
    main = jnp.concatenate([qa * log2_scale, _dup(ka, A_KV_HEADS), _dup(va, A_KV_HEADS), qb * log2_scale,
                            _dup(ksl, B_KV_HEADS), _dup(vsl, B_KV_HEADS),
                            _dup(kwn, B_KV_HEADS), _dup(vwn, B_KV_HEADS)], axis=-1).astype(BF16)
    d = w_in.shape[0]
    per_group = 3 * B_GROUP
    gates = [jnp.pad(gt[:, g * per_group:(g + 1) * per_group], ((0, 0), (0, PAIR - per_group)))
             for g in range(B_KV_HEADS)]
    tail = jnp.concatenate([kc, vc] + gates, axis=-1).astype(BF16)
    return main, tail


def _odd_weights(w_in):
    c_mix = C_HEADS * HEAD_DIM
    main = jnp.concatenate([w_in[:, :c_mix] * (ATTN_SCALE * LOG2E), w_in[:, c_mix:3 * c_mix]],
                           axis=-1).astype(BF16)
    tail = jnp.pad(w_in[:, 3 * c_mix:], ((0, 0), (0, PAIR - C_HEADS))).astype(BF16)
    return main, tail


def kernel(x, rel_bias, norm_mix, norm_ffn, norm_final, w_in_even, w_out_even, a_sinks, nsa_pe_k, nsa_pe_v,
           nsa_cmp_k_w1, nsa_cmp_k_w2, nsa_cmp_v_w1, nsa_cmp_v_w2, w_in_odd, w_out_odd, fox_fgate_b,
           w_ffn_up, w_ffn_down):
    batch, seq, d = x.shape
    depth = norm_mix.shape[0]
    m = batch * seq
    xf = x.reshape(m, d)
    bias_a, bias_tab, bias_c = _bias_tables(rel_bias, seq)
    ovt = _overlap_t(seq)
    n_cmp_pad = seq // NSA_CMP_STRIDE
    g_final = norm_final.reshape(1, d)

    for layer in range(depth):
        g_mix = norm_mix[layer].reshape(1, d)
        if layer % 2 == 0:
            e = layer // 2
            w_main, w_tail = _even_weights(w_in_even[e])
            qkv = norm_matmul(xf, g_mix, w_main, BF16)
            tail = norm_matmul(xf, g_mix, w_tail, F32)
            sinks = jnp.broadcast_to((a_sinks[e] * LOG2E).reshape(A_Q_HEADS, 1, 1), (A_Q_HEADS, 1, PAIR))
            a_out = swa_attention(qkv, bias_a, sinks, batch, seq)
            r = tail[:, :2 * B_KV_HEADS * HEAD_DIM].reshape(batch, seq, 2 * B_KV_HEADS, HEAD_DIM)
            r = jnp.transpose(r, (0, 2, 1, 3)).reshape(batch, 2 * B_KV_HEADS, n_cmp_pad, NSA_CMP_STRIDE * HEAD_DIM)
            pe = jnp.stack([nsa_pe_k[e].reshape(1, -1), nsa_pe_v[e].reshape(1, -1)])
            w1 = jnp.stack([nsa_cmp_k_w1[e], nsa_cmp_v_w1[e]]).astype(BF16)
            w2 = jnp.stack([_dup(nsa_cmp_k_w2[e], 1), _dup(nsa_cmp_v_w2[e], 1)]).astype(BF16)
            cmp_kv = nsa_compress(r, pe, w1, w2)
            b_out = nsa_attention(qkv, cmp_kv, tail, bias_c, bias_tab, ovt, batch, seq)
            mixed = (a_out, 0, b_out, 0, w_out_even[e].astype(BF16))
        else:
            o = layer // 2
            w_main, w_tail = _odd_weights(w_in_odd[o])
            qkv = norm_matmul(xf, g_mix, w_main, BF16)
            f_tail = norm_matmul(xf, g_mix, w_tail, F32)
            f_bias = jnp.pad(fox_fgate_b[o], (0, PAIR - C_HEADS)).reshape(1, PAIR)
            decay = fox_decay(f_tail, f_bias, batch, seq)
            c_out = fox_attention(qkv, decay, batch, seq)
            mixed = (c_out, 0, c_out, 1, w_out_odd[o].astype(BF16))
        xf = mix_out_ffn(xf, *mixed, norm_ffn[layer].reshape(1, d), w_ffn_up[layer].astype(BF16),
                         w_ffn_down[layer].astype(BF16), g_final, layer == depth - 1)
    return xf.reshape(batch, seq, d)
```

```python
import functools
import math

import jax
import jax.numpy as jnp
import numpy as np
from jax import lax
from jax.experimental import pallas as pl
from jax.experimental.pallas import tpu as pltpu

F32 = jnp.float32
BF16 = jnp.bfloat16

D_MODEL = 2048
HEAD_DIM = 64
PAIR = 2 * HEAD_DIM
A_Q_HEADS = 16
A_KV_HEADS = 4
A_WINDOW = 128
B_Q_HEADS = 16
B_KV_HEADS = 2
B_GROUP = B_Q_HEADS // B_KV_HEADS
A_GROUP = A_Q_HEADS // A_KV_HEADS
NSA_CMP_LEN = 32
NSA_CMP_STRIDE = 16
NSA_CMP_HIDDEN = 4 * HEAD_DIM
NSA_SEL_LEN = 64
NSA_TOP_N = 16
NSA_WINDOW = 512
NSA_FORCE_SCORE = 1.0e4
C_HEADS = 32
D_FF = 4 * D_MODEL
REL_BUCKETS = 32
REL_MAX_DIST = 1024
Q_BLOCK = 128
RMS_EPS = 1e-6
ATTN_SCALE = HEAD_DIM ** -0.5
LOG2E = 1.4426950408889634
NEAR_TILES = 8
NEG_INF = float("-inf")
ROW_CHUNK = 32

VMEM_LIMIT_BYTES = 56 * 1024 * 1024


def _cparams(*sem):
    return pltpu.CompilerParams(dimension_semantics=sem, vmem_limit_bytes=VMEM_LIMIT_BYTES)


def _nt(a, b):
    return lax.dot_general(a, b, (((1,), (1,)), ((), ())), preferred_element_type=F32)


def _split3(x):
    hi = x.astype(BF16)
    r1 = x - hi.astype(F32)
    mid = r1.astype(BF16)
    lo = (r1 - mid.astype(F32)).astype(BF16)
    return hi, mid, lo


def _rms(x, g):
    ms = jnp.mean(x * x, axis=-1, keepdims=True)
    return x * lax.rsqrt(ms + RMS_EPS) * g


def _stack_pairs(q_ref, first_pair, n_pairs):
    lane = lax.broadcasted_iota(jnp.int32, (Q_BLOCK, PAIR), 1)
    lo = lane < HEAD_DIM
    parts = []
    for p in range(n_pairs):
        qp = q_ref[:, (first_pair + p) * PAIR:(first_pair + p + 1) * PAIR]
        parts.append(jnp.where(lo, qp, jnp.zeros_like(qp)))
        parts.append(jnp.where(lo, jnp.zeros_like(qp), qp))
    return jnp.concatenate(parts, axis=0)


def _unstack_pairs(o, n_pairs, rows):
    lane = lax.broadcasted_iota(jnp.int32, (rows, PAIR), 1)
    lo = lane < HEAD_DIM
    return [jnp.where(lo, o[(2 * p) * rows:(2 * p + 1) * rows], o[(2 * p + 1) * rows:(2 * p + 2) * rows])
            for p in range(n_pairs)]


def _norm_mm_kernel(x_ref, g_ref, w_ref, o_ref, h_ref):
    @pl.when(pl.program_id(1) == 0)
    def _():
        h_ref[...] = _rms(x_ref[...], g_ref[...]).astype(BF16)

    o_ref[...] = jnp.dot(h_ref[...], w_ref[...], preferred_element_type=F32).astype(o_ref.dtype)


def norm_matmul(x, g, w, out_dtype, tm=1024, tn=1024):
    m, d = x.shape
    n = w.shape[1]
    tn = min(tn, n)
    return pl.pallas_call(
        _norm_mm_kernel,
        grid=(m // tm, n // tn),
        in_specs=[pl.BlockSpec((tm, d), lambda i, j: (i, 0)),
                  pl.BlockSpec((1, d), lambda i, j: (0, 0)),
                  pl.BlockSpec((d, tn), lambda i, j: (0, j))],
        out_specs=pl.BlockSpec((tm, tn), lambda i, j: (i, j)),
        out_shape=jax.ShapeDtypeStruct((m, n), out_dtype),
        scratch_shapes=[pltpu.VMEM((tm, d), BF16)],
        compiler_params=_cparams("parallel", "arbitrary"),
        name="norm_matmul",
    )(x, g, w)


def _mix_ffn_kernel(x_ref, a1_ref, a2_ref, wo1_ref, wo2_ref, g_ref, wu_ref, wd_ref, gf_ref, o_ref, h_ref, acc_ref,
                    *, final_norm):
    k = pl.program_id(1)

    @pl.when(k == 0)
    def _():
        y = jnp.dot(a1_ref[...], wo1_ref[...], preferred_element_type=F32)
        y = y + jnp.dot(a2_ref[...], wo2_ref[...], preferred_element_type=F32)
        x1 = x_ref[...] + y
        acc_ref[...] = x1
        h_ref[...] = _rms(x1, g_ref[...]).astype(BF16)

    u = jnp.dot(h_ref[...], wu_ref[...], preferred_element_type=F32)
    u = jnp.maximum(u, 0.0)
    acc_ref[...] += jnp.dot((u * u).astype(BF16), wd_ref[...], preferred_element_type=F32)

    @pl.when(k == pl.num_programs(1) - 1)
    def _():
        y = acc_ref[...]
        if final_norm:
            y = _rms(y, gf_ref[...])
        o_ref[...] = y


def mix_out_ffn(x, a1, a1_blk, a2, a2_blk, w_out, w_out_idx, g, w_up, w_down, layer, g_final, final_norm,
                tm=512, tf=1024):
    m, d = x.shape
    ff = w_up.shape[2]
    half = d // 2
    return pl.pallas_call(
        functools.partial(_mix_ffn_kernel, final_norm=final_norm),
        grid=(m // tm, ff // tf),
        in_specs=[pl.BlockSpec((tm, d), lambda i, k: (i, 0)),
                  pl.BlockSpec((tm, half), lambda i, k: (i, a1_blk)),
                  pl.BlockSpec((tm, half), lambda i, k: (i, a2_blk)),
                  pl.BlockSpec((None, half, d), lambda i, k: (w_out_idx, 0, 0), pipeline_mode=pl.Buffered(1)),
                  pl.BlockSpec((None, half, d), lambda i, k: (w_out_idx, 1, 0), pipeline_mode=pl.Buffered(1)),
                  pl.BlockSpec((1, d), lambda i, k: (0, 0)),
                  pl.BlockSpec((None, d, tf), lambda i, k: (layer, 0, k)),
                  pl.BlockSpec((None, tf, d), lambda i, k: (layer, k, 0)),
                  pl.BlockSpec((1, d), lambda i, k: (0, 0))],
        out_specs=pl.BlockSpec((tm, d), lambda i, k: (i, 0)),
        out_shape=jax.ShapeDtypeStruct((m, d), F32),
        scratch_shapes=[pltpu.VMEM((tm, d), BF16), pltpu.VMEM((tm, d), F32)],
        compiler_params=_cparams("parallel", "arbitrary"),
        name="mix_ffn",
    )(x, a1, a2, w_out, w_out, g, w_up, w_down, g_final)


def _swa_kernel(q_ref, kp_ref, kc_ref, vp_ref, vc_ref, bias_ref, sink_ref, o_ref,
                s_ref, p_ref, m_ref, l_ref, a_ref, pm_ref):
    first_block = jnp.where(pl.program_id(1) == 0, 1, 0)
    rows = A_GROUP * Q_BLOCK
    chunks_per_head = Q_BLOCK // ROW_CHUNK
    lane = lax.broadcasted_iota(jnp.int32, (Q_BLOCK, PAIR), 1)
    for g in range(A_KV_HEADS):
        qs = _stack_pairs(q_ref, g * (A_GROUP // 2), A_GROUP // 2)
        k = jnp.concatenate([kp_ref[:, g * PAIR:(g + 1) * PAIR], kc_ref[:, g * PAIR:(g + 1) * PAIR]], axis=0)
        v = jnp.concatenate([vp_ref[:, g * PAIR:(g + 1) * PAIR], vc_ref[:, g * PAIR:(g + 1) * PAIR]], axis=0)
        s_ref[g] = _nt(qs, k)
        m_ref[g] = jnp.concatenate([jnp.broadcast_to(sink_ref[g * A_GROUP + r], (Q_BLOCK, PAIR))
                                    for r in range(A_GROUP)], axis=0)
        l_ref[g] = jnp.concatenate([jnp.where(lane == 0, 1.0, 0.0)] * A_GROUP, axis=0)

        def lane_tiles(c, u):
            head = g * A_GROUP + c // chunks_per_head
            q0 = (c % chunks_per_head) * ROW_CHUNK
            return (s_ref[g, c * ROW_CHUNK:(c + 1) * ROW_CHUNK, u * Q_BLOCK:(u + 1) * Q_BLOCK]
                    + bias_ref[first_block, head, q0:q0 + ROW_CHUNK, u * Q_BLOCK:(u + 1) * Q_BLOCK])

        _softmax_tile(lane_tiles, rows // ROW_CHUNK, ROW_CHUNK, 2, m_ref.at[g], l_ref.at[g], a_ref.at[g],
                      pm_ref.at[g], p_ref.at[g])
        o = jnp.dot(p_ref[g], v, preferred_element_type=F32)
        o = o / jnp.sum(l_ref[g], axis=-1, keepdims=True)
        for p, blk in enumerate(_unstack_pairs(o, A_GROUP // 2, Q_BLOCK)):
            c0 = (g * (A_GROUP // 2) + p) * PAIR
            o_ref[:, c0:c0 + PAIR] = blk.astype(o_ref.dtype)


def swa_attention(qkv, bias_a, sinks, batch, seq):
    nb = seq // Q_BLOCK
    m = batch * seq
    qa_w = A_Q_HEADS * HEAD_DIM
    kv_w = A_KV_HEADS * PAIR
    k_blk = qa_w // kv_w
    v_blk = k_blk + 1
    row = lambda b, i: b * nb + i
    prev = lambda b, i: b * nb + jnp.maximum(i - 1, 0)
    return pl.pallas_call(
        _swa_kernel,
        grid=(batch, nb),
        in_specs=[pl.BlockSpec((Q_BLOCK, qa_w), lambda b, i: (row(b, i), 0)),
                  pl.BlockSpec((Q_BLOCK, kv_w), lambda b, i: (prev(b, i), k_blk)),
                  pl.BlockSpec((Q_BLOCK, kv_w), lambda b, i: (row(b, i), k_blk)),
                  pl.BlockSpec((Q_BLOCK, kv_w), lambda b, i: (prev(b, i), v_blk)),
                  pl.BlockSpec((Q_BLOCK, kv_w), lambda b, i: (row(b, i), v_blk)),
                  pl.BlockSpec((2, A_Q_HEADS, Q_BLOCK, 2 * Q_BLOCK), lambda b, i: (0, 0, 0, 0)),
                  pl.BlockSpec((A_Q_HEADS, 1, PAIR), lambda b, i: (0, 0, 0))],
        out_specs=pl.BlockSpec((Q_BLOCK, qa_w), lambda b, i: (row(b, i), 0)),
        out_shape=jax.ShapeDtypeStruct((m, qa_w), BF16),
        scratch_shapes=[pltpu.VMEM((A_KV_HEADS, A_GROUP * Q_BLOCK, 2 * Q_BLOCK), F32),
                        pltpu.VMEM((A_KV_HEADS, A_GROUP * Q_BLOCK, 2 * Q_BLOCK), BF16)]
        + [pltpu.VMEM((A_KV_HEADS, A_GROUP * Q_BLOCK, PAIR), F32)] * 4,
        compiler_params=_cparams("parallel", "parallel"),
        name="swa",
    )(qkv, qkv, qkv, qkv, qkv, bias_a, sinks)


def _compress_kernel(r_ref, pe_ref, w1_ref, w2_ref, o_ref):
    half = NSA_CMP_STRIDE * HEAD_DIM
    r = r_ref[0, 0]
    xa = (r + pe_ref[0, :, :half]).astype(BF16)
    xb = (r + pe_ref[0, :, half:]).astype(BF16)
    a = jnp.dot(xa, w1_ref[0, :half, :], preferred_element_type=F32)
    b = jnp.dot(xb, w1_ref[0, half:, :], preferred_element_type=F32)
    n = r.shape[0]
    hid = jax.nn.gelu(a + pltpu.roll(b, n - 1, 0))
    o_ref[0, 0] = jnp.dot(hid.astype(BF16), w2_ref[0], preferred_element_type=F32).astype(o_ref.dtype)


def nsa_compress(r, pe, w1, w2dup):
    batch, _, n, width = r.shape
    return pl.pallas_call(
        _compress_kernel,
        grid=(batch, 2 * B_KV_HEADS),
        in_specs=[pl.BlockSpec((1, 1, n, width), lambda b, w: (b, w, 0, 0)),
                  pl.BlockSpec((1, 1, 2 * width), lambda b, w: (w // B_KV_HEADS, 0, 0)),
                  pl.BlockSpec((1, 2 * width, NSA_CMP_HIDDEN), lambda b, w: (w // B_KV_HEADS, 0, 0)),
                  pl.BlockSpec((1, NSA_CMP_HIDDEN, PAIR), lambda b, w: (w // B_KV_HEADS, 0, 0))],
        out_specs=pl.BlockSpec((1, 1, n, PAIR), lambda b, w: (b, w, 0, 0)),
        out_shape=jax.ShapeDtypeStruct((batch, 2 * B_KV_HEADS, n, PAIR), BF16),
        compiler_params=_cparams("parallel", "parallel"),
        name="nsa_compress",
    )(r, pe, w1, w2dup)


NSA_KEY_TILE = 512
NSA_STREAMS = 2
WIN_TILES = NSA_WINDOW // Q_BLOCK + 1
TAB_FAR = NEAR_TILES
TAB_WIN_EDGE = NEAR_TILES + 1
TAB_NONE = NEAR_TILES + 2
TAB_ENTRIES = NEAR_TILES + 3
UNSELECTED = -1.0e30


def _nsa_kernel(q_ref, kcm_ref, vcm_ref, ks_ref, vs_ref, kw_ref, vw_ref, gate_ref, bias_c_ref, tab_ref, ovt_ref,
                o_ref, val_ref, psum_ref, s_ref, p_ref, m_ref, l_ref, a_ref, pm_ref, acc_ref, *, n_cmp_pad):
    i = pl.program_id(2)
    rows = B_GROUP * Q_BLOCK
    srows = rows // NSA_STREAMS
    heads_per_stream = B_GROUP // NSA_STREAMS
    chunks_per_head = Q_BLOCK // ROW_CHUNK
    n_sel = val_ref.shape[0]
    sel_shift = int(math.log2(NSA_SEL_LEN))
    qs = _stack_pairs(q_ref, 0, B_GROUP // 2)

    qs_st = [qs[st * srows:(st + 1) * srows] for st in range(NSA_STREAMS)]

    c_tiles = n_cmp_pad // Q_BLOCK
    psum_ref[...] = jnp.zeros(psum_ref.shape, F32)
    o_c = []
    for st in range(NSA_STREAMS):
        s_ref[st, :, :n_cmp_pad] = _nt(qs_st[st], kcm_ref[0, 0])

        def cmp_block(c, u):
            return (slice(c * ROW_CHUNK, (c + 1) * ROW_CHUNK), slice(u * Q_BLOCK, (u + 1) * Q_BLOCK),
                    st * heads_per_stream + c // chunks_per_head, (c % chunks_per_head) * ROW_CHUNK)

        for c in range(srows // ROW_CHUNK):
            pm = None
            for u in range(c_tiles):
                rws, cols, head, q0 = cmp_block(c, u)
                t = s_ref[st, rws, cols] + bias_c_ref[head, q0:q0 + ROW_CHUNK, cols]
                s_ref[st, rws, cols] = t
                pm = t if pm is None else jnp.maximum(pm, t)
            pm_ref[st, c * ROW_CHUNK:(c + 1) * ROW_CHUNK, :] = pm
        m_c = jnp.max(pm_ref[st], axis=-1, keepdims=True)
        m_ref[st] = jnp.broadcast_to(jnp.where(m_c == NEG_INF, 0.0, m_c), m_ref.shape[1:])
        for c in range(srows // ROW_CHUNK):
            mb = m_ref[st, c * ROW_CHUNK:(c + 1) * ROW_CHUNK, :]
            esum = None
            for u in range(c_tiles):
                rws, cols, _, _ = cmp_block(c, u)
                e = jnp.exp2(s_ref[st, rws, cols] - mb)
                s_ref[st, rws, cols] = e
                esum = e if esum is None else esum + e
            l_ref[st, c * ROW_CHUNK:(c + 1) * ROW_CHUNK, :] = esum
        denom = jnp.maximum(jnp.sum(l_ref[st], axis=-1, keepdims=True), 1e-30)
        a_ref[st] = jnp.broadcast_to(1.0 / denom, a_ref.shape[1:])
        for c in range(srows // ROW_CHUNK):
            inv = a_ref[st, c * ROW_CHUNK:(c + 1) * ROW_CHUNK, :]
            for u in range(c_tiles):
                rws, cols, _, q0 = cmp_block(c, u)
                p = s_ref[st, rws, cols] * inv
                p_ref[st, rws, cols] = p.astype(BF16)
                psum_ref[q0:q0 + ROW_CHUNK, cols] += p
        o_c.append(jnp.dot(p_ref[st, :, :n_cmp_pad], vcm_ref[0, 0], preferred_element_type=F32))
    o_c = jnp.concatenate(o_c, axis=0)

    p_sum = psum_ref[...]
    ovt = ovt_ref[...]
    imp = None
    for piece in _split3(p_sum):
        t = _nt(ovt, piece)
        imp = t if imp is None else imp + t
    blk = lax.broadcasted_iota(jnp.int32, (n_sel, Q_BLOCK), 0)
    qpos = lax.broadcasted_iota(jnp.int32, (n_sel, Q_BLOCK), 1) + i * Q_BLOCK
    cur = lax.shift_right_logical(qpos, sel_shift)
    forced = (blk == 0) | (blk == cur) | (blk == cur - 1)
    future = blk * NSA_SEL_LEN > qpos
    val = jnp.where(future, NEG_INF, jnp.where(forced, NSA_FORCE_SCORE, imp))
    val_ref[...] = val
    group = 8
    ranks = []
    for g0 in range(0, n_sel, group):
        val_g = val[g0:g0 + group]
        blk_g = blk[g0:g0 + group]
        rank_g = jnp.zeros((group, Q_BLOCK), F32)
        for s2 in range(n_sel):
            other = val_ref[s2:s2 + 1, :]
            if s2 < g0:
                ahead = other >= val_g
            elif s2 >= g0 + group:
                ahead = other > val_g
            else:
                ahead = ((blk_g > s2) & (other >= val_g)) | (other > val_g)
            rank_g = rank_g + jnp.where(ahead, 1.0, 0.0)
        ranks.append(rank_g)
    rank = jnp.concatenate(ranks, axis=0)
    sel_t = jnp.where((rank < float(NSA_TOP_N)) & (val > NEG_INF), 1.0, 0.0).astype(BF16)
    if n_sel < Q_BLOCK:
        sel_t = jnp.concatenate([sel_t, jnp.zeros((Q_BLOCK - n_sel, Q_BLOCK), BF16)], axis=0)
    eye = jnp.where(lax.broadcasted_iota(jnp.int32, (Q_BLOCK, Q_BLOCK), 0)
                    == lax.broadcasted_iota(jnp.int32, (Q_BLOCK, Q_BLOCK), 1), 1.0, 0.0).astype(BF16)
    unsel = jnp.where(_nt(eye, sel_t) > 0.5, 0.0, UNSELECTED).astype(BF16)
    unsel_rows = jnp.concatenate([unsel] * heads_per_stream, axis=0)
    qs_sel = [jnp.concatenate([qs_st[st], unsel_rows], axis=1) for st in range(NSA_STREAMS)]

    key_lane = lax.broadcasted_iota(jnp.int32, (NSA_KEY_TILE, PAIR), 1)
    key_blk = lax.shift_right_logical(lax.broadcasted_iota(jnp.int32, (NSA_KEY_TILE, PAIR), 0), sel_shift)
    lane_minus_blk = key_lane - key_blk

    sel_refs = (s_ref, p_ref, m_ref, l_ref, a_ref, pm_ref, acc_ref)

    def reset(refs):
        _, _, m_r, l_r, _, _, acc_r = refs
        m_r[...] = jnp.full(m_r.shape, NEG_INF, F32)
        l_r[...] = jnp.zeros(l_r.shape, F32)
        acc_r[...] = jnp.zeros(acc_r.shape, F32)

    def soft_pv(refs, st, tab_idx, v):
        s_r, p_r, m_r, l_r, a_r, pm_r, acc_r = refs
        width = len(tab_idx) * Q_BLOCK

        def raw(c, u):
            return s_r[st, c * ROW_CHUNK:(c + 1) * ROW_CHUNK, u * Q_BLOCK:(u + 1) * Q_BLOCK]

        def biased(c, u):
            head = st * heads_per_stream + c // chunks_per_head
            q0 = (c % chunks_per_head) * ROW_CHUNK
            return raw(c, u) + tab_ref[0, tab_idx[u], head, q0:q0 + ROW_CHUNK, :]

        def keep(c, u, s):
            s_r[st, c * ROW_CHUNK:(c + 1) * ROW_CHUNK, u * Q_BLOCK:(u + 1) * Q_BLOCK] = s

        _softmax_tile(biased, srows // ROW_CHUNK, ROW_CHUNK, len(tab_idx), m_r.at[st], l_r.at[st],
                      a_r.at[st], pm_r.at[st], p_r.at[st], keep=keep, kept_tiles=raw)
        acc_r[st] = a_r[st] * acc_r[st] + jnp.dot(p_r[st, :, :width], v, preferred_element_type=F32)

    def finish(refs):
        _, _, _, l_r, _, _, acc_r = refs
        return jnp.concatenate(
            [acc_r[st] / jnp.maximum(jnp.sum(l_r[st], axis=-1, keepdims=True), 1e-30)
             for st in range(NSA_STREAMS)], axis=0)

    blocks_per_tile = NSA_KEY_TILE // NSA_SEL_LEN
    lane_tiles_per_tile = NSA_KEY_TILE // Q_BLOCK

    def sel_scores(st, jt):
        off = pl.multiple_of(jt * NSA_KEY_TILE, NSA_KEY_TILE)
        block_onehot = jnp.where(lane_minus_blk == jt * blocks_per_tile, 1.0, 0.0).astype(BF16)
        k_aug = jnp.concatenate([ks_ref[pl.ds(off, NSA_KEY_TILE), :], block_onehot], axis=1)
        s_ref[st, :, :NSA_KEY_TILE] = _nt(qs_sel[st], k_aug)

    def sel_tile(jt, last):
        v = vs_ref[pl.ds(pl.multiple_of(jt * NSA_KEY_TILE, NSA_KEY_TILE), NSA_KEY_TILE), :]
        tab_idx = []
        for u in range(lane_tiles_per_tile):
            d = i - (jt * lane_tiles_per_tile + u)
            tab_idx.append(jnp.where(d < 0, TAB_NONE, jnp.minimum(d, TAB_FAR)))
        for st in range(NSA_STREAMS):
            soft_pv(sel_refs, st, tab_idx, v)
            if not last:
                sel_scores(st, jt + 1)

    def sel_body(jt, carry):
        sel_tile(jt, False)
        return carry

    n_tiles = lax.shift_right_logical(i + lane_tiles_per_tile, int(math.log2(lane_tiles_per_tile)))
    reset(sel_refs)
    for st in range(NSA_STREAMS):
        sel_scores(st, 0)
    lax.fori_loop(0, n_tiles - 1, sel_body, 0)
    sel_tile(n_tiles - 1, True)
    o_s = finish(sel_refs)

    first_blk = jnp.maximum(i - (WIN_TILES - 1), 0)
    off = pl.multiple_of(first_blk * Q_BLOCK, Q_BLOCK)
    k_win = kw_ref[pl.ds(off, WIN_TILES * Q_BLOCK), :]
    v_win = vw_ref[pl.ds(off, WIN_TILES * Q_BLOCK), :]
    win_idx = []
    for u in range(WIN_TILES):
        d = i - (first_blk + u)
        win_idx.append(jnp.where(d < 0, TAB_NONE, jnp.where(d == WIN_TILES - 1, TAB_WIN_EDGE, d)))
    reset(sel_refs)
    for st in range(NSA_STREAMS):
        s_ref[st] = _nt(qs_st[st], k_win)
    for st in range(NSA_STREAMS):
        soft_pv(sel_refs, st, win_idx, v_win)
    o_w = finish(sel_refs)

    gates = jax.nn.sigmoid(gate_ref[...])

    def gate_col(br):
        return jnp.concatenate([gates[:, 3 * r + br:3 * r + br + 1] for r in range(B_GROUP)], axis=0)

    o = gate_col(0) * o_c + gate_col(1) * o_s + gate_col(2) * o_w
    for p, blk_out in enumerate(_unstack_pairs(o, B_GROUP // 2, Q_BLOCK)):
        o_ref[:, p * PAIR:(p + 1) * PAIR] = blk_out.astype(o_ref.dtype)


def nsa_attention(qkv, cmp_kv, tail, bias_c, bias_tab, ovt, batch, seq):
    nb = seq // Q_BLOCK
    m = batch * seq
    n_cmp_pad = seq // NSA_CMP_STRIDE
    n_sel = seq // NSA_SEL_LEN
    assert n_sel <= Q_BLOCK and seq % NSA_KEY_TILE == 0 and seq >= WIN_TILES * Q_BLOCK
    assert n_cmp_pad % Q_BLOCK == 0 and n_cmp_pad <= NSA_KEY_TILE
    grp_w = B_GROUP * HEAD_DIM
    q_blk0 = (A_Q_HEADS * HEAD_DIM + 2 * A_KV_HEADS * PAIR) // grp_w
    kv_blk0 = (A_Q_HEADS * HEAD_DIM + 2 * A_KV_HEADS * PAIR + B_Q_HEADS * HEAD_DIM) // PAIR
    kv_spec = lambda t: pl.BlockSpec((seq, PAIR), lambda b, g, i: (b, kv_blk0 + t * B_KV_HEADS + g))
    srows = B_GROUP * Q_BLOCK // NSA_STREAMS

    def branch_scratch(width):
        return ([pltpu.VMEM((NSA_STREAMS, srows, width), F32), pltpu.VMEM((NSA_STREAMS, srows, width), BF16)]
                + [pltpu.VMEM((NSA_STREAMS, srows, PAIR), F32)] * 5)

    return pl.pallas_call(
        functools.partial(_nsa_kernel, n_cmp_pad=n_cmp_pad),
        grid=(batch, B_KV_HEADS, nb),
        in_specs=[pl.BlockSpec((Q_BLOCK, grp_w), lambda b, g, i: (b * nb + i, q_blk0 + g)),
                  pl.BlockSpec((1, 1, n_cmp_pad, PAIR), lambda b, g, i: (b, g, 0, 0)),
                  pl.BlockSpec((1, 1, n_cmp_pad, PAIR), lambda b, g, i: (b, B_KV_HEADS + g, 0, 0)),
                  kv_spec(0), kv_spec(1), kv_spec(2), kv_spec(3),
                  pl.BlockSpec((Q_BLOCK, PAIR), lambda b, g, i: (b * nb + i, 2 + g)),
                  pl.BlockSpec((B_GROUP, None, Q_BLOCK, n_cmp_pad), lambda b, g, i: (g, i, 0, 0)),
                  pl.BlockSpec((1, TAB_ENTRIES, B_GROUP, Q_BLOCK, Q_BLOCK), lambda b, g, i: (g, 0, 0, 0, 0)),
                  pl.BlockSpec((n_sel, n_cmp_pad), lambda b, g, i: (0, 0))],
        out_specs=pl.BlockSpec((Q_BLOCK, grp_w), lambda b, g, i: (b * nb + i, g)),
        out_shape=jax.ShapeDtypeStruct((m, B_Q_HEADS * HEAD_DIM), BF16),
        scratch_shapes=[pltpu.VMEM((n_sel, Q_BLOCK), F32), pltpu.VMEM((Q_BLOCK, n_cmp_pad), F32)]
        + branch_scratch(max(NSA_KEY_TILE, WIN_TILES * Q_BLOCK)),
        compiler_params=_cparams("parallel", "parallel", "arbitrary"),
        name="nsa",
    )(qkv, cmp_kv, cmp_kv, qkv, qkv, qkv, qkv, tail, bias_c, bias_tab, ovt)


DECAY_STEP = 512
FOX_SPLIT = 1


def _decay_kernel(f_ref, fb_ref, place_ref, o_ref, carry_ref):
    r_i = lax.broadcasted_iota(jnp.int32, (Q_BLOCK, Q_BLOCK), 0)
    c_i = lax.broadcasted_iota(jnp.int32, (Q_BLOCK, Q_BLOCK), 1)
    tri = jnp.where(c_i <= r_i, 1.0, 0.0).astype(BF16)

    @pl.when(pl.program_id(1) == 0)
    def _():
        carry_ref[...] = jnp.zeros_like(carry_ref)

    carry = carry_ref[...]
    for r in range(DECAY_STEP // Q_BLOCK):
        x = f_ref[r * Q_BLOCK:(r + 1) * Q_BLOCK, :] + fb_ref[...]
        ls = jax.nn.log_sigmoid(x)
        cs = jnp.broadcast_to(carry, ls.shape)
        for piece in _split3(ls):
            cs = cs + jnp.dot(tri, piece, preferred_element_type=F32)
        feat = None
        for n, piece in enumerate(_split3(cs * LOG2E)):
            term = jnp.dot(piece, place_ref[n], preferred_element_type=F32)
            feat = term if feat is None else feat + term
        o_ref[r * Q_BLOCK:(r + 1) * Q_BLOCK, :] = feat.astype(o_ref.dtype)
        carry = cs[Q_BLOCK - 1:Q_BLOCK, :]
    carry_ref[...] = carry


def _decay_placement():
    place = np.zeros((3, PAIR, (C_HEADS // 2) * PAIR), np.float32)
    for h in range(C_HEADS):
        for n in range(3):
            place[n, h, (h // 2) * PAIR + 3 * (h % 2) + n] = -1.0
    return jnp.asarray(place, dtype=BF16)


def fox_decay(f_tail, f_bias, batch, seq):
    steps = seq // DECAY_STEP
    width = (C_HEADS // 2) * PAIR
    return pl.pallas_call(
        _decay_kernel,
        grid=(batch, steps),
        in_specs=[pl.BlockSpec((DECAY_STEP, PAIR), lambda b, s: (b * steps + s, 0)),
                  pl.BlockSpec((1, PAIR), lambda b, s: (0, 0)),
                  pl.BlockSpec((3, PAIR, width), lambda b, s: (0, 0, 0))],
        out_specs=pl.BlockSpec((DECAY_STEP, width), lambda b, s: (b * steps + s, 0)),
        out_shape=jax.ShapeDtypeStruct((batch * seq, width), BF16),
        scratch_shapes=[pltpu.VMEM((1, PAIR), F32)],
        compiler_params=_cparams("parallel", "arbitrary"),
        name="fox_decay",
    )(f_tail, f_bias, _decay_placement())


def _softmax_tile(lane_tiles, n_chunks, rc, n_lane_tiles, m_ref, l_ref, a_ref, pm_ref, p_ref, keep=None,
                  kept_tiles=None):
    second = lane_tiles if kept_tiles is None else kept_tiles
    for c in range(n_chunks):
        pm = None
        for u in range(n_lane_tiles):
            s = lane_tiles(c, u)
            if s is None:
                continue
            if keep is not None:
                keep(c, u, s)
            pm = s if pm is None else jnp.maximum(pm, s)
        pm_ref[c * rc:(c + 1) * rc, :] = pm
    m_old = m_ref[...]
    m_new = jnp.maximum(m_old, jnp.max(pm_ref[...], axis=-1, keepdims=True))
    a_ref[...] = jnp.exp2(m_old - m_new)
    m_ref[...] = m_new
    for c in range(n_chunks):
        rows = slice(c * rc, (c + 1) * rc)
        mb = m_ref[rows, :]
        psum = None
        for u in range(n_lane_tiles):
            s = second(c, u)
            if s is None:
                p_ref[rows, u * Q_BLOCK:(u + 1) * Q_BLOCK] = jnp.zeros((rc, Q_BLOCK), BF16)
                continue
            p = jnp.exp2(s - mb)
            p_ref[rows, u * Q_BLOCK:(u + 1) * Q_BLOCK] = p.astype(BF16)
            psum = p if psum is None else psum + p
        l_ref[rows, :] = a_ref[rows, :] * l_ref[rows, :] + psum


def _fox_kernel(q_ref, k_ref, v_ref, e_ref, o_ref, s_ref, p_ref, m_ref, l_ref, a_ref, pm_ref, acc_ref, *, tq):
    t = pl.program_id(2)
    kw = tq
    n_lane_tiles = kw // Q_BLOCK
    srows = tq // FOX_SPLIT
    n_chunks = srows // ROW_CHUNK
    n_streams = 2 * FOX_SPLIT
    q = q_ref[...]
    lane = lax.broadcasted_iota(jnp.int32, (tq, PAIR), 1)
    lo = lane < HEAD_DIM
    zero = jnp.zeros_like(q)
    q_aug = [jnp.concatenate([jnp.where(lo if hh == 0 else ~lo, q, zero),
                              jnp.where((lane >= 3 * hh) & (lane < 3 * hh + 3), 1.0, 0.0).astype(BF16)], axis=1)
             for hh in range(2)]
    qs = [q_aug[st // FOX_SPLIT][(st % FOX_SPLIT) * srows:(st % FOX_SPLIT + 1) * srows] for st in range(n_streams)]
    m_ref[...] = jnp.full(m_ref.shape, NEG_INF, F32)
    l_ref[...] = jnp.zeros(l_ref.shape, F32)
    acc_ref[...] = jnp.zeros(acc_ref.shape, F32)
    col_minus_row = (lax.broadcasted_iota(jnp.int32, (ROW_CHUNK, Q_BLOCK), 1)
                     - lax.broadcasted_iota(jnp.int32, (ROW_CHUNK, Q_BLOCK), 0))

    def scores(st, j):
        off = pl.multiple_of(j * kw, kw)
        k_aug = jnp.concatenate([k_ref[pl.ds(off, kw), :], e_ref[pl.ds(off, kw), :]], axis=1)
        s_ref[st] = _nt(qs[st], k_aug)

    def tile(j, diagonal):
        v = v_ref[pl.ds(pl.multiple_of(j * kw, kw), kw), :]
        for st in range(n_streams):
            row0 = (st % FOX_SPLIT) * srows

            def lane_tiles(c, u):
                first_row, first_key = row0 + c * ROW_CHUNK, u * Q_BLOCK
                if diagonal and first_key > first_row + ROW_CHUNK - 1:
                    return None
                s = s_ref[st, c * ROW_CHUNK:(c + 1) * ROW_CHUNK, u * Q_BLOCK:(u + 1) * Q_BLOCK]
                if diagonal and first_key + Q_BLOCK - 1 > first_row:
                    s = jnp.where(col_minus_row <= first_row - first_key, s, NEG_INF)
                return s

            _softmax_tile(lane_tiles, n_chunks, ROW_CHUNK, n_lane_tiles, m_ref.at[st], l_ref.at[st], a_ref.at[st],
                          pm_ref.at[st], p_ref.at[st])
            acc_ref[st] = a_ref[st] * acc_ref[st] + jnp.dot(p_ref[st], v, preferred_element_type=F32)
            if not diagonal:
                scores(st, j + 1)

    def body(j, carry):
        tile(j, False)
        return carry

    for st in range(n_streams):
        scores(st, 0)
    lax.fori_loop(0, t, body, 0)
    tile(t, True)
    o = [jnp.concatenate([acc_ref[st] / jnp.maximum(jnp.sum(l_ref[st], axis=-1, keepdims=True), 1e-30)
                          for st in range(hh * FOX_SPLIT, (hh + 1) * FOX_SPLIT)], axis=0) for hh in range(2)]
    o_ref[...] = jnp.where(lo, o[0], o[1]).astype(o_ref.dtype)


def fox_attention(qkv, decay, batch, seq, tq=512):
    m = batch * seq
    n_pairs = C_HEADS // 2
    nt = seq // tq
    return pl.pallas_call(
        functools.partial(_fox_kernel, tq=tq),
        grid=(batch, n_pairs, nt),
        in_specs=[pl.BlockSpec((tq, PAIR), lambda b, h, t: (b * nt + t, h)),
                  pl.BlockSpec((seq, PAIR), lambda b, h, t: (b, n_pairs + h)),
                  pl.BlockSpec((seq, PAIR), lambda b, h, t: (b, 2 * n_pairs + h)),
                  pl.BlockSpec((seq, PAIR), lambda b, h, t: (b, h))],
        out_specs=pl.BlockSpec((tq, PAIR), lambda b, h, t: (b * nt + t, h)),
        out_shape=jax.ShapeDtypeStruct((m, C_HEADS * HEAD_DIM), BF16),
        scratch_shapes=[pltpu.VMEM((2 * FOX_SPLIT, tq // FOX_SPLIT, tq), F32),
                        pltpu.VMEM((2 * FOX_SPLIT, tq // FOX_SPLIT, tq), BF16)]
        + [pltpu.VMEM((2 * FOX_SPLIT, tq // FOX_SPLIT, PAIR), F32)] * 5,
        compiler_params=_cparams("parallel", "parallel", "arbitrary"),
        name="fox",
    )(qkv, qkv, qkv, decay)


def _rel_bucket(dist):
    n = jnp.maximum(dist, 0)
    max_exact = REL_BUCKETS // 2
    nf = jnp.maximum(n, 1).astype(jnp.float32)
    large = max_exact + (jnp.log(nf / max_exact) / math.log(REL_MAX_DIST / max_exact)
                         * (REL_BUCKETS - max_exact)).astype(jnp.int32)
    return jnp.where(n < max_exact, n, jnp.minimum(large, REL_BUCKETS - 1))


def _lookup(table, idx):
    onehot = (idx[..., None] == jnp.arange(table.shape[0])).astype(F32)
    return jnp.einsum("hn,...n->h...", table.T, onehot, precision=lax.Precision.HIGHEST)


def _bias_tables(rel_bias, seq):
    nb = seq // Q_BLOCK
    n_cmp_pad = seq // NSA_CMP_STRIDE
    ql = jnp.arange(Q_BLOCK)
    d = jnp.arange(NEAR_TILES)
    dist_t = d[:, None, None] * Q_BLOCK + ql[None, :, None] - ql[None, None, :]
    tt = _lookup(rel_bias, _rel_bucket(dist_t))
    dist_a = ql[:, None] + Q_BLOCK - jnp.arange(2 * Q_BLOCK)[None, :]
    seen = (dist_a >= 0) & (dist_a < A_WINDOW)
    own = jnp.arange(2 * Q_BLOCK)[None, :] >= Q_BLOCK
    bias_a = jnp.concatenate([tt[:A_Q_HEADS, 1], tt[:A_Q_HEADS, 0]], axis=-1) * LOG2E
    bias_a = jnp.stack([jnp.where(seen, bias_a, NEG_INF), jnp.where(seen & own, bias_a, NEG_INF)])
    tb = tt[A_Q_HEADS:]
    upper = (ql[None, :] > ql[:, None])[None]
    far = jnp.broadcast_to(rel_bias[REL_BUCKETS - 1, A_Q_HEADS:][:, None, None], tb[:, 0].shape)
    entries = ([jnp.where(upper, NEG_INF, tb[:, 0])] + [tb[:, d] for d in range(1, NEAR_TILES)]
               + [far, jnp.where(upper, tb[:, WIN_TILES - 1], NEG_INF), jnp.full_like(far, NEG_INF)])
    bias_tab = jnp.stack(entries, axis=1).reshape(B_KV_HEADS, B_GROUP, TAB_ENTRIES, Q_BLOCK, Q_BLOCK)
    bias_tab = jnp.transpose(bias_tab, (0, 2, 1, 3, 4)) * LOG2E
    cend = jnp.arange(n_cmp_pad) * NSA_CMP_STRIDE + NSA_CMP_LEN - 1
    dist_c = (jnp.arange(nb)[:, None, None] * Q_BLOCK + ql[None, :, None]) - cend[None, None, :]
    bias_c = _lookup(rel_bias[:, A_Q_HEADS:] * LOG2E, _rel_bucket(dist_c))
    visible = (dist_c >= 0) & (jnp.arange(n_cmp_pad) < n_cmp_pad - 1)
    bias_c = jnp.where(visible[None], bias_c, NEG_INF)
    return bias_a, bias_tab, bias_c


def _overlap_t(seq):
    n_cmp_pad = seq // NSA_CMP_STRIDE
    n_sel = seq // NSA_SEL_LEN
    cstart = np.arange(n_cmp_pad) * NSA_CMP_STRIDE
    sstart = np.arange(n_sel) * NSA_SEL_LEN
    ov = (cstart[None, :] < sstart[:, None] + NSA_SEL_LEN) & (cstart[None, :] + NSA_CMP_LEN > sstart[:, None])
    ov[:, n_cmp_pad - 1] = False
    return jnp.asarray(ov.astype(np.float32), dtype=BF16)


def _dup(w, n_heads):
    d = w.shape[0]
    w = w.reshape(d, n_heads, 1, HEAD_DIM)
    return jnp.broadcast_to(w, (d, n_heads, 2, HEAD_DIM)).reshape(d, n_heads * PAIR)


def _even_weights(w_in):
    sizes = (A_Q_HEADS * HEAD_DIM, A_KV_HEADS * HEAD_DIM, A_KV_HEADS * HEAD_DIM, B_Q_HEADS * HEAD_DIM) \
        + (B_KV_HEADS * HEAD_DIM,) * 6 + (3 * B_Q_HEADS,)
    qa, ka, va, qb, kc, vc, ksl, vsl, kwn, vwn, gt = jnp.split(w_in, np.cumsum(sizes)[:-1].tolist(), axis=-1)
    log2_scale = ATTN_SCALE * LOG2E
    main = jnp.concatenate([qa * log2_scale, _dup(ka, A_KV_HEADS), _dup(va, A_KV_HEADS), qb * log2_scale,
                            _dup(ksl, B_KV_HEADS), _dup(vsl, B_KV_HEADS),
                            _dup(kwn, B_KV_HEADS), _dup(vwn, B_KV_HEADS)], axis=-1).astype(BF16)
    d = w_in.shape[0]
    per_group = 3 * B_GROUP
    gates = [jnp.pad(gt[:, g * per_group:(g + 1) * per_group], ((0, 0), (0, PAIR - per_group)))
             for g in range(B_KV_HEADS)]
    tail = jnp.concatenate([kc, vc] + gates, axis=-1).astype(BF16)
    return main, tail


def _odd_weights(w_in):
    c_mix = C_HEADS * HEAD_DIM
    col_scale = np.concatenate([np.full(c_mix, ATTN_SCALE * LOG2E, np.float32), np.ones(2 * c_mix, np.float32)])
    main = (w_in[:, :3 * c_mix] * col_scale).astype(BF16)
    tail = jnp.pad(w_in[:, 3 * c_mix:], ((0, 0), (0, PAIR - C_HEADS))).astype(BF16)
    return main, tail


def kernel(x, rel_bias, norm_mix, norm_ffn, norm_final, w_in_even, w_out_even, a_sinks, nsa_pe_k, nsa_pe_v,
           nsa_cmp_k_w1, nsa_cmp_k_w2, nsa_cmp_v_w1, nsa_cmp_v_w2, w_in_odd, w_out_odd, fox_fgate_b,
           w_ffn_up, w_ffn_down):
    batch, seq, d = x.shape
    depth = norm_mix.shape[0]
    m = batch * seq
    xf = x.reshape(m, d)
    bias_a, bias_tab, bias_c = _bias_tables(rel_bias, seq)
    ovt = _overlap_t(seq)
    n_cmp_pad = seq // NSA_CMP_STRIDE
    g_final = norm_final.reshape(1, d)
    w_up_bf16, w_down_bf16 = w_ffn_up.astype(BF16), w_ffn_down.astype(BF16)
    w_out_even_bf16, w_out_odd_bf16 = w_out_even.astype(BF16), w_out_odd.astype(BF16)

    for layer in range(depth):
        g_mix = norm_mix[layer].reshape(1, d)
        if layer % 2 == 0:
            e = layer // 2
            w_main, w_tail = _even_weights(w_in_even[e])
            qkv = norm_matmul(xf, g_mix, w_main, BF16)
            tail = norm_matmul(xf, g_mix, w_tail, F32)
            sinks = jnp.broadcast_to((a_sinks[e] * LOG2E).reshape(A_Q_HEADS, 1, 1), (A_Q_HEADS, 1, PAIR))
            a_out = swa_attention(qkv, bias_a, sinks, batch, seq)
            r = tail[:, :2 * B_KV_HEADS * HEAD_DIM].reshape(batch, seq, 2 * B_KV_HEADS, HEAD_DIM)
            r = jnp.transpose(r, (0, 2, 1, 3)).reshape(batch, 2 * B_KV_HEADS, n_cmp_pad, NSA_CMP_STRIDE * HEAD_DIM)
            pe = jnp.stack([nsa_pe_k[e].reshape(1, -1), nsa_pe_v[e].reshape(1, -1)])
            w1 = jnp.stack([nsa_cmp_k_w1[e], nsa_cmp_v_w1[e]]).astype(BF16)
            w2 = jnp.stack([_dup(nsa_cmp_k_w2[e], 1), _dup(nsa_cmp_v_w2[e], 1)]).astype(BF16)
            cmp_kv = nsa_compress(r, pe, w1, w2)
            b_out = nsa_attention(qkv, cmp_kv, tail, bias_c, bias_tab, ovt, batch, seq)
            mixed = (a_out, 0, b_out, 0, w_out_even_bf16, e)
        else:
            o = layer // 2
            w_main, w_tail = _odd_weights(w_in_odd[o])
            qkv = norm_matmul(xf, g_mix, w_main, BF16)
            f_tail = norm_matmul(xf, g_mix, w_tail, F32)
            f_bias = jnp.pad(fox_fgate_b[o], (0, PAIR - C_HEADS)).reshape(1, PAIR)
            decay = fox_decay(f_tail, f_bias, batch, seq)
            c_out = fox_attention(qkv, decay, batch, seq)
            mixed = (c_out, 0, c_out, 1, w_out_odd_bf16, o)
        xf = mix_out_ffn(xf, *mixed, norm_ffn[layer].reshape(1, d), w_up_bf16, w_down_bf16, layer,
                         g_final, layer == depth - 1)
    return xf.reshape(batch, seq, d)
```

```python
import functools
import math

import jax
import jax.numpy as jnp
import numpy as np
from jax import lax
from jax.experimental import pallas as pl
from jax.experimental.pallas import tpu as pltpu

F32 = jnp.float32
BF16 = jnp.bfloat16

D_MODEL = 2048
HEAD_DIM = 64
PAIR = 2 * HEAD_DIM
A_Q_HEADS = 16
A_KV_HEADS = 4
A_WINDOW = 128
B_Q_HEADS = 16
B_KV_HEADS = 2
B_GROUP = B_Q_HEADS // B_KV_HEADS
A_GROUP = A_Q_HEADS // A_KV_HEADS
NSA_CMP_LEN = 32
NSA_CMP_STRIDE = 16
NSA_CMP_HIDDEN = 4 * HEAD_DIM
NSA_SEL_LEN = 64
NSA_TOP_N = 16
NSA_WINDOW = 512
NSA_FORCE_SCORE = 1.0e4
C_HEADS = 32
D_FF = 4 * D_MODEL
REL_BUCKETS = 32
REL_MAX_DIST = 1024
Q_BLOCK = 128
RMS_EPS = 1e-6
ATTN_SCALE = HEAD_DIM ** -0.5
LOG2E = 1.4426950408889634
NEAR_TILES = 8
NEG_INF = float("-inf")
ROW_CHUNK = 32

VMEM_LIMIT_BYTES = 56 * 1024 * 1024


def _cparams(*sem):
    return pltpu.CompilerParams(dimension_semantics=sem, vmem_limit_bytes=VMEM_LIMIT_BYTES)


def _nt(a, b):
    return lax.dot_general(a, b, (((1,), (1,)), ((), ())), preferred_element_type=F32)


def _split3(x):
    hi = x.astype(BF16)
    r1 = x - hi.astype(F32)
    mid = r1.astype(BF16)
    lo = (r1 - mid.astype(F32)).astype(BF16)
    return hi, mid, lo


def _rms(x, g):
    ms = jnp.mean(x * x, axis=-1, keepdims=True)
    return x * lax.rsqrt(ms + RMS_EPS) * g


def _stack_pairs(q_ref, first_pair, n_pairs):
    lane = lax.broadcasted_iota(jnp.int32, (Q_BLOCK, PAIR), 1)
    lo = lane < HEAD_DIM
    parts = []
    for p in range(n_pairs):
        qp = q_ref[:, (first_pair + p) * PAIR:(first_pair + p + 1) * PAIR]
        parts.append(jnp.where(lo, qp, jnp.zeros_like(qp)))
        parts.append(jnp.where(lo, jnp.zeros_like(qp), qp))
    return jnp.concatenate(parts, axis=0)


def _unstack_pairs(o, n_pairs, rows):
    lane = lax.broadcasted_iota(jnp.int32, (rows, PAIR), 1)
    lo = lane < HEAD_DIM
    return [jnp.where(lo, o[(2 * p) * rows:(2 * p + 1) * rows], o[(2 * p + 1) * rows:(2 * p + 2) * rows])
            for p in range(n_pairs)]


def _norm_mm_kernel(x_ref, g_ref, w_ref, wt_ref, o_ref, ot_ref, h_ref):
    @pl.when(pl.program_id(1) == 0)
    def _():
        h_ref[...] = _rms(x_ref[...], g_ref[...]).astype(BF16)
        ot_ref[...] = jnp.dot(h_ref[...], wt_ref[...], preferred_element_type=F32)

    o_ref[...] = jnp.dot(h_ref[...], w_ref[...], preferred_element_type=F32).astype(o_ref.dtype)


def norm_matmul(x, g, w, w_tail, tm=1024, tn=1024):
    m, d = x.shape
    n = w.shape[1]
    nt = w_tail.shape[1]
    return pl.pallas_call(
        _norm_mm_kernel,
        grid=(m // tm, n // tn),
        in_specs=[pl.BlockSpec((tm, d), lambda i, j: (i, 0)),
                  pl.BlockSpec((1, d), lambda i, j: (0, 0)),
                  pl.BlockSpec((d, tn), lambda i, j: (0, j)),
                  pl.BlockSpec((d, nt), lambda i, j: (0, 0))],
        out_specs=[pl.BlockSpec((tm, tn), lambda i, j: (i, j)),
                   pl.BlockSpec((tm, nt), lambda i, j: (i, 0))],
        out_shape=[jax.ShapeDtypeStruct((m, n), BF16), jax.ShapeDtypeStruct((m, nt), F32)],
        scratch_shapes=[pltpu.VMEM((tm, d), BF16)],
        compiler_params=_cparams("parallel", "arbitrary"),
        name="norm_matmul",
    )(x, g, w, w_tail)


def _mix_ffn_kernel(x_ref, a1_ref, a2_ref, wo1_ref, wo2_ref, g_ref, wu_ref, wd_ref, gf_ref, o_ref, h_ref, acc_ref,
                    *, final_norm):
    k = pl.program_id(1)

    @pl.when(k == 0)
    def _():
        y = jnp.dot(a1_ref[...], wo1_ref[...], preferred_element_type=F32)
        y = y + jnp.dot(a2_ref[...], wo2_ref[...], preferred_element_type=F32)
        x1 = x_ref[...] + y
        acc_ref[...] = x1
        h_ref[...] = _rms(x1, g_ref[...]).astype(BF16)

    u = jnp.dot(h_ref[...], wu_ref[...], preferred_element_type=F32)
    u = jnp.maximum(u, 0.0)
    acc_ref[...] += jnp.dot((u * u).astype(BF16), wd_ref[...], preferred_element_type=F32)

    @pl.when(k == pl.num_programs(1) - 1)
    def _():
        y = acc_ref[...]
        if final_norm:
            y = _rms(y, gf_ref[...])
        o_ref[...] = y


def mix_out_ffn(x, a1, a1_blk, a2, a2_blk, w_out, w_out_idx, g, w_up, w_down, layer, g_final, final_norm,
                tm=512, tf=1024):
    m, d = x.shape
    ff = w_up.shape[2]
    half = d // 2
    return pl.pallas_call(
        functools.partial(_mix_ffn_kernel, final_norm=final_norm),
        grid=(m // tm, ff // tf),
        in_specs=[pl.BlockSpec((tm, d), lambda i, k: (i, 0)),
                  pl.BlockSpec((tm, half), lambda i, k: (i, a1_blk)),
                  pl.BlockSpec((tm, half), lambda i, k: (i, a2_blk)),
                  pl.BlockSpec((None, half, d), lambda i, k: (w_out_idx, 0, 0), pipeline_mode=pl.Buffered(1)),
                  pl.BlockSpec((None, half, d), lambda i, k: (w_out_idx, 1, 0), pipeline_mode=pl.Buffered(1)),
                  pl.BlockSpec((1, d), lambda i, k: (0, 0)),
                  pl.BlockSpec((None, d, tf), lambda i, k: (layer, 0, k)),
                  pl.BlockSpec((None, tf, d), lambda i, k: (layer, k, 0)),
                  pl.BlockSpec((1, d), lambda i, k: (0, 0))],
        out_specs=pl.BlockSpec((tm, d), lambda i, k: (i, 0)),
        out_shape=jax.ShapeDtypeStruct((m, d), F32),
        scratch_shapes=[pltpu.VMEM((tm, d), BF16), pltpu.VMEM((tm, d), F32)],
        compiler_params=_cparams("parallel", "arbitrary"),
        name="mix_ffn",
    )(x, a1, a2, w_out, w_out, g, w_up, w_down, g_final)


def _swa_kernel(q_ref, kp_ref, kc_ref, vp_ref, vc_ref, bias_ref, sink_ref, o_ref,
                s_ref, p_ref, m_ref, l_ref, a_ref, pm_ref):
    first_block = jnp.where(pl.program_id(1) == 0, 1, 0)
    rows = A_GROUP * Q_BLOCK
    chunks_per_head = Q_BLOCK // ROW_CHUNK
    lane = lax.broadcasted_iota(jnp.int32, (Q_BLOCK, PAIR), 1)
    for g in range(A_KV_HEADS):
        qs = _stack_pairs(q_ref, g * (A_GROUP // 2), A_GROUP // 2)
        k = jnp.concatenate([kp_ref[:, g * PAIR:(g + 1) * PAIR], kc_ref[:, g * PAIR:(g + 1) * PAIR]], axis=0)
        v = jnp.concatenate([vp_ref[:, g * PAIR:(g + 1) * PAIR], vc_ref[:, g * PAIR:(g + 1) * PAIR]], axis=0)
        s_ref[g] = _nt(qs, k)
        m_ref[g] = jnp.concatenate([jnp.broadcast_to(sink_ref[g * A_GROUP + r], (Q_BLOCK, PAIR))
                                    for r in range(A_GROUP)], axis=0)
        l_ref[g] = jnp.concatenate([jnp.where(lane == 0, 1.0, 0.0)] * A_GROUP, axis=0)

        def lane_tiles(c, u):
            head = g * A_GROUP + c // chunks_per_head
            q0 = (c % chunks_per_head) * ROW_CHUNK
            return (s_ref[g, c * ROW_CHUNK:(c + 1) * ROW_CHUNK, u * Q_BLOCK:(u + 1) * Q_BLOCK]
                    + bias_ref[first_block, head, q0:q0 + ROW_CHUNK, u * Q_BLOCK:(u + 1) * Q_BLOCK])

        _softmax_tile(lane_tiles, rows // ROW_CHUNK, ROW_CHUNK, 2, m_ref.at[g], l_ref.at[g], a_ref.at[g],
                      pm_ref.at[g], p_ref.at[g])
        o = jnp.dot(p_ref[g], v, preferred_element_type=F32)
        o = o / jnp.sum(l_ref[g], axis=-1, keepdims=True)
        for p, blk in enumerate(_unstack_pairs(o, A_GROUP // 2, Q_BLOCK)):
            c0 = (g * (A_GROUP // 2) + p) * PAIR
            o_ref[:, c0:c0 + PAIR] = blk.astype(o_ref.dtype)


def swa_attention(qkv, bias_a, sinks, batch, seq):
    nb = seq // Q_BLOCK
    m = batch * seq
    qa_w = A_Q_HEADS * HEAD_DIM
    kv_w = A_KV_HEADS * PAIR
    k_blk = qa_w // kv_w
    v_blk = k_blk + 1
    row = lambda b, i: b * nb + i
    prev = lambda b, i: b * nb + jnp.maximum(i - 1, 0)
    return pl.pallas_call(
        _swa_kernel,
        grid=(batch, nb),
        in_specs=[pl.BlockSpec((Q_BLOCK, qa_w), lambda b, i: (row(b, i), 0)),
                  pl.BlockSpec((Q_BLOCK, kv_w), lambda b, i: (prev(b, i), k_blk)),
                  pl.BlockSpec((Q_BLOCK, kv_w), lambda b, i: (row(b, i), k_blk)),
                  pl.BlockSpec((Q_BLOCK, kv_w), lambda b, i: (prev(b, i), v_blk)),
                  pl.BlockSpec((Q_BLOCK, kv_w), lambda b, i: (row(b, i), v_blk)),
                  pl.BlockSpec((2, A_Q_HEADS, Q_BLOCK, 2 * Q_BLOCK), lambda b, i: (0, 0, 0, 0)),
                  pl.BlockSpec((A_Q_HEADS, 1, PAIR), lambda b, i: (0, 0, 0))],
        out_specs=pl.BlockSpec((Q_BLOCK, qa_w), lambda b, i: (row(b, i), 0)),
        out_shape=jax.ShapeDtypeStruct((m, qa_w), BF16),
        scratch_shapes=[pltpu.VMEM((A_KV_HEADS, A_GROUP * Q_BLOCK, 2 * Q_BLOCK), F32),
                        pltpu.VMEM((A_KV_HEADS, A_GROUP * Q_BLOCK, 2 * Q_BLOCK), BF16)]
        + [pltpu.VMEM((A_KV_HEADS, A_GROUP * Q_BLOCK, PAIR), F32)] * 4,
        compiler_params=_cparams("parallel", "parallel"),
        name="swa",
    )(qkv, qkv, qkv, qkv, qkv, bias_a, sinks)


def _compress_kernel(r_ref, pe_ref, w1_ref, w2_ref, o_ref):
    half = NSA_CMP_STRIDE * HEAD_DIM
    r = r_ref[0, 0]
    xa = (r + pe_ref[0, :, :half]).astype(BF16)
    xb = (r + pe_ref[0, :, half:]).astype(BF16)
    a = jnp.dot(xa, w1_ref[0, :half, :], preferred_element_type=F32)
    b = jnp.dot(xb, w1_ref[0, half:, :], preferred_element_type=F32)
    n = r.shape[0]
    hid = jax.nn.gelu(a + pltpu.roll(b, n - 1, 0))
    o_ref[0, 0] = jnp.dot(hid.astype(BF16), w2_ref[0], preferred_element_type=F32).astype(o_ref.dtype)


def nsa_compress(r, pe, w1, w2dup):
    batch, _, n, width = r.shape
    return pl.pallas_call(
        _compress_kernel,
        grid=(batch, 2 * B_KV_HEADS),
        in_specs=[pl.BlockSpec((1, 1, n, width), lambda b, w: (b, w, 0, 0)),
                  pl.BlockSpec((1, 1, 2 * width), lambda b, w: (w // B_KV_HEADS, 0, 0)),
                  pl.BlockSpec((1, 2 * width, NSA_CMP_HIDDEN), lambda b, w: (w // B_KV_HEADS, 0, 0)),
                  pl.BlockSpec((1, NSA_CMP_HIDDEN, PAIR), lambda b, w: (w // B_KV_HEADS, 0, 0))],
        out_specs=pl.BlockSpec((1, 1, n, PAIR), lambda b, w: (b, w, 0, 0)),
        out_shape=jax.ShapeDtypeStruct((batch, 2 * B_KV_HEADS, n, PAIR), BF16),
        compiler_params=_cparams("parallel", "parallel"),
        name="nsa_compress",
    )(r, pe, w1, w2dup)


NSA_KEY_TILE = 512
NSA_STREAMS = 2
WIN_TILES = NSA_WINDOW // Q_BLOCK + 1
TAB_FAR = NEAR_TILES
TAB_WIN_EDGE = NEAR_TILES + 1
TAB_NONE = NEAR_TILES + 2
TAB_ENTRIES = NEAR_TILES + 3
UNSELECTED = -1.0e30


def _nsa_kernel(q_ref, kcm_ref, vcm_ref, ks_ref, vs_ref, kw_ref, vw_ref, gate_ref, bias_c_ref, tab_ref, ovt_ref,
                o_ref, val_ref, psum_ref, s_ref, p_ref, m_ref, l_ref, a_ref, pm_ref, acc_ref, *, n_cmp_pad):
    i = pl.program_id(2)
    rows = B_GROUP * Q_BLOCK
    srows = rows // NSA_STREAMS
    heads_per_stream = B_GROUP // NSA_STREAMS
    chunks_per_head = Q_BLOCK // ROW_CHUNK
    n_sel = val_ref.shape[0]
    sel_shift = int(math.log2(NSA_SEL_LEN))
    qs = _stack_pairs(q_ref, 0, B_GROUP // 2)

    qs_st = [qs[st * srows:(st + 1) * srows] for st in range(NSA_STREAMS)]

    c_tiles = n_cmp_pad // Q_BLOCK
    psum_ref[...] = jnp.zeros(psum_ref.shape, F32)
    o_c = []
    for st in range(NSA_STREAMS):
        s_ref[st, :, :n_cmp_pad] = _nt(qs_st[st], kcm_ref[0, 0])

        def cmp_block(c, u):
            return (slice(c * ROW_CHUNK, (c + 1) * ROW_CHUNK), slice(u * Q_BLOCK, (u + 1) * Q_BLOCK),
                    st * heads_per_stream + c // chunks_per_head, (c % chunks_per_head) * ROW_CHUNK)

        for c in range(srows // ROW_CHUNK):
            pm = None
            for u in range(c_tiles):
                rws, cols, head, q0 = cmp_block(c, u)
                t = s_ref[st, rws, cols] + bias_c_ref[head, q0:q0 + ROW_CHUNK, cols]
                s_ref[st, rws, cols] = t
                pm = t if pm is None else jnp.maximum(pm, t)
            pm_ref[st, c * ROW_CHUNK:(c + 1) * ROW_CHUNK, :] = pm
        m_c = jnp.max(pm_ref[st], axis=-1, keepdims=True)
        m_ref[st] = jnp.broadcast_to(jnp.where(m_c == NEG_INF, 0.0, m_c), m_ref.shape[1:])
        for c in range(srows // ROW_CHUNK):
            mb = m_ref[st, c * ROW_CHUNK:(c + 1) * ROW_CHUNK, :]
            esum = None
            for u in range(c_tiles):
                rws, cols, _, _ = cmp_block(c, u)
                e = jnp.exp2(s_ref[st, rws, cols] - mb)
                s_ref[st, rws, cols] = e
                esum = e if esum is None else esum + e
            l_ref[st, c * ROW_CHUNK:(c + 1) * ROW_CHUNK, :] = esum
        denom = jnp.maximum(jnp.sum(l_ref[st], axis=-1, keepdims=True), 1e-30)
        a_ref[st] = jnp.broadcast_to(1.0 / denom, a_ref.shape[1:])
        for c in range(srows // ROW_CHUNK):
            inv = a_ref[st, c * ROW_CHUNK:(c + 1) * ROW_CHUNK, :]
            for u in range(c_tiles):
                rws, cols, _, q0 = cmp_block(c, u)
                p = s_ref[st, rws, cols] * inv
                p_ref[st, rws, cols] = p.astype(BF16)
                psum_ref[q0:q0 + ROW_CHUNK, cols] += p
        o_c.append(jnp.dot(p_ref[st, :, :n_cmp_pad], vcm_ref[0, 0], preferred_element_type=F32))
    o_c = jnp.concatenate(o_c, axis=0)

    p_sum = psum_ref[...]
    ovt = ovt_ref[...]
    imp = None
    for piece in _split3(p_sum):
        t = _nt(ovt, piece)
        imp = t if imp is None else imp + t
    blk = lax.broadcasted_iota(jnp.int32, (n_sel, Q_BLOCK), 0)
    qpos = lax.broadcasted_iota(jnp.int32, (n_sel, Q_BLOCK), 1) + i * Q_BLOCK
    cur = lax.shift_right_logical(qpos, sel_shift)
    forced = (blk == 0) | (blk == cur) | (blk == cur - 1)
    future = blk * NSA_SEL_LEN > qpos
    val = jnp.where(future, NEG_INF, jnp.where(forced, NSA_FORCE_SCORE, imp))
    val_ref[...] = val
    group = 8
    ranks = []
    for g0 in range(0, n_sel, group):
        val_g = val[g0:g0 + group]
        blk_g = blk[g0:g0 + group]
        rank_g = jnp.zeros((group, Q_BLOCK), F32)
        for s2 in range(n_sel):
            other = val_ref[s2:s2 + 1, :]
            if s2 < g0:
                ahead = other >= val_g
            elif s2 >= g0 + group:
                ahead = other > val_g
            else:
                ahead = ((blk_g > s2) & (other >= val_g)) | (other > val_g)
            rank_g = rank_g + jnp.where(ahead, 1.0, 0.0)
        ranks.append(rank_g)
    rank = jnp.concatenate(ranks, axis=0)
    sel_t = jnp.where((rank < float(NSA_TOP_N)) & (val > NEG_INF), 1.0, 0.0).astype(BF16)
    if n_sel < Q_BLOCK:
        sel_t = jnp.concatenate([sel_t, jnp.zeros((Q_BLOCK - n_sel, Q_BLOCK), BF16)], axis=0)
    eye = jnp.where(lax.broadcasted_iota(jnp.int32, (Q_BLOCK, Q_BLOCK), 0)
                    == lax.broadcasted_iota(jnp.int32, (Q_BLOCK, Q_BLOCK), 1), 1.0, 0.0).astype(BF16)
    unsel = jnp.where(_nt(eye, sel_t) > 0.5, 0.0, UNSELECTED).astype(BF16)
    unsel_rows = jnp.concatenate([unsel] * heads_per_stream, axis=0)
    qs_sel = [jnp.concatenate([qs_st[st], unsel_rows], axis=1) for st in range(NSA_STREAMS)]

    key_lane = lax.broadcasted_iota(jnp.int32, (NSA_KEY_TILE, PAIR), 1)
    key_blk = lax.shift_right_logical(lax.broadcasted_iota(jnp.int32, (NSA_KEY_TILE, PAIR), 0), sel_shift)
    lane_minus_blk = key_lane - key_blk

    sel_refs = (s_ref, p_ref, m_ref, l_ref, a_ref, pm_ref, acc_ref)

    def reset(refs):
        _, _, m_r, l_r, _, _, acc_r = refs
        m_r[...] = jnp.full(m_r.shape, NEG_INF, F32)
        l_r[...] = jnp.zeros(l_r.shape, F32)
        acc_r[...] = jnp.zeros(acc_r.shape, F32)

    def soft_pv(refs, st, tab_idx, v):
        s_r, p_r, m_r, l_r, a_r, pm_r, acc_r = refs
        width = len(tab_idx) * Q_BLOCK

        def raw(c, u):
            return s_r[st, c * ROW_CHUNK:(c + 1) * ROW_CHUNK, u * Q_BLOCK:(u + 1) * Q_BLOCK]

        def biased(c, u):
            head = st * heads_per_stream + c // chunks_per_head
            q0 = (c % chunks_per_head) * ROW_CHUNK
            return raw(c, u) + tab_ref[0, tab_idx[u], head, q0:q0 + ROW_CHUNK, :]

        def keep(c, u, s):
            s_r[st, c * ROW_CHUNK:(c + 1) * ROW_CHUNK, u * Q_BLOCK:(u + 1) * Q_BLOCK] = s

        _softmax_tile(biased, srows // ROW_CHUNK, ROW_CHUNK, len(tab_idx), m_r.at[st], l_r.at[st],
                      a_r.at[st], pm_r.at[st], p_r.at[st], keep=keep, kept_tiles=raw)
        acc_r[st] = a_r[st] * acc_r[st] + jnp.dot(p_r[st, :, :width], v, preferred_element_type=F32)

    def finish(refs):
        _, _, _, l_r, _, _, acc_r = refs
        return jnp.concatenate(
            [acc_r[st] / jnp.maximum(jnp.sum(l_r[st], axis=-1, keepdims=True), 1e-30)
             for st in range(NSA_STREAMS)], axis=0)

    blocks_per_tile = NSA_KEY_TILE // NSA_SEL_LEN
    lane_tiles_per_tile = NSA_KEY_TILE // Q_BLOCK

    def sel_scores(st, jt):
        off = pl.multiple_of(jt * NSA_KEY_TILE, NSA_KEY_TILE)
        block_onehot = jnp.where(lane_minus_blk == jt * blocks_per_tile, 1.0, 0.0).astype(BF16)
        k_aug = jnp.concatenate([ks_ref[pl.ds(off, NSA_KEY_TILE), :], block_onehot], axis=1)
        s_ref[st, :, :NSA_KEY_TILE] = _nt(qs_sel[st], k_aug)

    def sel_tile(jt, last):
        v = vs_ref[pl.ds(pl.multiple_of(jt * NSA_KEY_TILE, NSA_KEY_TILE), NSA_KEY_TILE), :]
        tab_idx = []
        for u in range(lane_tiles_per_tile):
            d = i - (jt * lane_tiles_per_tile + u)
            tab_idx.append(jnp.where(d < 0, TAB_NONE, jnp.minimum(d, TAB_FAR)))
        for st in range(NSA_STREAMS):
            soft_pv(sel_refs, st, tab_idx, v)
            if not last:
                sel_scores(st, jt + 1)

    def sel_body(jt, carry):
        sel_tile(jt, False)
        return carry

    n_tiles = lax.shift_right_logical(i + lane_tiles_per_tile, int(math.log2(lane_tiles_per_tile)))
    reset(sel_refs)
    for st in range(NSA_STREAMS):
        sel_scores(st, 0)
    lax.fori_loop(0, n_tiles - 1, sel_body, 0)
    sel_tile(n_tiles - 1, True)
    o_s = finish(sel_refs)

    first_blk = jnp.maximum(i - (WIN_TILES - 1), 0)
    off = pl.multiple_of(first_blk * Q_BLOCK, Q_BLOCK)
    k_win = kw_ref[pl.ds(off, WIN_TILES * Q_BLOCK), :]
    v_win = vw_ref[pl.ds(off, WIN_TILES * Q_BLOCK), :]
    win_idx = []
    for u in range(WIN_TILES):
        d = i - (first_blk + u)
        win_idx.append(jnp.where(d < 0, TAB_NONE, jnp.where(d == WIN_TILES - 1, TAB_WIN_EDGE, d)))
    reset(sel_refs)
    for st in range(NSA_STREAMS):
        s_ref[st] = _nt(qs_st[st], k_win)
    for st in range(NSA_STREAMS):
        soft_pv(sel_refs, st, win_idx, v_win)
    o_w = finish(sel_refs)

    gates = jax.nn.sigmoid(gate_ref[...])

    def gate_col(br):
        return jnp.concatenate([gates[:, 3 * r + br:3 * r + br + 1] for r in range(B_GROUP)], axis=0)

    o = gate_col(0) * o_c + gate_col(1) * o_s + gate_col(2) * o_w
    for p, blk_out in enumerate(_unstack_pairs(o, B_GROUP // 2, Q_BLOCK)):
        o_ref[:, p * PAIR:(p + 1) * PAIR] = blk_out.astype(o_ref.dtype)


def nsa_attention(qkv, cmp_kv, tail, bias_c, bias_tab, ovt, batch, seq):
    nb = seq // Q_BLOCK
    m = batch * seq
    n_cmp_pad = seq // NSA_CMP_STRIDE
    n_sel = seq // NSA_SEL_LEN
    assert n_sel <= Q_BLOCK and seq % NSA_KEY_TILE == 0 and seq >= WIN_TILES * Q_BLOCK
    assert n_cmp_pad % Q_BLOCK == 0 and n_cmp_pad <= NSA_KEY_TILE
    grp_w = B_GROUP * HEAD_DIM
    q_blk0 = (A_Q_HEADS * HEAD_DIM + 2 * A_KV_HEADS * PAIR) // grp_w
    kv_blk0 = (A_Q_HEADS * HEAD_DIM + 2 * A_KV_HEADS * PAIR + B_Q_HEADS * HEAD_DIM) // PAIR
    kv_spec = lambda t: pl.BlockSpec((seq, PAIR), lambda b, g, i: (b, kv_blk0 + t * B_KV_HEADS + g))
    srows = B_GROUP * Q_BLOCK // NSA_STREAMS

    def branch_scratch(width):
        return ([pltpu.VMEM((NSA_STREAMS, srows, width), F32), pltpu.VMEM((NSA_STREAMS, srows, width), BF16)]
                + [pltpu.VMEM((NSA_STREAMS, srows, PAIR), F32)] * 5)

    return pl.pallas_call(
        functools.partial(_nsa_kernel, n_cmp_pad=n_cmp_pad),
        grid=(batch, B_KV_HEADS, nb),
        in_specs=[pl.BlockSpec((Q_BLOCK, grp_w), lambda b, g, i: (b * nb + i, q_blk0 + g)),
                  pl.BlockSpec((1, 1, n_cmp_pad, PAIR), lambda b, g, i: (b, g, 0, 0)),
                  pl.BlockSpec((1, 1, n_cmp_pad, PAIR), lambda b, g, i: (b, B_KV_HEADS + g, 0, 0)),
                  kv_spec(0), kv_spec(1), kv_spec(2), kv_spec(3),
                  pl.BlockSpec((Q_BLOCK, PAIR), lambda b, g, i: (b * nb + i, 2 + g)),
                  pl.BlockSpec((B_GROUP, None, Q_BLOCK, n_cmp_pad), lambda b, g, i: (g, i, 0, 0)),
                  pl.BlockSpec((1, TAB_ENTRIES, B_GROUP, Q_BLOCK, Q_BLOCK), lambda b, g, i: (g, 0, 0, 0, 0)),
                  pl.BlockSpec((n_sel, n_cmp_pad), lambda b, g, i: (0, 0))],
        out_specs=pl.BlockSpec((Q_BLOCK, grp_w), lambda b, g, i: (b * nb + i, g)),
        out_shape=jax.ShapeDtypeStruct((m, B_Q_HEADS * HEAD_DIM), BF16),
        scratch_shapes=[pltpu.VMEM((n_sel, Q_BLOCK), F32), pltpu.VMEM((Q_BLOCK, n_cmp_pad), F32)]
        + branch_scratch(max(NSA_KEY_TILE, WIN_TILES * Q_BLOCK)),
        compiler_params=_cparams("parallel", "parallel", "arbitrary"),
        name="nsa",
    )(qkv, cmp_kv, cmp_kv, qkv, qkv, qkv, qkv, tail, bias_c, bias_tab, ovt)


DECAY_STEP = 512
FOX_SPLIT = 1


def _decay_kernel(f_ref, fb_ref, place_ref, o_ref, carry_ref):
    r_i = lax.broadcasted_iota(jnp.int32, (Q_BLOCK, Q_BLOCK), 0)
    c_i = lax.broadcasted_iota(jnp.int32, (Q_BLOCK, Q_BLOCK), 1)
    tri = jnp.where(c_i <= r_i, 1.0, 0.0).astype(BF16)

    @pl.when(pl.program_id(1) == 0)
    def _():
        carry_ref[...] = jnp.zeros_like(carry_ref)

    carry = carry_ref[...]
    for r in range(DECAY_STEP // Q_BLOCK):
        x = f_ref[r * Q_BLOCK:(r + 1) * Q_BLOCK, :] + fb_ref[...]
        ls = jax.nn.log_sigmoid(x)
        cs = jnp.broadcast_to(carry, ls.shape)
        for piece in _split3(ls):
            cs = cs + jnp.dot(tri, piece, preferred_element_type=F32)
        feat = None
        for n, piece in enumerate(_split3(cs * LOG2E)):
            term = jnp.dot(piece, place_ref[n], preferred_element_type=F32)
            feat = term if feat is None else feat + term
        o_ref[r * Q_BLOCK:(r + 1) * Q_BLOCK, :] = feat.astype(o_ref.dtype)
        carry = cs[Q_BLOCK - 1:Q_BLOCK, :]
    carry_ref[...] = carry


def _decay_placement():
    place = np.zeros((3, PAIR, (C_HEADS // 2) * PAIR), np.float32)
    for h in range(C_HEADS):
        for n in range(3):
            place[n, h, (h // 2) * PAIR + 3 * (h % 2) + n] = -1.0
    return jnp.asarray(place, dtype=BF16)


def fox_decay(f_tail, f_bias, batch, seq):
    steps = seq // DECAY_STEP
    width = (C_HEADS // 2) * PAIR
    return pl.pallas_call(
        _decay_kernel,
        grid=(batch, steps),
        in_specs=[pl.BlockSpec((DECAY_STEP, PAIR), lambda b, s: (b * steps + s, 0)),
                  pl.BlockSpec((1, PAIR), lambda b, s: (0, 0)),
                  pl.BlockSpec((3, PAIR, width), lambda b, s: (0, 0, 0))],
        out_specs=pl.BlockSpec((DECAY_STEP, width), lambda b, s: (b * steps + s, 0)),
        out_shape=jax.ShapeDtypeStruct((batch * seq, width), BF16),
        scratch_shapes=[pltpu.VMEM((1, PAIR), F32)],
        compiler_params=_cparams("parallel", "arbitrary"),
        name="fox_decay",
    )(f_tail, f_bias, _decay_placement())


def _softmax_tile(lane_tiles, n_chunks, rc, n_lane_tiles, m_ref, l_ref, a_ref, pm_ref, p_ref, keep=None,
                  kept_tiles=None):
    second = lane_tiles if kept_tiles is None else kept_tiles
    for c in range(n_chunks):
        pm = None
        for u in range(n_lane_tiles):
            s = lane_tiles(c, u)
            if s is None:
                continue
            if keep is not None:
                keep(c, u, s)
            pm = s if pm is None else jnp.maximum(pm, s)
        pm_ref[c * rc:(c + 1) * rc, :] = pm
    m_old = m_ref[...]
    m_new = jnp.maximum(m_old, jnp.max(pm_ref[...], axis=-1, keepdims=True))
    a_ref[...] = jnp.exp2(m_old - m_new)
    m_ref[...] = m_new
    for c in range(n_chunks):
        rows = slice(c * rc, (c + 1) * rc)
        mb = m_ref[rows, :]
        psum = None
        for u in range(n_lane_tiles):
            s = second(c, u)
            if s is None:
                p_ref[rows, u * Q_BLOCK:(u + 1) * Q_BLOCK] = jnp.zeros((rc, Q_BLOCK), BF16)
                continue
            p = jnp.exp2(s - mb)
            p_ref[rows, u * Q_BLOCK:(u + 1) * Q_BLOCK] = p.astype(BF16)
            psum = p if psum is None else psum + p
        l_ref[rows, :] = a_ref[rows, :] * l_ref[rows, :] + psum


def _fox_kernel(q_ref, k_ref, v_ref, e_ref, o_ref, s_ref, p_ref, m_ref, l_ref, a_ref, pm_ref, acc_ref, *, tq):
    t = pl.program_id(2)
    kw = tq
    n_lane_tiles = kw // Q_BLOCK
    srows = tq // FOX_SPLIT
    n_chunks = srows // ROW_CHUNK
    n_streams = 2 * FOX_SPLIT
    q = q_ref[...]
    lane = lax.broadcasted_iota(jnp.int32, (tq, PAIR), 1)
    lo = lane < HEAD_DIM
    zero = jnp.zeros_like(q)
    q_aug = [jnp.concatenate([jnp.where(lo if hh == 0 else ~lo, q, zero),
                              jnp.where((lane >= 3 * hh) & (lane < 3 * hh + 3), 1.0, 0.0).astype(BF16)], axis=1)
             for hh in range(2)]
    qs = [q_aug[st // FOX_SPLIT][(st % FOX_SPLIT) * srows:(st % FOX_SPLIT + 1) * srows] for st in range(n_streams)]
    m_ref[...] = jnp.full(m_ref.shape, NEG_INF, F32)
    l_ref[...] = jnp.zeros(l_ref.shape, F32)
    acc_ref[...] = jnp.zeros(acc_ref.shape, F32)
    col_minus_row = (lax.broadcasted_iota(jnp.int32, (ROW_CHUNK, Q_BLOCK), 1)
                     - lax.broadcasted_iota(jnp.int32, (ROW_CHUNK, Q_BLOCK), 0))

    def scores(st, j):
        off = pl.multiple_of(j * kw, kw)
        k_aug = jnp.concatenate([k_ref[pl.ds(off, kw), :], e_ref[pl.ds(off, kw), :]], axis=1)
        s_ref[st] = _nt(qs[st], k_aug)

    def tile(j, diagonal):
        v = v_ref[pl.ds(pl.multiple_of(j * kw, kw), kw), :]
        for st in range(n_streams):
            row0 = (st % FOX_SPLIT) * srows

            def lane_tiles(c, u):
                first_row, first_key = row0 + c * ROW_CHUNK, u * Q_BLOCK
                if diagonal and first_key > first_row + ROW_CHUNK - 1:
                    return None
                s = s_ref[st, c * ROW_CHUNK:(c + 1) * ROW_CHUNK, u * Q_BLOCK:(u + 1) * Q_BLOCK]
                if diagonal and first_key + Q_BLOCK - 1 > first_row:
                    s = jnp.where(col_minus_row <= first_row - first_key, s, NEG_INF)
                return s

            _softmax_tile(lane_tiles, n_chunks, ROW_CHUNK, n_lane_tiles, m_ref.at[st], l_ref.at[st], a_ref.at[st],
                          pm_ref.at[st], p_ref.at[st])
            acc_ref[st] = a_ref[st] * acc_ref[st] + jnp.dot(p_ref[st], v, preferred_element_type=F32)
            if not diagonal:
                scores(st, j + 1)

    def body(j, carry):
        tile(j, False)
        return carry

    for st in range(n_streams):
        scores(st, 0)
    lax.fori_loop(0, t, body, 0)
    tile(t, True)
    o = [jnp.concatenate([acc_ref[st] / jnp.maximum(jnp.sum(l_ref[st], axis=-1, keepdims=True), 1e-30)
                          for st in range(hh * FOX_SPLIT, (hh + 1) * FOX_SPLIT)], axis=0) for hh in range(2)]
    o_ref[...] = jnp.where(lo, o[0], o[1]).astype(o_ref.dtype)


def fox_attention(qkv, decay, batch, seq, tq=512):
    m = batch * seq
    n_pairs = C_HEADS // 2
    nt = seq // tq
    return pl.pallas_call(
        functools.partial(_fox_kernel, tq=tq),
        grid=(batch, n_pairs, nt),
        in_specs=[pl.BlockSpec((tq, PAIR), lambda b, h, t: (b * nt + t, h)),
                  pl.BlockSpec((seq, PAIR), lambda b, h, t: (b, n_pairs + h)),
                  pl.BlockSpec((seq, PAIR), lambda b, h, t: (b, 2 * n_pairs + h)),
                  pl.BlockSpec((seq, PAIR), lambda b, h, t: (b, h))],
        out_specs=pl.BlockSpec((tq, PAIR), lambda b, h, t: (b * nt + t, h)),
        out_shape=jax.ShapeDtypeStruct((m, C_HEADS * HEAD_DIM), BF16),
        scratch_shapes=[pltpu.VMEM((2 * FOX_SPLIT, tq // FOX_SPLIT, tq), F32),
                        pltpu.VMEM((2 * FOX_SPLIT, tq // FOX_SPLIT, tq), BF16)]
        + [pltpu.VMEM((2 * FOX_SPLIT, tq // FOX_SPLIT, PAIR), F32)] * 5,
        compiler_params=_cparams("parallel", "parallel", "arbitrary"),
        name="fox",
    )(qkv, qkv, qkv, decay)


def _rel_bucket(dist):
    n = jnp.maximum(dist, 0)
    max_exact = REL_BUCKETS // 2
    nf = jnp.maximum(n, 1).astype(jnp.float32)
    large = max_exact + (jnp.log(nf / max_exact) / math.log(REL_MAX_DIST / max_exact)
                         * (REL_BUCKETS - max_exact)).astype(jnp.int32)
    return jnp.where(n < max_exact, n, jnp.minimum(large, REL_BUCKETS - 1))


def _lookup(table, idx):
    onehot = (idx[..., None] == jnp.arange(table.shape[0])).astype(F32)
    return jnp.einsum("hn,...n->h...", table.T, onehot, precision=lax.Precision.HIGHEST)


def _bias_tables(rel_bias, seq):
    nb = seq // Q_BLOCK
    n_cmp_pad = seq // NSA_CMP_STRIDE
    ql = jnp.arange(Q_BLOCK)
    d = jnp.arange(NEAR_TILES)
    dist_t = d[:, None, None] * Q_BLOCK + ql[None, :, None] - ql[None, None, :]
    tt = _lookup(rel_bias, _rel_bucket(dist_t))
    dist_a = ql[:, None] + Q_BLOCK - jnp.arange(2 * Q_BLOCK)[None, :]
    seen = (dist_a >= 0) & (dist_a < A_WINDOW)
    own = jnp.arange(2 * Q_BLOCK)[None, :] >= Q_BLOCK
    bias_a = jnp.concatenate([tt[:A_Q_HEADS, 1], tt[:A_Q_HEADS, 0]], axis=-1) * LOG2E
    bias_a = jnp.stack([jnp.where(seen, bias_a, NEG_INF), jnp.where(seen & own, bias_a, NEG_INF)])
    tb = tt[A_Q_HEADS:]
    upper = (ql[None, :] > ql[:, None])[None]
    far = jnp.broadcast_to(rel_bias[REL_BUCKETS - 1, A_Q_HEADS:][:, None, None], tb[:, 0].shape)
    entries = ([jnp.where(upper, NEG_INF, tb[:, 0])] + [tb[:, d] for d in range(1, NEAR_TILES)]
               + [far, jnp.where(upper, tb[:, WIN_TILES - 1], NEG_INF), jnp.full_like(far, NEG_INF)])
    bias_tab = jnp.stack(entries, axis=1).reshape(B_KV_HEADS, B_GROUP, TAB_ENTRIES, Q_BLOCK, Q_BLOCK)
    bias_tab = jnp.transpose(bias_tab, (0, 2, 1, 3, 4)) * LOG2E
    cend = jnp.arange(n_cmp_pad) * NSA_CMP_STRIDE + NSA_CMP_LEN - 1
    dist_c = (jnp.arange(nb)[:, None, None] * Q_BLOCK + ql[None, :, None]) - cend[None, None, :]
    bias_c = _lookup(rel_bias[:, A_Q_HEADS:] * LOG2E, _rel_bucket(dist_c))
    visible = (dist_c >= 0) & (jnp.arange(n_cmp_pad) < n_cmp_pad - 1)
    bias_c = jnp.where(visible[None], bias_c, NEG_INF)
    return bias_a, bias_tab, bias_c


def _overlap_t(seq):
    n_cmp_pad = seq // NSA_CMP_STRIDE
    n_sel = seq // NSA_SEL_LEN
    cstart = np.arange(n_cmp_pad) * NSA_CMP_STRIDE
    sstart = np.arange(n_sel) * NSA_SEL_LEN
    ov = (cstart[None, :] < sstart[:, None] + NSA_SEL_LEN) & (cstart[None, :] + NSA_CMP_LEN > sstart[:, None])
    ov[:, n_cmp_pad - 1] = False
    return jnp.asarray(ov.astype(np.float32), dtype=BF16)


def _dup(w, n_heads):
    d = w.shape[0]
    w = w.reshape(d, n_heads, 1, HEAD_DIM)
    return jnp.broadcast_to(w, (d, n_heads, 2, HEAD_DIM)).reshape(d, n_heads * PAIR)


def _even_weights(w_in):
    sizes = (A_Q_HEADS * HEAD_DIM, A_KV_HEADS * HEAD_DIM, A_KV_HEADS * HEAD_DIM, B_Q_HEADS * HEAD_DIM) \
        + (B_KV_HEADS * HEAD_DIM,) * 6 + (3 * B_Q_HEADS,)
    qa, ka, va, qb, kc, vc, ksl, vsl, kwn, vwn, gt = jnp.split(w_in, np.cumsum(sizes)[:-1].tolist(), axis=-1)
    log2_scale = ATTN_SCALE * LOG2E
    main = jnp.concatenate([qa * log2_scale, _dup(ka, A_KV_HEADS), _dup(va, A_KV_HEADS), qb * log2_scale,
                            _dup(ksl, B_KV_HEADS), _dup(vsl, B_KV_HEADS),
                            _dup(kwn, B_KV_HEADS), _dup(vwn, B_KV_HEADS)], axis=-1).astype(BF16)
    d = w_in.shape[0]
    per_group = 3 * B_GROUP
    gates = [jnp.pad(gt[:, g * per_group:(g + 1) * per_group], ((0, 0), (0, PAIR - per_group)))
             for g in range(B_KV_HEADS)]
    tail = jnp.concatenate([kc, vc] + gates, axis=-1).astype(BF16)
    return main, tail


def _odd_weights(w_in):
    c_mix = C_HEADS * HEAD_DIM
    col_scale = np.concatenate([np.full(c_mix, ATTN_SCALE * LOG2E, np.float32), np.ones(2 * c_mix, np.float32)])
    main = (w_in[:, :3 * c_mix] * col_scale).astype(BF16)
    tail = jnp.pad(w_in[:, 3 * c_mix:], ((0, 0), (0, PAIR - C_HEADS))).astype(BF16)
    return main, tail


def kernel(x, rel_bias, norm_mix, norm_ffn, norm_final, w_in_even, w_out_even, a_sinks, nsa_pe_k, nsa_pe_v,
           nsa_cmp_k_w1, nsa_cmp_k_w2, nsa_cmp_v_w1, nsa_cmp_v_w2, w_in_odd, w_out_odd, fox_fgate_b,
           w_ffn_up, w_ffn_down):
    batch, seq, d = x.shape
    depth = norm_mix.shape[0]
    m = batch * seq
    xf = x.reshape(m, d)
    bias_a, bias_tab, bias_c = _bias_tables(rel_bias, seq)
    ovt = _overlap_t(seq)
    n_cmp_pad = seq // NSA_CMP_STRIDE
    g_final = norm_final.reshape(1, d)
    w_up_bf16, w_down_bf16 = w_ffn_up.astype(BF16), w_ffn_down.astype(BF16)
    w_out_even_bf16, w_out_odd_bf16 = w_out_even.astype(BF16), w_out_odd.astype(BF16)

    for layer in range(depth):
        g_mix = norm_mix[layer].reshape(1, d)
        if layer % 2 == 0:
            e = layer // 2
            w_main, w_tail = _even_weights(w_in_even[e])
            qkv, tail = norm_matmul(xf, g_mix, w_main, w_tail)
            sinks = jnp.broadcast_to((a_sinks[e] * LOG2E).reshape(A_Q_HEADS, 1, 1), (A_Q_HEADS, 1, PAIR))
            a_out = swa_attention(qkv, bias_a, sinks, batch, seq)
            r = tail[:, :2 * B_KV_HEADS * HEAD_DIM].reshape(batch, seq, 2 * B_KV_HEADS, HEAD_DIM)
            r = jnp.transpose(r, (0, 2, 1, 3)).reshape(batch, 2 * B_KV_HEADS, n_cmp_pad, NSA_CMP_STRIDE * HEAD_DIM)
            pe = jnp.stack([nsa_pe_k[e].reshape(1, -1), nsa_pe_v[e].reshape(1, -1)])
            w1 = jnp.stack([nsa_cmp_k_w1[e], nsa_cmp_v_w1[e]]).astype(BF16)
            w2 = jnp.stack([_dup(nsa_cmp_k_w2[e], 1), _dup(nsa_cmp_v_w2[e], 1)]).astype(BF16)
            cmp_kv = nsa_compress(r, pe, w1, w2)
            b_out = nsa_attention(qkv, cmp_kv, tail, bias_c, bias_tab, ovt, batch, seq)
            mixed = (a_out, 0, b_out, 0, w_out_even_bf16, e)
        else:
            o = layer // 2
            w_main, w_tail = _odd_weights(w_in_odd[o])
            qkv, f_tail = norm_matmul(xf, g_mix, w_main, w_tail)
            f_bias = jnp.pad(fox_fgate_b[o], (0, PAIR - C_HEADS)).reshape(1, PAIR)
            decay = fox_decay(f_tail, f_bias, batch, seq)
            c_out = fox_attention(qkv, decay, batch, seq)
            mixed = (c_out, 0, c_out, 1, w_out_odd_bf16, o)
        xf = mix_out_ffn(xf, *mixed, norm_ffn[layer].reshape(1, d), w_up_bf16, w_down_bf16, layer,
                         g_final, layer == depth - 1)
    return xf.reshape(batch, seq, d)
```

```python
import functools
import math

import jax
import jax.numpy as jnp
import numpy as np
from jax import lax
from jax.experimental import pallas as pl
from jax.experimental.pallas import tpu as pltpu

F32 = jnp.float32
BF16 = jnp.bfloat16

HEAD_DIM = 64
PAIR = 2 * HEAD_DIM
A_Q_HEADS = 16
A_KV_HEADS = 4
A_WINDOW = 128
B_Q_HEADS = 16
B_KV_HEADS = 2
B_GROUP = B_Q_HEADS // B_KV_HEADS
A_GROUP = A_Q_HEADS // A_KV_HEADS
NSA_CMP_LEN = 32
NSA_CMP_STRIDE = 16
NSA_CMP_HIDDEN = 4 * HEAD_DIM
NSA_SEL_LEN = 64
NSA_TOP_N = 16
NSA_WINDOW = 512
NSA_FORCE_SCORE = 1.0e4
C_HEADS = 32
REL_BUCKETS = 32
REL_MAX_DIST = 1024
Q_BLOCK = 128
RMS_EPS = 1e-6
ATTN_SCALE = HEAD_DIM ** -0.5
LOG2E = 1.4426950408889634
NEAR_TILES = 8
NEG_INF = float("-inf")
ROW_CHUNK = 32

VMEM_LIMIT_BYTES = 56 * 1024 * 1024


def _cparams(*sem):
    return pltpu.CompilerParams(dimension_semantics=sem, vmem_limit_bytes=VMEM_LIMIT_BYTES)


def _nt(a, b):
    return lax.dot_general(a, b, (((1,), (1,)), ((), ())), preferred_element_type=F32)


def _split3(x):
    hi = x.astype(BF16)
    r1 = x - hi.astype(F32)
    mid = r1.astype(BF16)
    lo = (r1 - mid.astype(F32)).astype(BF16)
    return hi, mid, lo


def _rms(x, g):
    ms = jnp.mean(x * x, axis=-1, keepdims=True)
    return x * lax.rsqrt(ms + RMS_EPS) * g


def _stack_pairs(q_ref, first_pair, n_pairs):
    lane = lax.broadcasted_iota(jnp.int32, (Q_BLOCK, PAIR), 1)
    lo = lane < HEAD_DIM
    parts = []
    for p in range(n_pairs):
        qp = q_ref[:, (first_pair + p) * PAIR:(first_pair + p + 1) * PAIR]
        parts.append(jnp.where(lo, qp, jnp.zeros_like(qp)))
        parts.append(jnp.where(lo, jnp.zeros_like(qp), qp))
    return jnp.concatenate(parts, axis=0)


def _unstack_pairs(o, n_pairs, rows):
    lane = lax.broadcasted_iota(jnp.int32, (rows, PAIR), 1)
    lo = lane < HEAD_DIM
    return [jnp.where(lo, o[(2 * p) * rows:(2 * p + 1) * rows], o[(2 * p + 1) * rows:(2 * p + 2) * rows])
            for p in range(n_pairs)]


def _norm_mm_kernel(x_ref, g_ref, w_ref, wt_ref, o_ref, ot_ref, h_ref):
    @pl.when(pl.program_id(1) == 0)
    def _():
        h_ref[...] = _rms(x_ref[...], g_ref[...]).astype(BF16)
        ot_ref[...] = jnp.dot(h_ref[...], wt_ref[...], preferred_element_type=F32)

    o_ref[...] = jnp.dot(h_ref[...], w_ref[...], preferred_element_type=F32).astype(o_ref.dtype)


def norm_matmul(x, g, w, w_tail, idx, tm=1024, tn=1024):
    m, d = x.shape
    n = w.shape[2]
    nt = w_tail.shape[2]
    return pl.pallas_call(
        _norm_mm_kernel,
        grid=(m // tm, n // tn),
        in_specs=[pl.BlockSpec((tm, d), lambda i, j: (i, 0)),
                  pl.BlockSpec((1, d), lambda i, j: (0, 0)),
                  pl.BlockSpec((None, d, tn), lambda i, j: (idx, 0, j)),
                  pl.BlockSpec((None, d, nt), lambda i, j: (idx, 0, 0))],
        out_specs=[pl.BlockSpec((tm, tn), lambda i, j: (i, j)),
                   pl.BlockSpec((tm, nt), lambda i, j: (i, 0))],
        out_shape=[jax.ShapeDtypeStruct((m, n), BF16), jax.ShapeDtypeStruct((m, nt), F32)],
        scratch_shapes=[pltpu.VMEM((tm, d), BF16)],
        compiler_params=_cparams("parallel", "arbitrary"),
        name="norm_matmul",
    )(x, g, w, w_tail)


def _mix_ffn_kernel(x_ref, a1_ref, a2_ref, wo1_ref, wo2_ref, g_ref, wu_ref, wd_ref, gf_ref, o_ref, h_ref, acc_ref,
                    *, final_norm):
    k = pl.program_id(1)

    @pl.when(k == 0)
    def _():
        y = jnp.dot(a1_ref[...], wo1_ref[...], preferred_element_type=F32)
        y = y + jnp.dot(a2_ref[...], wo2_ref[...], preferred_element_type=F32)
        x1 = x_ref[...] + y
        acc_ref[...] = x1
        h_ref[...] = _rms(x1, g_ref[...]).astype(BF16)

    u = jnp.dot(h_ref[...], wu_ref[...], preferred_element_type=F32)
    u = jnp.maximum(u, 0.0)
    acc_ref[...] += jnp.dot((u * u).astype(BF16), wd_ref[...], preferred_element_type=F32)

    @pl.when(k == pl.num_programs(1) - 1)
    def _():
        y = acc_ref[...]
        if final_norm:
            y = _rms(y, gf_ref[...])
        o_ref[...] = y


def mix_out_ffn(x, a1, a1_blk, a2, a2_blk, w_out, w_out_idx, g, w_up, w_down, layer, g_final, final_norm,
                tm=512, tf=1024):
    m, d = x.shape
    ff = w_up.shape[2]
    half = d // 2
    return pl.pallas_call(
        functools.partial(_mix_ffn_kernel, final_norm=final_norm),
        grid=(m // tm, ff // tf),
        in_specs=[pl.BlockSpec((tm, d), lambda i, k: (i, 0)),
                  pl.BlockSpec((tm, half), lambda i, k: (i, a1_blk)),
                  pl.BlockSpec((tm, half), lambda i, k: (i, a2_blk)),
                  pl.BlockSpec((None, half, d), lambda i, k: (w_out_idx, 0, 0), pipeline_mode=pl.Buffered(1)),
                  pl.BlockSpec((None, half, d), lambda i, k: (w_out_idx, 1, 0), pipeline_mode=pl.Buffered(1)),
                  pl.BlockSpec((1, d), lambda i, k: (0, 0)),
                  pl.BlockSpec((None, d, tf), lambda i, k: (layer, 0, k)),
                  pl.BlockSpec((None, tf, d), lambda i, k: (layer, k, 0)),
                  pl.BlockSpec((1, d), lambda i, k: (0, 0))],
        out_specs=pl.BlockSpec((tm, d), lambda i, k: (i, 0)),
        out_shape=jax.ShapeDtypeStruct((m, d), F32),
        scratch_shapes=[pltpu.VMEM((tm, d), BF16), pltpu.VMEM((tm, d), F32)],
        compiler_params=_cparams("parallel", "arbitrary"),
        name="mix_ffn",
    )(x, a1, a2, w_out, w_out, g, w_up, w_down, g_final)


def _swa_kernel(q_ref, kp_ref, kc_ref, vp_ref, vc_ref, bias_ref, sink_ref, o_ref,
                s_ref, p_ref, m_ref, l_ref, a_ref, pm_ref):
    first_block = jnp.where(pl.program_id(1) == 0, 1, 0)
    rows = A_GROUP * Q_BLOCK
    chunks_per_head = Q_BLOCK // ROW_CHUNK
    lane = lax.broadcasted_iota(jnp.int32, (Q_BLOCK, PAIR), 1)
    for g in range(A_KV_HEADS):
        qs = _stack_pairs(q_ref, g * (A_GROUP // 2), A_GROUP // 2)
        k = jnp.concatenate([kp_ref[:, g * PAIR:(g + 1) * PAIR], kc_ref[:, g * PAIR:(g + 1) * PAIR]], axis=0)
        v = jnp.concatenate([vp_ref[:, g * PAIR:(g + 1) * PAIR], vc_ref[:, g * PAIR:(g + 1) * PAIR]], axis=0)
        s_ref[g] = _nt(qs, k)
        m_ref[g] = jnp.concatenate([jnp.broadcast_to(sink_ref[g * A_GROUP + r], (Q_BLOCK, PAIR))
                                    for r in range(A_GROUP)], axis=0)
        l_ref[g] = jnp.concatenate([jnp.where(lane == 0, 1.0, 0.0)] * A_GROUP, axis=0)

        def lane_tiles(c, u):
            head = g * A_GROUP + c // chunks_per_head
            q0 = (c % chunks_per_head) * ROW_CHUNK
            return (s_ref[g, c * ROW_CHUNK:(c + 1) * ROW_CHUNK, u * Q_BLOCK:(u + 1) * Q_BLOCK]
                    + bias_ref[first_block, head, q0:q0 + ROW_CHUNK, u * Q_BLOCK:(u + 1) * Q_BLOCK])

        _softmax_tile(lane_tiles, rows // ROW_CHUNK, ROW_CHUNK, 2, m_ref.at[g], l_ref.at[g], a_ref.at[g],
                      pm_ref.at[g], p_ref.at[g])
        o = jnp.dot(p_ref[g], v, preferred_element_type=F32)
        o = o / jnp.sum(l_ref[g], axis=-1, keepdims=True)
        for p, blk in enumerate(_unstack_pairs(o, A_GROUP // 2, Q_BLOCK)):
            c0 = (g * (A_GROUP // 2) + p) * PAIR
            o_ref[:, c0:c0 + PAIR] = blk.astype(o_ref.dtype)


def swa_attention(qkv, bias_a, sinks, batch, seq):
    nb = seq // Q_BLOCK
    m = batch * seq
    qa_w = A_Q_HEADS * HEAD_DIM
    kv_w = A_KV_HEADS * PAIR
    k_blk = qa_w // kv_w
    v_blk = k_blk + 1
    row = lambda b, i: b * nb + i
    prev = lambda b, i: b * nb + jnp.maximum(i - 1, 0)
    return pl.pallas_call(
        _swa_kernel,
        grid=(batch, nb),
        in_specs=[pl.BlockSpec((Q_BLOCK, qa_w), lambda b, i: (row(b, i), 0)),
                  pl.BlockSpec((Q_BLOCK, kv_w), lambda b, i: (prev(b, i), k_blk)),
                  pl.BlockSpec((Q_BLOCK, kv_w), lambda b, i: (row(b, i), k_blk)),
                  pl.BlockSpec((Q_BLOCK, kv_w), lambda b, i: (prev(b, i), v_blk)),
                  pl.BlockSpec((Q_BLOCK, kv_w), lambda b, i: (row(b, i), v_blk)),
                  pl.BlockSpec((2, A_Q_HEADS, Q_BLOCK, 2 * Q_BLOCK), lambda b, i: (0, 0, 0, 0)),
                  pl.BlockSpec((A_Q_HEADS, 1, PAIR), lambda b, i: (0, 0, 0))],
        out_specs=pl.BlockSpec((Q_BLOCK, qa_w), lambda b, i: (row(b, i), 0)),
        out_shape=jax.ShapeDtypeStruct((m, qa_w), BF16),
        scratch_shapes=[pltpu.VMEM((A_KV_HEADS, A_GROUP * Q_BLOCK, 2 * Q_BLOCK), F32),
                        pltpu.VMEM((A_KV_HEADS, A_GROUP * Q_BLOCK, 2 * Q_BLOCK), BF16)]
        + [pltpu.VMEM((A_KV_HEADS, A_GROUP * Q_BLOCK, PAIR), F32)] * 4,
        compiler_params=_cparams("parallel", "parallel"),
        name="swa",
    )(qkv, qkv, qkv, qkv, qkv, bias_a, sinks)


def _compress_kernel(r_ref, pe_ref, w1_ref, w2_ref, o_ref):
    half = NSA_CMP_STRIDE * HEAD_DIM
    r = r_ref[0, 0]
    xa = (r + pe_ref[0, :, :half]).astype(BF16)
    xb = (r + pe_ref[0, :, half:]).astype(BF16)
    a = jnp.dot(xa, w1_ref[0, :half, :], preferred_element_type=F32)
    b = jnp.dot(xb, w1_ref[0, half:, :], preferred_element_type=F32)
    n = r.shape[0]
    hid = jax.nn.gelu(a + pltpu.roll(b, n - 1, 0))
    o_ref[0, 0] = jnp.dot(hid.astype(BF16), w2_ref[0], preferred_element_type=F32).astype(o_ref.dtype)


def nsa_compress(r, pe, w1, w2dup):
    batch, _, n, width = r.shape
    return pl.pallas_call(
        _compress_kernel,
        grid=(batch, 2 * B_KV_HEADS),
        in_specs=[pl.BlockSpec((1, 1, n, width), lambda b, w: (b, w, 0, 0)),
                  pl.BlockSpec((1, 1, 2 * width), lambda b, w: (w // B_KV_HEADS, 0, 0)),
                  pl.BlockSpec((1, 2 * width, NSA_CMP_HIDDEN), lambda b, w: (w // B_KV_HEADS, 0, 0)),
                  pl.BlockSpec((1, NSA_CMP_HIDDEN, PAIR), lambda b, w: (w // B_KV_HEADS, 0, 0))],
        out_specs=pl.BlockSpec((1, 1, n, PAIR), lambda b, w: (b, w, 0, 0)),
        out_shape=jax.ShapeDtypeStruct((batch, 2 * B_KV_HEADS, n, PAIR), BF16),
        compiler_params=_cparams("parallel", "parallel"),
        name="nsa_compress",
    )(r, pe, w1, w2dup)


NSA_KEY_TILE = 512
NSA_STREAMS = 2
WIN_TILES = NSA_WINDOW // Q_BLOCK + 1
TAB_FAR = NEAR_TILES
TAB_WIN_EDGE = NEAR_TILES + 1
TAB_NONE = NEAR_TILES + 2
TAB_ENTRIES = NEAR_TILES + 3
UNSELECTED = -1.0e30


def _nsa_kernel(q_ref, kcm_ref, vcm_ref, ks_ref, vs_ref, kw_ref, vw_ref, gate_ref, bias_c_ref, tab_ref, ovt_ref,
                o_ref, val_ref, psum_ref, s_ref, p_ref, m_ref, l_ref, a_ref, pm_ref, acc_ref, *, n_cmp_pad):
    i = pl.program_id(2)
    rows = B_GROUP * Q_BLOCK
    srows = rows // NSA_STREAMS
    heads_per_stream = B_GROUP // NSA_STREAMS
    chunks_per_head = Q_BLOCK // ROW_CHUNK
    n_sel = val_ref.shape[0]
    sel_shift = int(math.log2(NSA_SEL_LEN))
    qs = _stack_pairs(q_ref, 0, B_GROUP // 2)

    qs_st = [qs[st * srows:(st + 1) * srows] for st in range(NSA_STREAMS)]

    c_tiles = n_cmp_pad // Q_BLOCK
    psum_ref[...] = jnp.zeros(psum_ref.shape, F32)
    o_c = []
    for st in range(NSA_STREAMS):
        s_ref[st, :, :n_cmp_pad] = _nt(qs_st[st], kcm_ref[0, 0])

        def cmp_block(c, u):
            return (slice(c * ROW_CHUNK, (c + 1) * ROW_CHUNK), slice(u * Q_BLOCK, (u + 1) * Q_BLOCK),
                    st * heads_per_stream + c // chunks_per_head, (c % chunks_per_head) * ROW_CHUNK)

        for c in range(srows // ROW_CHUNK):
            pm = None
            for u in range(c_tiles):
                rws, cols, head, q0 = cmp_block(c, u)
                t = s_ref[st, rws, cols] + bias_c_ref[head, q0:q0 + ROW_CHUNK, cols]
                s_ref[st, rws, cols] = t
                pm = t if pm is None else jnp.maximum(pm, t)
            pm_ref[st, c * ROW_CHUNK:(c + 1) * ROW_CHUNK, :] = pm
        m_c = jnp.max(pm_ref[st], axis=-1, keepdims=True)
        m_ref[st] = jnp.broadcast_to(jnp.where(m_c == NEG_INF, 0.0, m_c), m_ref.shape[1:])
        for c in range(srows // ROW_CHUNK):
            mb = m_ref[st, c * ROW_CHUNK:(c + 1) * ROW_CHUNK, :]
            esum = None
            for u in range(c_tiles):
                rws, cols, _, _ = cmp_block(c, u)
                e = jnp.exp2(s_ref[st, rws, cols] - mb)
                s_ref[st, rws, cols] = e
                esum = e if esum is None else esum + e
            l_ref[st, c * ROW_CHUNK:(c + 1) * ROW_CHUNK, :] = esum
        denom = jnp.maximum(jnp.sum(l_ref[st], axis=-1, keepdims=True), 1e-30)
        a_ref[st] = jnp.broadcast_to(1.0 / denom, a_ref.shape[1:])
        for c in range(srows // ROW_CHUNK):
            inv = a_ref[st, c * ROW_CHUNK:(c + 1) * ROW_CHUNK, :]
            for u in range(c_tiles):
                rws, cols, _, q0 = cmp_block(c, u)
                p = s_ref[st, rws, cols] * inv
                p_ref[st, rws, cols] = p.astype(BF16)
                psum_ref[q0:q0 + ROW_CHUNK, cols] += p
        o_c.append(jnp.dot(p_ref[st, :, :n_cmp_pad], vcm_ref[0, 0], preferred_element_type=F32))
    o_c = jnp.concatenate(o_c, axis=0)

    p_sum = psum_ref[...]
    ovt = ovt_ref[...]
    imp = None
    for piece in _split3(p_sum):
        t = _nt(ovt, piece)
        imp = t if imp is None else imp + t
    blk = lax.broadcasted_iota(jnp.int32, (n_sel, Q_BLOCK), 0)
    qpos = lax.broadcasted_iota(jnp.int32, (n_sel, Q_BLOCK), 1) + i * Q_BLOCK
    cur = lax.shift_right_logical(qpos, sel_shift)
    forced = (blk == 0) | (blk == cur) | (blk == cur - 1)
    future = blk * NSA_SEL_LEN > qpos
    val = jnp.where(future, NEG_INF, jnp.where(forced, NSA_FORCE_SCORE, imp))
    val_ref[...] = val
    group = 8
    ranks = []
    for g0 in range(0, n_sel, group):
        val_g = val[g0:g0 + group]
        blk_g = blk[g0:g0 + group]
        rank_g = jnp.zeros((group, Q_BLOCK), F32)
        for s2 in range(n_sel):
            other = val_ref[s2:s2 + 1, :]
            if s2 < g0:
                ahead = other >= val_g
            elif s2 >= g0 + group:
                ahead = other > val_g
            else:
                ahead = ((blk_g > s2) & (other >= val_g)) | (other > val_g)
            rank_g = rank_g + jnp.where(ahead, 1.0, 0.0)
        ranks.append(rank_g)
    rank = jnp.concatenate(ranks, axis=0)
    sel_t = jnp.where((rank < float(NSA_TOP_N)) & (val > NEG_INF), 1.0, 0.0).astype(BF16)
    if n_sel < Q_BLOCK:
        sel_t = jnp.concatenate([sel_t, jnp.zeros((Q_BLOCK - n_sel, Q_BLOCK), BF16)], axis=0)
    eye = jnp.where(lax.broadcasted_iota(jnp.int32, (Q_BLOCK, Q_BLOCK), 0)
                    == lax.broadcasted_iota(jnp.int32, (Q_BLOCK, Q_BLOCK), 1), 1.0, 0.0).astype(BF16)
    unsel = jnp.where(_nt(eye, sel_t) > 0.5, 0.0, UNSELECTED).astype(BF16)
    unsel_rows = jnp.concatenate([unsel] * heads_per_stream, axis=0)
    qs_sel = [jnp.concatenate([qs_st[st], unsel_rows], axis=1) for st in range(NSA_STREAMS)]

    key_lane = lax.broadcasted_iota(jnp.int32, (NSA_KEY_TILE, PAIR), 1)
    key_blk = lax.shift_right_logical(lax.broadcasted_iota(jnp.int32, (NSA_KEY_TILE, PAIR), 0), sel_shift)
    lane_minus_blk = key_lane - key_blk

    sel_refs = (s_ref, p_ref, m_ref, l_ref, a_ref, pm_ref, acc_ref)

    def reset(refs):
        _, _, m_r, l_r, _, _, acc_r = refs
        m_r[...] = jnp.full(m_r.shape, NEG_INF, F32)
        l_r[...] = jnp.zeros(l_r.shape, F32)
        acc_r[...] = jnp.zeros(acc_r.shape, F32)

    def soft_pv(refs, st, tab_idx, v):
        s_r, p_r, m_r, l_r, a_r, pm_r, acc_r = refs
        width = len(tab_idx) * Q_BLOCK

        def raw(c, u):
            return s_r[st, c * ROW_CHUNK:(c + 1) * ROW_CHUNK, u * Q_BLOCK:(u + 1) * Q_BLOCK]

        def biased(c, u):
            head = st * heads_per_stream + c // chunks_per_head
            q0 = (c % chunks_per_head) * ROW_CHUNK
            return raw(c, u) + tab_ref[0, tab_idx[u], head, q0:q0 + ROW_CHUNK, :]

        def keep(c, u, s):
            s_r[st, c * ROW_CHUNK:(c + 1) * ROW_CHUNK, u * Q_BLOCK:(u + 1) * Q_BLOCK] = s

        _softmax_tile(biased, srows // ROW_CHUNK, ROW_CHUNK, len(tab_idx), m_r.at[st], l_r.at[st],
                      a_r.at[st], pm_r.at[st], p_r.at[st], keep=keep, kept_tiles=raw)
        acc_r[st] = a_r[st] * acc_r[st] + jnp.dot(p_r[st, :, :width], v, preferred_element_type=F32)

    def finish(refs):
        _, _, _, l_r, _, _, acc_r = refs
        return jnp.concatenate(
            [acc_r[st] / jnp.maximum(jnp.sum(l_r[st], axis=-1, keepdims=True), 1e-30)
             for st in range(NSA_STREAMS)], axis=0)

    blocks_per_tile = NSA_KEY_TILE // NSA_SEL_LEN
    lane_tiles_per_tile = NSA_KEY_TILE // Q_BLOCK

    def sel_scores(st, jt):
        off = pl.multiple_of(jt * NSA_KEY_TILE, NSA_KEY_TILE)
        block_onehot = jnp.where(lane_minus_blk == jt * blocks_per_tile, 1.0, 0.0).astype(BF16)
        k_aug = jnp.concatenate([ks_ref[pl.ds(off, NSA_KEY_TILE), :], block_onehot], axis=1)
        s_ref[st, :, :NSA_KEY_TILE] = _nt(qs_sel[st], k_aug)

    def sel_tile(jt, last):
        v = vs_ref[pl.ds(pl.multiple_of(jt * NSA_KEY_TILE, NSA_KEY_TILE), NSA_KEY_TILE), :]
        tab_idx = []
        for u in range(lane_tiles_per_tile):
            d = i - (jt * lane_tiles_per_tile + u)
            tab_idx.append(jnp.where(d < 0, TAB_NONE, jnp.minimum(d, TAB_FAR)))
        for st in range(NSA_STREAMS):
            soft_pv(sel_refs, st, tab_idx, v)
            if not last:
                sel_scores(st, jt + 1)

    def sel_body(jt, carry):
        sel_tile(jt, False)
        return carry

    n_tiles = lax.shift_right_logical(i + lane_tiles_per_tile, int(math.log2(lane_tiles_per_tile)))
    reset(sel_refs)
    for st in range(NSA_STREAMS):
        sel_scores(st, 0)
    lax.fori_loop(0, n_tiles - 1, sel_body, 0)
    sel_tile(n_tiles - 1, True)
    o_s = finish(sel_refs)

    first_blk = jnp.maximum(i - (WIN_TILES - 1), 0)
    off = pl.multiple_of(first_blk * Q_BLOCK, Q_BLOCK)
    k_win = kw_ref[pl.ds(off, WIN_TILES * Q_BLOCK), :]
    v_win = vw_ref[pl.ds(off, WIN_TILES * Q_BLOCK), :]
    win_idx = []
    for u in range(WIN_TILES):
        d = i - (first_blk + u)
        win_idx.append(jnp.where(d < 0, TAB_NONE, jnp.where(d == WIN_TILES - 1, TAB_WIN_EDGE, d)))
    reset(sel_refs)
    for st in range(NSA_STREAMS):
        s_ref[st] = _nt(qs_st[st], k_win)
    for st in range(NSA_STREAMS):
        soft_pv(sel_refs, st, win_idx, v_win)
    o_w = finish(sel_refs)

    gates = jax.nn.sigmoid(gate_ref[...])

    def gate_col(br):
        return jnp.concatenate([gates[:, 3 * r + br:3 * r + br + 1] for r in range(B_GROUP)], axis=0)

    o = gate_col(0) * o_c + gate_col(1) * o_s + gate_col(2) * o_w
    for p, blk_out in enumerate(_unstack_pairs(o, B_GROUP // 2, Q_BLOCK)):
        o_ref[:, p * PAIR:(p + 1) * PAIR] = blk_out.astype(o_ref.dtype)


def nsa_attention(qkv, cmp_kv, tail, bias_c, bias_tab, ovt, batch, seq):
    nb = seq // Q_BLOCK
    m = batch * seq
    n_cmp_pad = seq // NSA_CMP_STRIDE
    n_sel = seq // NSA_SEL_LEN
    assert n_sel <= Q_BLOCK and seq % NSA_KEY_TILE == 0 and seq >= WIN_TILES * Q_BLOCK
    assert n_cmp_pad % Q_BLOCK == 0 and n_cmp_pad <= NSA_KEY_TILE
    grp_w = B_GROUP * HEAD_DIM
    q_blk0 = (A_Q_HEADS * HEAD_DIM + 2 * A_KV_HEADS * PAIR) // grp_w
    kv_blk0 = (A_Q_HEADS * HEAD_DIM + 2 * A_KV_HEADS * PAIR + B_Q_HEADS * HEAD_DIM) // PAIR
    kv_spec = lambda t: pl.BlockSpec((seq, PAIR), lambda b, g, i: (b, kv_blk0 + t * B_KV_HEADS + g))
    srows = B_GROUP * Q_BLOCK // NSA_STREAMS

    def branch_scratch(width):
        return ([pltpu.VMEM((NSA_STREAMS, srows, width), F32), pltpu.VMEM((NSA_STREAMS, srows, width), BF16)]
                + [pltpu.VMEM((NSA_STREAMS, srows, PAIR), F32)] * 5)

    return pl.pallas_call(
        functools.partial(_nsa_kernel, n_cmp_pad=n_cmp_pad),
        grid=(batch, B_KV_HEADS, nb),
        in_specs=[pl.BlockSpec((Q_BLOCK, grp_w), lambda b, g, i: (b * nb + i, q_blk0 + g)),
                  pl.BlockSpec((1, 1, n_cmp_pad, PAIR), lambda b, g, i: (b, g, 0, 0)),
                  pl.BlockSpec((1, 1, n_cmp_pad, PAIR), lambda b, g, i: (b, B_KV_HEADS + g, 0, 0)),
                  kv_spec(0), kv_spec(1), kv_spec(2), kv_spec(3),
                  pl.BlockSpec((Q_BLOCK, PAIR), lambda b, g, i: (b * nb + i, 2 + g)),
                  pl.BlockSpec((B_GROUP, None, Q_BLOCK, n_cmp_pad), lambda b, g, i: (g, i, 0, 0)),
                  pl.BlockSpec((1, TAB_ENTRIES, B_GROUP, Q_BLOCK, Q_BLOCK), lambda b, g, i: (g, 0, 0, 0, 0)),
                  pl.BlockSpec((n_sel, n_cmp_pad), lambda b, g, i: (0, 0))],
        out_specs=pl.BlockSpec((Q_BLOCK, grp_w), lambda b, g, i: (b * nb + i, g)),
        out_shape=jax.ShapeDtypeStruct((m, B_Q_HEADS * HEAD_DIM), BF16),
        scratch_shapes=[pltpu.VMEM((n_sel, Q_BLOCK), F32), pltpu.VMEM((Q_BLOCK, n_cmp_pad), F32)]
        + branch_scratch(max(NSA_KEY_TILE, WIN_TILES * Q_BLOCK)),
        compiler_params=_cparams("parallel", "parallel", "arbitrary"),
        name="nsa",
    )(qkv, cmp_kv, cmp_kv, qkv, qkv, qkv, qkv, tail, bias_c, bias_tab, ovt)


DECAY_STEP = 512
FOX_SPLIT = 1


def _decay_kernel(f_ref, fb_ref, place_ref, o_ref, carry_ref):
    r_i = lax.broadcasted_iota(jnp.int32, (Q_BLOCK, Q_BLOCK), 0)
    c_i = lax.broadcasted_iota(jnp.int32, (Q_BLOCK, Q_BLOCK), 1)
    tri = jnp.where(c_i <= r_i, 1.0, 0.0).astype(BF16)

    @pl.when(pl.program_id(1) == 0)
    def _():
        carry_ref[...] = jnp.zeros_like(carry_ref)

    carry = carry_ref[...]
    for r in range(DECAY_STEP // Q_BLOCK):
        x = f_ref[r * Q_BLOCK:(r + 1) * Q_BLOCK, :] + fb_ref[...]
        ls = jax.nn.log_sigmoid(x)
        cs = jnp.broadcast_to(carry, ls.shape)
        for piece in _split3(ls):
            cs = cs + jnp.dot(tri, piece, preferred_element_type=F32)
        feat = None
        for n, piece in enumerate(_split3(cs * LOG2E)):
            term = jnp.dot(piece, place_ref[n], preferred_element_type=F32)
            feat = term if feat is None else feat + term
        o_ref[r * Q_BLOCK:(r + 1) * Q_BLOCK, :] = feat.astype(o_ref.dtype)
        carry = cs[Q_BLOCK - 1:Q_BLOCK, :]
    carry_ref[...] = carry


def _decay_placement():
    place = np.zeros((3, PAIR, (C_HEADS // 2) * PAIR), np.float32)
    for h in range(C_HEADS):
        for n in range(3):
            place[n, h, (h // 2) * PAIR + 3 * (h % 2) + n] = -1.0
    return jnp.asarray(place, dtype=BF16)


def fox_decay(f_tail, f_bias, batch, seq):
    steps = seq // DECAY_STEP
    width = (C_HEADS // 2) * PAIR
    return pl.pallas_call(
        _decay_kernel,
        grid=(batch, steps),
        in_specs=[pl.BlockSpec((DECAY_STEP, PAIR), lambda b, s: (b * steps + s, 0)),
                  pl.BlockSpec((1, PAIR), lambda b, s: (0, 0)),
                  pl.BlockSpec((3, PAIR, width), lambda b, s: (0, 0, 0))],
        out_specs=pl.BlockSpec((DECAY_STEP, width), lambda b, s: (b * steps + s, 0)),
        out_shape=jax.ShapeDtypeStruct((batch * seq, width), BF16),
        scratch_shapes=[pltpu.VMEM((1, PAIR), F32)],
        compiler_params=_cparams("parallel", "arbitrary"),
        name="fox_decay",
    )(f_tail, f_bias, _decay_placement())


def _softmax_tile(lane_tiles, n_chunks, rc, n_lane_tiles, m_ref, l_ref, a_ref, pm_ref, p_ref, keep=None,
                  kept_tiles=None):
    second = lane_tiles if kept_tiles is None else kept_tiles
    for c in range(n_chunks):
        pm = None
        for u in range(n_lane_tiles):
            s = lane_tiles(c, u)
            if s is None:
                continue
            if keep is not None:
                keep(c, u, s)
            pm = s if pm is None else jnp.maximum(pm, s)
        pm_ref[c * rc:(c + 1) * rc, :] = pm
    m_old = m_ref[...]
    m_new = jnp.maximum(m_old, jnp.max(pm_ref[...], axis=-1, keepdims=True))
    a_ref[...] = jnp.exp2(m_old - m_new)
    m_ref[...] = m_new
    for c in range(n_chunks):
        rows = slice(c * rc, (c + 1) * rc)
        mb = m_ref[rows, :]
        psum = None
        for u in range(n_lane_tiles):
            s = second(c, u)
            if s is None:
                p_ref[rows, u * Q_BLOCK:(u + 1) * Q_BLOCK] = jnp.zeros((rc, Q_BLOCK), BF16)
                continue
            p = jnp.exp2(s - mb)
            p_ref[rows, u * Q_BLOCK:(u + 1) * Q_BLOCK] = p.astype(BF16)
            psum = p if psum is None else psum + p
        l_ref[rows, :] = a_ref[rows, :] * l_ref[rows, :] + psum


def _fox_kernel(q_ref, k_ref, v_ref, e_ref, o_ref, s_ref, p_ref, m_ref, l_ref, a_ref, pm_ref, acc_ref, *, tq):
    t = pl.program_id(2)
    kw = tq
    n_lane_tiles = kw // Q_BLOCK
    srows = tq // FOX_SPLIT
    n_chunks = srows // ROW_CHUNK
    n_streams = 2 * FOX_SPLIT
    q = q_ref[...]
    lane = lax.broadcasted_iota(jnp.int32, (tq, PAIR), 1)
    lo = lane < HEAD_DIM
    zero = jnp.zeros_like(q)
    q_aug = [jnp.concatenate([jnp.where(lo if hh == 0 else ~lo, q, zero),
                              jnp.where((lane >= 3 * hh) & (lane < 3 * hh + 3), 1.0, 0.0).astype(BF16)], axis=1)
             for hh in range(2)]
    qs = [q_aug[st // FOX_SPLIT][(st % FOX_SPLIT) * srows:(st % FOX_SPLIT + 1) * srows] for st in range(n_streams)]
    m_ref[...] = jnp.full(m_ref.shape, NEG_INF, F32)
    l_ref[...] = jnp.zeros(l_ref.shape, F32)
    acc_ref[...] = jnp.zeros(acc_ref.shape, F32)
    col_minus_row = (lax.broadcasted_iota(jnp.int32, (ROW_CHUNK, Q_BLOCK), 1)
                     - lax.broadcasted_iota(jnp.int32, (ROW_CHUNK, Q_BLOCK), 0))

    def scores(st, j):
        off = pl.multiple_of(j * kw, kw)
        k_aug = jnp.concatenate([k_ref[pl.ds(off, kw), :], e_ref[pl.ds(off, kw), :]], axis=1)
        s_ref[st] = _nt(qs[st], k_aug)

    def tile(j, diagonal):
        v = v_ref[pl.ds(pl.multiple_of(j * kw, kw), kw), :]
        for st in range(n_streams):
            row0 = (st % FOX_SPLIT) * srows

            def lane_tiles(c, u):
                first_row, first_key = row0 + c * ROW_CHUNK, u * Q_BLOCK
                if diagonal and first_key > first_row + ROW_CHUNK - 1:
                    return None
                s = s_ref[st, c * ROW_CHUNK:(c + 1) * ROW_CHUNK, u * Q_BLOCK:(u + 1) * Q_BLOCK]
                if diagonal and first_key + Q_BLOCK - 1 > first_row:
                    s = jnp.where(col_minus_row <= first_row - first_key, s, NEG_INF)
                return s

            _softmax_tile(lane_tiles, n_chunks, ROW_CHUNK, n_lane_tiles, m_ref.at[st], l_ref.at[st], a_ref.at[st],
                          pm_ref.at[st], p_ref.at[st])
            acc_ref[st] = a_ref[st] * acc_ref[st] + jnp.dot(p_ref[st], v, preferred_element_type=F32)
            if not diagonal:
                scores(st, j + 1)

    def body(j, carry):
        tile(j, False)
        return carry

    for st in range(n_streams):
        scores(st, 0)
    lax.fori_loop(0, t, body, 0)
    tile(t, True)
    o = [jnp.concatenate([acc_ref[st] / jnp.maximum(jnp.sum(l_ref[st], axis=-1, keepdims=True), 1e-30)
                          for st in range(hh * FOX_SPLIT, (hh + 1) * FOX_SPLIT)], axis=0) for hh in range(2)]
    o_ref[...] = jnp.where(lo, o[0], o[1]).astype(o_ref.dtype)


def fox_attention(qkv, decay, batch, seq, tq=512):
    m = batch * seq
    n_pairs = C_HEADS // 2
    nt = seq // tq
    return pl.pallas_call(
        functools.partial(_fox_kernel, tq=tq),
        grid=(batch, n_pairs, nt),
        in_specs=[pl.BlockSpec((tq, PAIR), lambda b, h, t: (b * nt + t, h)),
                  pl.BlockSpec((seq, PAIR), lambda b, h, t: (b, n_pairs + h)),
                  pl.BlockSpec((seq, PAIR), lambda b, h, t: (b, 2 * n_pairs + h)),
                  pl.BlockSpec((seq, PAIR), lambda b, h, t: (b, h))],
        out_specs=pl.BlockSpec((tq, PAIR), lambda b, h, t: (b * nt + t, h)),
        out_shape=jax.ShapeDtypeStruct((m, C_HEADS * HEAD_DIM), BF16),
        scratch_shapes=[pltpu.VMEM((2 * FOX_SPLIT, tq // FOX_SPLIT, tq), F32),
                        pltpu.VMEM((2 * FOX_SPLIT, tq // FOX_SPLIT, tq), BF16)]
        + [pltpu.VMEM((2 * FOX_SPLIT, tq // FOX_SPLIT, PAIR), F32)] * 5,
        compiler_params=_cparams("parallel", "parallel", "arbitrary"),
        name="fox",
    )(qkv, qkv, qkv, decay)


def _rel_bucket(dist):
    n = jnp.maximum(dist, 0)
    max_exact = REL_BUCKETS // 2
    nf = jnp.maximum(n, 1).astype(jnp.float32)
    large = max_exact + (jnp.log(nf / max_exact) / math.log(REL_MAX_DIST / max_exact)
                         * (REL_BUCKETS - max_exact)).astype(jnp.int32)
    return jnp.where(n < max_exact, n, jnp.minimum(large, REL_BUCKETS - 1))


def _lookup(table, idx):
    onehot = (idx[..., None] == jnp.arange(table.shape[0])).astype(F32)
    return jnp.einsum("hn,...n->h...", table.T, onehot, precision=lax.Precision.HIGHEST)


def _bias_tables(rel_bias, seq):
    nb = seq // Q_BLOCK
    n_cmp_pad = seq // NSA_CMP_STRIDE
    ql = jnp.arange(Q_BLOCK)
    d = jnp.arange(NEAR_TILES)
    dist_t = d[:, None, None] * Q_BLOCK + ql[None, :, None] - ql[None, None, :]
    tt = _lookup(rel_bias, _rel_bucket(dist_t))
    dist_a = ql[:, None] + Q_BLOCK - jnp.arange(2 * Q_BLOCK)[None, :]
    seen = (dist_a >= 0) & (dist_a < A_WINDOW)
    own = jnp.arange(2 * Q_BLOCK)[None, :] >= Q_BLOCK
    bias_a = jnp.concatenate([tt[:A_Q_HEADS, 1], tt[:A_Q_HEADS, 0]], axis=-1) * LOG2E
    bias_a = jnp.stack([jnp.where(seen, bias_a, NEG_INF), jnp.where(seen & own, bias_a, NEG_INF)])
    tb = tt[A_Q_HEADS:]
    upper = (ql[None, :] > ql[:, None])[None]
    far = jnp.broadcast_to(rel_bias[REL_BUCKETS - 1, A_Q_HEADS:][:, None, None], tb[:, 0].shape)
    entries = ([jnp.where(upper, NEG_INF, tb[:, 0])] + [tb[:, d] for d in range(1, NEAR_TILES)]
               + [far, jnp.where(upper, tb[:, WIN_TILES - 1], NEG_INF), jnp.full_like(far, NEG_INF)])
    bias_tab = jnp.stack(entries, axis=1).reshape(B_KV_HEADS, B_GROUP, TAB_ENTRIES, Q_BLOCK, Q_BLOCK)
    bias_tab = jnp.transpose(bias_tab, (0, 2, 1, 3, 4)) * LOG2E
    cend = jnp.arange(n_cmp_pad) * NSA_CMP_STRIDE + NSA_CMP_LEN - 1
    dist_c = (jnp.arange(nb)[:, None, None] * Q_BLOCK + ql[None, :, None]) - cend[None, None, :]
    bias_c = _lookup(rel_bias[:, A_Q_HEADS:] * LOG2E, _rel_bucket(dist_c))
    visible = (dist_c >= 0) & (jnp.arange(n_cmp_pad) < n_cmp_pad - 1)
    bias_c = jnp.where(visible[None], bias_c, NEG_INF)
    return bias_a, bias_tab, bias_c


def _overlap_t(seq):
    n_cmp_pad = seq // NSA_CMP_STRIDE
    n_sel = seq // NSA_SEL_LEN
    cstart = np.arange(n_cmp_pad) * NSA_CMP_STRIDE
    sstart = np.arange(n_sel) * NSA_SEL_LEN
    ov = (cstart[None, :] < sstart[:, None] + NSA_SEL_LEN) & (cstart[None, :] + NSA_CMP_LEN > sstart[:, None])
    ov[:, n_cmp_pad - 1] = False
    return jnp.asarray(ov.astype(np.float32), dtype=BF16)


def _dup(w, n_heads):
    d = w.shape[0]
    w = w.reshape(d, n_heads, 1, HEAD_DIM)
    return jnp.broadcast_to(w, (d, n_heads, 2, HEAD_DIM)).reshape(d, n_heads * PAIR)


def _even_weights(w_in):
    sizes = (A_Q_HEADS * HEAD_DIM, A_KV_HEADS * HEAD_DIM, A_KV_HEADS * HEAD_DIM, B_Q_HEADS * HEAD_DIM) \
        + (B_KV_HEADS * HEAD_DIM,) * 6 + (3 * B_Q_HEADS,)
    qa, ka, va, qb, kc, vc, ksl, vsl, kwn, vwn, gt = jnp.split(w_in, np.cumsum(sizes)[:-1].tolist(), axis=-1)
    log2_scale = ATTN_SCALE * LOG2E
    main = jnp.concatenate([qa * log2_scale, _dup(ka, A_KV_HEADS), _dup(va, A_KV_HEADS), qb * log2_scale,
                            _dup(ksl, B_KV_HEADS), _dup(vsl, B_KV_HEADS),
                            _dup(kwn, B_KV_HEADS), _dup(vwn, B_KV_HEADS)], axis=-1).astype(BF16)
    d = w_in.shape[0]
    per_group = 3 * B_GROUP
    gates = [jnp.pad(gt[:, g * per_group:(g + 1) * per_group], ((0, 0), (0, PAIR - per_group)))
             for g in range(B_KV_HEADS)]
    tail = jnp.concatenate([kc, vc] + gates, axis=-1).astype(BF16)
    return main, tail


def _odd_weights(w_in):
    c_mix = C_HEADS * HEAD_DIM
    col_scale = np.concatenate([np.full(c_mix, ATTN_SCALE * LOG2E, np.float32), np.ones(2 * c_mix, np.float32)])
    main = (w_in[:, :, :3 * c_mix] * col_scale).astype(BF16)
    tail = jnp.pad(w_in[:, :, 3 * c_mix:], ((0, 0), (0, 0), (0, PAIR - C_HEADS))).astype(BF16)
    return main, tail


def kernel(x, rel_bias, norm_mix, norm_ffn, norm_final, w_in_even, w_out_even, a_sinks, nsa_pe_k, nsa_pe_v,
           nsa_cmp_k_w1, nsa_cmp_k_w2, nsa_cmp_v_w1, nsa_cmp_v_w2, w_in_odd, w_out_odd, fox_fgate_b,
           w_ffn_up, w_ffn_down):
    batch, seq, d = x.shape
    depth = norm_mix.shape[0]
    m = batch * seq
    xf = x.reshape(m, d)
    bias_a, bias_tab, bias_c = _bias_tables(rel_bias, seq)
    ovt = _overlap_t(seq)
    n_cmp_pad = seq // NSA_CMP_STRIDE
    g_final = norm_final.reshape(1, d)
    w_up_bf16, w_down_bf16 = w_ffn_up.astype(BF16), w_ffn_down.astype(BF16)
    w_out_even_bf16, w_out_odd_bf16 = w_out_even.astype(BF16), w_out_odd.astype(BF16)
    w_odd_main, w_odd_tail = _odd_weights(w_in_odd)

    for layer in range(depth):
        g_mix = norm_mix[layer].reshape(1, d)
        if layer % 2 == 0:
            e = layer // 2
            w_main, w_tail = _even_weights(w_in_even[e])
            qkv, tail = norm_matmul(xf, g_mix, w_main[None], w_tail[None], 0)
            sinks = jnp.broadcast_to((a_sinks[e] * LOG2E).reshape(A_Q_HEADS, 1, 1), (A_Q_HEADS, 1, PAIR))
            a_out = swa_attention(qkv, bias_a, sinks, batch, seq)
            r = tail[:, :2 * B_KV_HEADS * HEAD_DIM].reshape(batch, seq, 2 * B_KV_HEADS, HEAD_DIM)
            r = jnp.transpose(r, (0, 2, 1, 3)).reshape(batch, 2 * B_KV_HEADS, n_cmp_pad, NSA_CMP_STRIDE * HEAD_DIM)
            pe = jnp.stack([nsa_pe_k[e].reshape(1, -1), nsa_pe_v[e].reshape(1, -1)])
            w1 = jnp.stack([nsa_cmp_k_w1[e], nsa_cmp_v_w1[e]]).astype(BF16)
            w2 = jnp.stack([_dup(nsa_cmp_k_w2[e], 1), _dup(nsa_cmp_v_w2[e], 1)]).astype(BF16)
            cmp_kv = nsa_compress(r, pe, w1, w2)
            b_out = nsa_attention(qkv, cmp_kv, tail, bias_c, bias_tab, ovt, batch, seq)
            mixed = (a_out, 0, b_out, 0, w_out_even_bf16, e)
        else:
            o = layer // 2
            qkv, f_tail = norm_matmul(xf, g_mix, w_odd_main, w_odd_tail, o)
            f_bias = jnp.pad(fox_fgate_b[o], (0, PAIR - C_HEADS)).reshape(1, PAIR)
            decay = fox_decay(f_tail, f_bias, batch, seq)
            c_out = fox_attention(qkv, decay, batch, seq)
            mixed = (c_out, 0, c_out, 1, w_out_odd_bf16, o)
        xf = mix_out_ffn(xf, *mixed, norm_ffn[layer].reshape(1, d), w_up_bf16, w_down_bf16, layer,
                         g_final, layer == depth - 1)
    return xf.reshape(batch, seq, d)
```

```python
import functools
import math

import jax
import jax.numpy as jnp
import numpy as np
from jax import lax
from jax.experimental import pallas as pl
from jax.experimental.pallas import tpu as pltpu

F32 = jnp.float32
BF16 = jnp.bfloat16

HEAD_DIM = 64
PAIR = 2 * HEAD_DIM
A_Q_HEADS = 16
A_KV_HEADS = 4
A_WINDOW = 128
B_Q_HEADS = 16
B_KV_HEADS = 2
B_GROUP = B_Q_HEADS // B_KV_HEADS
A_GROUP = A_Q_HEADS // A_KV_HEADS
NSA_CMP_LEN = 32
NSA_CMP_STRIDE = 16
NSA_CMP_HIDDEN = 4 * HEAD_DIM
NSA_SEL_LEN = 64
NSA_TOP_N = 16
NSA_WINDOW = 512
NSA_FORCE_SCORE = 1.0e4
C_HEADS = 32
REL_BUCKETS = 32
REL_MAX_DIST = 1024
Q_BLOCK = 128
RMS_EPS = 1e-6
ATTN_SCALE = HEAD_DIM ** -0.5
LOG2E = 1.4426950408889634
NEAR_TILES = 8
NEG_INF = float("-inf")
ROW_CHUNK = 32

VMEM_LIMIT_BYTES = 56 * 1024 * 1024


def _cparams(*sem):
    return pltpu.CompilerParams(dimension_semantics=sem, vmem_limit_bytes=VMEM_LIMIT_BYTES)


def _nt(a, b):
    return lax.dot_general(a, b, (((1,), (1,)), ((), ())), preferred_element_type=F32)


def _split3(x):
    hi = x.astype(BF16)
    r1 = x - hi.astype(F32)
    mid = r1.astype(BF16)
    lo = (r1 - mid.astype(F32)).astype(BF16)
    return hi, mid, lo


def _rms(x, g):
    ms = jnp.mean(x * x, axis=-1, keepdims=True)
    return x * lax.rsqrt(ms + RMS_EPS) * g


def _stack_pairs(q_ref, first_pair, n_pairs):
    lane = lax.broadcasted_iota(jnp.int32, (Q_BLOCK, PAIR), 1)
    lo = lane < HEAD_DIM
    parts = []
    for p in range(n_pairs):
        qp = q_ref[:, (first_pair + p) * PAIR:(first_pair + p + 1) * PAIR]
        parts.append(jnp.where(lo, qp, jnp.zeros_like(qp)))
        parts.append(jnp.where(lo, jnp.zeros_like(qp), qp))
    return jnp.concatenate(parts, axis=0)


def _unstack_pairs(o, n_pairs, rows):
    lane = lax.broadcasted_iota(jnp.int32, (rows, PAIR), 1)
    lo = lane < HEAD_DIM
    return [jnp.where(lo, o[(2 * p) * rows:(2 * p + 1) * rows], o[(2 * p + 1) * rows:(2 * p + 2) * rows])
            for p in range(n_pairs)]


def _norm_mm_kernel(x_ref, g_ref, w_ref, wt_ref, o_ref, ot_ref, h_ref):
    @pl.when(pl.program_id(1) == 0)
    def _():
        h_ref[...] = _rms(x_ref[...], g_ref[...]).astype(BF16)
        ot_ref[...] = jnp.dot(h_ref[...], wt_ref[...], preferred_element_type=F32)

    o_ref[...] = jnp.dot(h_ref[...], w_ref[...], preferred_element_type=F32).astype(o_ref.dtype)


def norm_matmul(x, g, w, w_tail, idx, tm=1024, tn=1024):
    m, d = x.shape
    n = (w.shape[2] // tn) * tn
    nt = w_tail.shape[2]
    return pl.pallas_call(
        _norm_mm_kernel,
        grid=(m // tm, n // tn),
        in_specs=[pl.BlockSpec((tm, d), lambda i, j: (i, 0)),
                  pl.BlockSpec((1, d), lambda i, j: (0, 0)),
                  pl.BlockSpec((None, d, tn), lambda i, j: (idx, 0, j)),
                  pl.BlockSpec((None, d, nt), lambda i, j: (idx, 0, 0))],
        out_specs=[pl.BlockSpec((tm, tn), lambda i, j: (i, j)),
                   pl.BlockSpec((tm, nt), lambda i, j: (i, 0))],
        out_shape=[jax.ShapeDtypeStruct((m, n), BF16), jax.ShapeDtypeStruct((m, nt), F32)],
        scratch_shapes=[pltpu.VMEM((tm, d), BF16)],
        compiler_params=_cparams("parallel", "arbitrary"),
        name="norm_matmul",
    )(x, g, w, w_tail)


def _mix_ffn_kernel(x_ref, a1_ref, a2_ref, wo1_ref, wo2_ref, g_ref, wu_ref, wd_ref, gf_ref, o_ref, h_ref, acc_ref,
                    *, final_norm):
    k = pl.program_id(1)

    @pl.when(k == 0)
    def _():
        y = jnp.dot(a1_ref[...], wo1_ref[...], preferred_element_type=F32)
        y = y + jnp.dot(a2_ref[...], wo2_ref[...], preferred_element_type=F32)
        x1 = x_ref[...] + y
        acc_ref[...] = x1
        h_ref[...] = _rms(x1, g_ref[...]).astype(BF16)

    u = jnp.dot(h_ref[...], wu_ref[...], preferred_element_type=F32)
    u = jnp.maximum(u, 0.0)
    acc_ref[...] += jnp.dot((u * u).astype(BF16), wd_ref[...], preferred_element_type=F32)

    @pl.when(k == pl.num_programs(1) - 1)
    def _():
        y = acc_ref[...]
        if final_norm:
            y = _rms(y, gf_ref[...])
        o_ref[...] = y


def mix_out_ffn(x, a1, a1_blk, a2, a2_blk, w_out, w_out_idx, g, w_up, w_down, layer, g_final, final_norm,
                tm=512, tf=1024):
    m, d = x.shape
    ff = w_up.shape[2]
    half = d // 2
    return pl.pallas_call(
        functools.partial(_mix_ffn_kernel, final_norm=final_norm),
        grid=(m // tm, ff // tf),
        in_specs=[pl.BlockSpec((tm, d), lambda i, k: (i, 0)),
                  pl.BlockSpec((tm, half), lambda i, k: (i, a1_blk)),
                  pl.BlockSpec((tm, half), lambda i, k: (i, a2_blk)),
                  pl.BlockSpec((None, half, d), lambda i, k: (w_out_idx, 0, 0), pipeline_mode=pl.Buffered(1)),
                  pl.BlockSpec((None, half, d), lambda i, k: (w_out_idx, 1, 0), pipeline_mode=pl.Buffered(1)),
                  pl.BlockSpec((1, d), lambda i, k: (0, 0)),
                  pl.BlockSpec((None, d, tf), lambda i, k: (layer, 0, k)),
                  pl.BlockSpec((None, tf, d), lambda i, k: (layer, k, 0)),
                  pl.BlockSpec((1, d), lambda i, k: (0, 0))],
        out_specs=pl.BlockSpec((tm, d), lambda i, k: (i, 0)),
        out_shape=jax.ShapeDtypeStruct((m, d), F32),
        scratch_shapes=[pltpu.VMEM((tm, d), BF16), pltpu.VMEM((tm, d), F32)],
        compiler_params=_cparams("parallel", "arbitrary"),
        name="mix_ffn",
    )(x, a1, a2, w_out, w_out, g, w_up, w_down, g_final)


def _swa_kernel(q_ref, kp_ref, kc_ref, vp_ref, vc_ref, bias_ref, sink_ref, o_ref,
                s_ref, p_ref, m_ref, l_ref, a_ref, pm_ref):
    first_block = jnp.where(pl.program_id(1) == 0, 1, 0)
    rows = A_GROUP * Q_BLOCK
    chunks_per_head = Q_BLOCK // ROW_CHUNK
    lane = lax.broadcasted_iota(jnp.int32, (Q_BLOCK, PAIR), 1)
    for g in range(A_KV_HEADS):
        qs = _stack_pairs(q_ref, g * (A_GROUP // 2), A_GROUP // 2)
        k = jnp.concatenate([kp_ref[:, g * PAIR:(g + 1) * PAIR], kc_ref[:, g * PAIR:(g + 1) * PAIR]], axis=0)
        v = jnp.concatenate([vp_ref[:, g * PAIR:(g + 1) * PAIR], vc_ref[:, g * PAIR:(g + 1) * PAIR]], axis=0)
        s_ref[g] = _nt(qs, k)
        m_ref[g] = jnp.concatenate([jnp.broadcast_to(sink_ref[g * A_GROUP + r], (Q_BLOCK, PAIR))
                                    for r in range(A_GROUP)], axis=0)
        l_ref[g] = jnp.concatenate([jnp.where(lane == 0, 1.0, 0.0)] * A_GROUP, axis=0)

        def lane_tiles(c, u):
            head = g * A_GROUP + c // chunks_per_head
            q0 = (c % chunks_per_head) * ROW_CHUNK
            return (s_ref[g, c * ROW_CHUNK:(c + 1) * ROW_CHUNK, u * Q_BLOCK:(u + 1) * Q_BLOCK]
                    + bias_ref[first_block, head, q0:q0 + ROW_CHUNK, u * Q_BLOCK:(u + 1) * Q_BLOCK])

        _softmax_tile(lane_tiles, rows // ROW_CHUNK, ROW_CHUNK, 2, m_ref.at[g], l_ref.at[g], a_ref.at[g],
                      pm_ref.at[g], p_ref.at[g])
        o = jnp.dot(p_ref[g], v, preferred_element_type=F32)
        o = o / jnp.sum(l_ref[g], axis=-1, keepdims=True)
        for p, blk in enumerate(_unstack_pairs(o, A_GROUP // 2, Q_BLOCK)):
            c0 = (g * (A_GROUP // 2) + p) * PAIR
            o_ref[:, c0:c0 + PAIR] = blk.astype(o_ref.dtype)


def swa_attention(qkv, bias_a, sinks, batch, seq):
    nb = seq // Q_BLOCK
    m = batch * seq
    qa_w = A_Q_HEADS * HEAD_DIM
    kv_w = A_KV_HEADS * PAIR
    k_blk = qa_w // kv_w
    v_blk = k_blk + 1
    row = lambda b, i: b * nb + i
    prev = lambda b, i: b * nb + jnp.maximum(i - 1, 0)
    return pl.pallas_call(
        _swa_kernel,
        grid=(batch, nb),
        in_specs=[pl.BlockSpec((Q_BLOCK, qa_w), lambda b, i: (row(b, i), 0)),
                  pl.BlockSpec((Q_BLOCK, kv_w), lambda b, i: (prev(b, i), k_blk)),
                  pl.BlockSpec((Q_BLOCK, kv_w), lambda b, i: (row(b, i), k_blk)),
                  pl.BlockSpec((Q_BLOCK, kv_w), lambda b, i: (prev(b, i), v_blk)),
                  pl.BlockSpec((Q_BLOCK, kv_w), lambda b, i: (row(b, i), v_blk)),
                  pl.BlockSpec((2, A_Q_HEADS, Q_BLOCK, 2 * Q_BLOCK), lambda b, i: (0, 0, 0, 0)),
                  pl.BlockSpec((A_Q_HEADS, 1, PAIR), lambda b, i: (0, 0, 0))],
        out_specs=pl.BlockSpec((Q_BLOCK, qa_w), lambda b, i: (row(b, i), 0)),
        out_shape=jax.ShapeDtypeStruct((m, qa_w), BF16),
        scratch_shapes=[pltpu.VMEM((A_KV_HEADS, A_GROUP * Q_BLOCK, 2 * Q_BLOCK), F32),
                        pltpu.VMEM((A_KV_HEADS, A_GROUP * Q_BLOCK, 2 * Q_BLOCK), BF16)]
        + [pltpu.VMEM((A_KV_HEADS, A_GROUP * Q_BLOCK, PAIR), F32)] * 4,
        compiler_params=_cparams("parallel", "parallel"),
        name="swa",
    )(qkv, qkv, qkv, qkv, qkv, bias_a, sinks)


def _compress_kernel(r_ref, pe_ref, w1_ref, w2_ref, o_ref):
    half = NSA_CMP_STRIDE * HEAD_DIM
    r = r_ref[0, 0]
    xa = (r + pe_ref[0, :, :half]).astype(BF16)
    xb = (r + pe_ref[0, :, half:]).astype(BF16)
    a = jnp.dot(xa, w1_ref[0, :half, :], preferred_element_type=F32)
    b = jnp.dot(xb, w1_ref[0, half:, :], preferred_element_type=F32)
    n = r.shape[0]
    hid = jax.nn.gelu(a + pltpu.roll(b, n - 1, 0))
    o_ref[0, 0] = jnp.dot(hid.astype(BF16), w2_ref[0], preferred_element_type=F32).astype(o_ref.dtype)


def nsa_compress(r, pe, w1, w2dup):
    batch, _, n, width = r.shape
    return pl.pallas_call(
        _compress_kernel,
        grid=(batch, 2 * B_KV_HEADS),
        in_specs=[pl.BlockSpec((1, 1, n, width), lambda b, w: (b, w, 0, 0)),
                  pl.BlockSpec((1, 1, 2 * width), lambda b, w: (w // B_KV_HEADS, 0, 0)),
                  pl.BlockSpec((1, 2 * width, NSA_CMP_HIDDEN), lambda b, w: (w // B_KV_HEADS, 0, 0)),
                  pl.BlockSpec((1, NSA_CMP_HIDDEN, PAIR), lambda b, w: (w // B_KV_HEADS, 0, 0))],
        out_specs=pl.BlockSpec((1, 1, n, PAIR), lambda b, w: (b, w, 0, 0)),
        out_shape=jax.ShapeDtypeStruct((batch, 2 * B_KV_HEADS, n, PAIR), BF16),
        compiler_params=_cparams("parallel", "parallel"),
        name="nsa_compress",
    )(r, pe, w1, w2dup)


NSA_KEY_TILE = 512
NSA_STREAMS = 2
WIN_TILES = NSA_WINDOW // Q_BLOCK + 1
TAB_FAR = NEAR_TILES
TAB_WIN_EDGE = NEAR_TILES + 1
TAB_NONE = NEAR_TILES + 2
TAB_ENTRIES = NEAR_TILES + 3
UNSELECTED = -1.0e30


def _nsa_kernel(q_ref, kcm_ref, vcm_ref, ks_ref, vs_ref, kw_ref, vw_ref, gate_ref, bias_c_ref, tab_ref, ovt_ref,
                o_ref, val_ref, psum_ref, s_ref, p_ref, m_ref, l_ref, a_ref, pm_ref, acc_ref, *, n_cmp_pad):
    i = pl.program_id(2)
    rows = B_GROUP * Q_BLOCK
    srows = rows // NSA_STREAMS
    heads_per_stream = B_GROUP // NSA_STREAMS
    chunks_per_head = Q_BLOCK // ROW_CHUNK
    n_sel = val_ref.shape[0]
    sel_shift = int(math.log2(NSA_SEL_LEN))
    qs = _stack_pairs(q_ref, 0, B_GROUP // 2)

    qs_st = [qs[st * srows:(st + 1) * srows] for st in range(NSA_STREAMS)]

    c_tiles = n_cmp_pad // Q_BLOCK
    psum_ref[...] = jnp.zeros(psum_ref.shape, F32)
    o_c = []
    for st in range(NSA_STREAMS):
        s_ref[st, :, :n_cmp_pad] = _nt(qs_st[st], kcm_ref[0, 0])

        def cmp_block(c, u):
            return (slice(c * ROW_CHUNK, (c + 1) * ROW_CHUNK), slice(u * Q_BLOCK, (u + 1) * Q_BLOCK),
                    st * heads_per_stream + c // chunks_per_head, (c % chunks_per_head) * ROW_CHUNK)

        for c in range(srows // ROW_CHUNK):
            pm = None
            for u in range(c_tiles):
                rws, cols, head, q0 = cmp_block(c, u)
                t = s_ref[st, rws, cols] + bias_c_ref[head, q0:q0 + ROW_CHUNK, cols]
                s_ref[st, rws, cols] = t
                pm = t if pm is None else jnp.maximum(pm, t)
            pm_ref[st, c * ROW_CHUNK:(c + 1) * ROW_CHUNK, :] = pm
        m_c = jnp.max(pm_ref[st], axis=-1, keepdims=True)
        m_ref[st] = jnp.broadcast_to(jnp.where(m_c == NEG_INF, 0.0, m_c), m_ref.shape[1:])
        for c in range(srows // ROW_CHUNK):
            mb = m_ref[st, c * ROW_CHUNK:(c + 1) * ROW_CHUNK, :]
            esum = None
            for u in range(c_tiles):
                rws, cols, _, _ = cmp_block(c, u)
                e = jnp.exp2(s_ref[st, rws, cols] - mb)
                s_ref[st, rws, cols] = e
                esum = e if esum is None else esum + e
            l_ref[st, c * ROW_CHUNK:(c + 1) * ROW_CHUNK, :] = esum
        denom = jnp.maximum(jnp.sum(l_ref[st], axis=-1, keepdims=True), 1e-30)
        a_ref[st] = jnp.broadcast_to(1.0 / denom, a_ref.shape[1:])
        for c in range(srows // ROW_CHUNK):
            inv = a_ref[st, c * ROW_CHUNK:(c + 1) * ROW_CHUNK, :]
            for u in range(c_tiles):
                rws, cols, _, q0 = cmp_block(c, u)
                p = s_ref[st, rws, cols] * inv
                p_ref[st, rws, cols] = p.astype(BF16)
                psum_ref[q0:q0 + ROW_CHUNK, cols] += p
        o_c.append(jnp.dot(p_ref[st, :, :n_cmp_pad], vcm_ref[0, 0], preferred_element_type=F32))
    o_c = jnp.concatenate(o_c, axis=0)

    p_sum = psum_ref[...]
    ovt = ovt_ref[...]
    imp = None
    for piece in _split3(p_sum):
        t = _nt(ovt, piece)
        imp = t if imp is None else imp + t
    blk = lax.broadcasted_iota(jnp.int32, (n_sel, Q_BLOCK), 0)
    qpos = lax.broadcasted_iota(jnp.int32, (n_sel, Q_BLOCK), 1) + i * Q_BLOCK
    cur = lax.shift_right_logical(qpos, sel_shift)
    forced = (blk == 0) | (blk == cur) | (blk == cur - 1)
    future = blk * NSA_SEL_LEN > qpos
    val = jnp.where(future, NEG_INF, jnp.where(forced, NSA_FORCE_SCORE, imp))
    val_ref[...] = val
    group = 8
    ranks = []
    for g0 in range(0, n_sel, group):
        val_g = val[g0:g0 + group]
        blk_g = blk[g0:g0 + group]
        rank_g = jnp.zeros((group, Q_BLOCK), F32)
        for s2 in range(n_sel):
            other = val_ref[s2:s2 + 1, :]
            if s2 < g0:
                ahead = other >= val_g
            elif s2 >= g0 + group:
                ahead = other > val_g
            else:
                ahead = ((blk_g > s2) & (other >= val_g)) | (other > val_g)
            rank_g = rank_g + jnp.where(ahead, 1.0, 0.0)
        ranks.append(rank_g)
    rank = jnp.concatenate(ranks, axis=0)
    sel_t = jnp.where((rank < float(NSA_TOP_N)) & (val > NEG_INF), 1.0, 0.0).astype(BF16)
    if n_sel < Q_BLOCK:
        sel_t = jnp.concatenate([sel_t, jnp.zeros((Q_BLOCK - n_sel, Q_BLOCK), BF16)], axis=0)
    eye = jnp.where(lax.broadcasted_iota(jnp.int32, (Q_BLOCK, Q_BLOCK), 0)
                    == lax.broadcasted_iota(jnp.int32, (Q_BLOCK, Q_BLOCK), 1), 1.0, 0.0).astype(BF16)
    unsel = jnp.where(_nt(eye, sel_t) > 0.5, 0.0, UNSELECTED).astype(BF16)
    unsel_rows = jnp.concatenate([unsel] * heads_per_stream, axis=0)
    qs_sel = [jnp.concatenate([qs_st[st], unsel_rows], axis=1) for st in range(NSA_STREAMS)]

    key_lane = lax.broadcasted_iota(jnp.int32, (NSA_KEY_TILE, PAIR), 1)
    key_blk = lax.shift_right_logical(lax.broadcasted_iota(jnp.int32, (NSA_KEY_TILE, PAIR), 0), sel_shift)
    lane_minus_blk = key_lane - key_blk

    sel_refs = (s_ref, p_ref, m_ref, l_ref, a_ref, pm_ref, acc_ref)

    def reset(refs):
        _, _, m_r, l_r, _, _, acc_r = refs
        m_r[...] = jnp.full(m_r.shape, NEG_INF, F32)
        l_r[...] = jnp.zeros(l_r.shape, F32)
        acc_r[...] = jnp.zeros(acc_r.shape, F32)

    def soft_pv(refs, st, tab_idx, v):
        s_r, p_r, m_r, l_r, a_r, pm_r, acc_r = refs
        width = len(tab_idx) * Q_BLOCK

        def raw(c, u):
            return s_r[st, c * ROW_CHUNK:(c + 1) * ROW_CHUNK, u * Q_BLOCK:(u + 1) * Q_BLOCK]

        def biased(c, u):
            head = st * heads_per_stream + c // chunks_per_head
            q0 = (c % chunks_per_head) * ROW_CHUNK
            return raw(c, u) + tab_ref[0, tab_idx[u], head, q0:q0 + ROW_CHUNK, :]

        def keep(c, u, s):
            s_r[st, c * ROW_CHUNK:(c + 1) * ROW_CHUNK, u * Q_BLOCK:(u + 1) * Q_BLOCK] = s

        _softmax_tile(biased, srows // ROW_CHUNK, ROW_CHUNK, len(tab_idx), m_r.at[st], l_r.at[st],
                      a_r.at[st], pm_r.at[st], p_r.at[st], keep=keep, kept_tiles=raw)
        acc_r[st] = a_r[st] * acc_r[st] + jnp.dot(p_r[st, :, :width], v, preferred_element_type=F32)

    def finish(refs):
        _, _, _, l_r, _, _, acc_r = refs
        return jnp.concatenate(
            [acc_r[st] / jnp.maximum(jnp.sum(l_r[st], axis=-1, keepdims=True), 1e-30)
             for st in range(NSA_STREAMS)], axis=0)

    blocks_per_tile = NSA_KEY_TILE // NSA_SEL_LEN
    lane_tiles_per_tile = NSA_KEY_TILE // Q_BLOCK

    def sel_scores(st, jt):
        off = pl.multiple_of(jt * NSA_KEY_TILE, NSA_KEY_TILE)
        block_onehot = jnp.where(lane_minus_blk == jt * blocks_per_tile, 1.0, 0.0).astype(BF16)
        k_aug = jnp.concatenate([ks_ref[pl.ds(off, NSA_KEY_TILE), :], block_onehot], axis=1)
        s_ref[st, :, :NSA_KEY_TILE] = _nt(qs_sel[st], k_aug)

    def sel_tile(jt, last):
        v = vs_ref[pl.ds(pl.multiple_of(jt * NSA_KEY_TILE, NSA_KEY_TILE), NSA_KEY_TILE), :]
        tab_idx = []
        for u in range(lane_tiles_per_tile):
            d = i - (jt * lane_tiles_per_tile + u)
            tab_idx.append(jnp.where(d < 0, TAB_NONE, jnp.minimum(d, TAB_FAR)))
        for st in range(NSA_STREAMS):
            soft_pv(sel_refs, st, tab_idx, v)
            if not last:
                sel_scores(st, jt + 1)

    def sel_body(jt, carry):
        sel_tile(jt, False)
        return carry

    n_tiles = lax.shift_right_logical(i + lane_tiles_per_tile, int(math.log2(lane_tiles_per_tile)))
    reset(sel_refs)
    for st in range(NSA_STREAMS):
        sel_scores(st, 0)
    lax.fori_loop(0, n_tiles - 1, sel_body, 0)
    sel_tile(n_tiles - 1, True)
    o_s = finish(sel_refs)

    first_blk = jnp.maximum(i - (WIN_TILES - 1), 0)
    off = pl.multiple_of(first_blk * Q_BLOCK, Q_BLOCK)
    k_win = kw_ref[pl.ds(off, WIN_TILES * Q_BLOCK), :]
    v_win = vw_ref[pl.ds(off, WIN_TILES * Q_BLOCK), :]
    win_idx = []
    for u in range(WIN_TILES):
        d = i - (first_blk + u)
        win_idx.append(jnp.where(d < 0, TAB_NONE, jnp.where(d == WIN_TILES - 1, TAB_WIN_EDGE, d)))
    reset(sel_refs)
    for st in range(NSA_STREAMS):
        s_ref[st] = _nt(qs_st[st], k_win)
    for st in range(NSA_STREAMS):
        soft_pv(sel_refs, st, win_idx, v_win)
    o_w = finish(sel_refs)

    gates = jax.nn.sigmoid(gate_ref[...])

    def gate_col(br):
        return jnp.concatenate([gates[:, 3 * r + br:3 * r + br + 1] for r in range(B_GROUP)], axis=0)

    o = gate_col(0) * o_c + gate_col(1) * o_s + gate_col(2) * o_w
    for p, blk_out in enumerate(_unstack_pairs(o, B_GROUP // 2, Q_BLOCK)):
        o_ref[:, p * PAIR:(p + 1) * PAIR] = blk_out.astype(o_ref.dtype)


def nsa_attention(qkv, cmp_kv, tail, bias_c, bias_tab, ovt, batch, seq):
    nb = seq // Q_BLOCK
    m = batch * seq
    n_cmp_pad = seq // NSA_CMP_STRIDE
    n_sel = seq // NSA_SEL_LEN
    assert n_sel <= Q_BLOCK and seq % NSA_KEY_TILE == 0 and seq >= WIN_TILES * Q_BLOCK
    assert n_cmp_pad % Q_BLOCK == 0 and n_cmp_pad <= NSA_KEY_TILE
    grp_w = B_GROUP * HEAD_DIM
    q_blk0 = (A_Q_HEADS * HEAD_DIM + 2 * A_KV_HEADS * PAIR) // grp_w
    kv_blk0 = (A_Q_HEADS * HEAD_DIM + 2 * A_KV_HEADS * PAIR + B_Q_HEADS * HEAD_DIM) // PAIR
    kv_spec = lambda t: pl.BlockSpec((seq, PAIR), lambda b, g, i: (b, kv_blk0 + t * B_KV_HEADS + g))
    srows = B_GROUP * Q_BLOCK // NSA_STREAMS

    def branch_scratch(width):
        return ([pltpu.VMEM((NSA_STREAMS, srows, width), F32), pltpu.VMEM((NSA_STREAMS, srows, width), BF16)]
                + [pltpu.VMEM((NSA_STREAMS, srows, PAIR), F32)] * 5)

    return pl.pallas_call(
        functools.partial(_nsa_kernel, n_cmp_pad=n_cmp_pad),
        grid=(batch, B_KV_HEADS, nb),
        in_specs=[pl.BlockSpec((Q_BLOCK, grp_w), lambda b, g, i: (b * nb + i, q_blk0 + g)),
                  pl.BlockSpec((1, 1, n_cmp_pad, PAIR), lambda b, g, i: (b, g, 0, 0)),
                  pl.BlockSpec((1, 1, n_cmp_pad, PAIR), lambda b, g, i: (b, B_KV_HEADS + g, 0, 0)),
                  kv_spec(0), kv_spec(1), kv_spec(2), kv_spec(3),
                  pl.BlockSpec((Q_BLOCK, PAIR), lambda b, g, i: (b * nb + i, 2 + g)),
                  pl.BlockSpec((B_GROUP, None, Q_BLOCK, n_cmp_pad), lambda b, g, i: (g, i, 0, 0)),
                  pl.BlockSpec((1, TAB_ENTRIES, B_GROUP, Q_BLOCK, Q_BLOCK), lambda b, g, i: (g, 0, 0, 0, 0)),
                  pl.BlockSpec((n_sel, n_cmp_pad), lambda b, g, i: (0, 0))],
        out_specs=pl.BlockSpec((Q_BLOCK, grp_w), lambda b, g, i: (b * nb + i, g)),
        out_shape=jax.ShapeDtypeStruct((m, B_Q_HEADS * HEAD_DIM), BF16),
        scratch_shapes=[pltpu.VMEM((n_sel, Q_BLOCK), F32), pltpu.VMEM((Q_BLOCK, n_cmp_pad), F32)]
        + branch_scratch(max(NSA_KEY_TILE, WIN_TILES * Q_BLOCK)),
        compiler_params=_cparams("parallel", "parallel", "arbitrary"),
        name="nsa",
    )(qkv, cmp_kv, cmp_kv, qkv, qkv, qkv, qkv, tail, bias_c, bias_tab, ovt)


DECAY_STEP = 512
FOX_SPLIT = 1


def _decay_kernel(f_ref, fb_ref, place_ref, o_ref, carry_ref):
    r_i = lax.broadcasted_iota(jnp.int32, (Q_BLOCK, Q_BLOCK), 0)
    c_i = lax.broadcasted_iota(jnp.int32, (Q_BLOCK, Q_BLOCK), 1)
    tri = jnp.where(c_i <= r_i, 1.0, 0.0).astype(BF16)

    @pl.when(pl.program_id(1) == 0)
    def _():
        carry_ref[...] = jnp.zeros_like(carry_ref)

    carry = carry_ref[...]
    for r in range(DECAY_STEP // Q_BLOCK):
        x = f_ref[r * Q_BLOCK:(r + 1) * Q_BLOCK, :] + fb_ref[...]
        ls = jax.nn.log_sigmoid(x)
        cs = jnp.broadcast_to(carry, ls.shape)
        for piece in _split3(ls):
            cs = cs + jnp.dot(tri, piece, preferred_element_type=F32)
        feat = None
        for n, piece in enumerate(_split3(cs * LOG2E)):
            term = jnp.dot(piece, place_ref[n], preferred_element_type=F32)
            feat = term if feat is None else feat + term
        o_ref[r * Q_BLOCK:(r + 1) * Q_BLOCK, :] = feat.astype(o_ref.dtype)
        carry = cs[Q_BLOCK - 1:Q_BLOCK, :]
    carry_ref[...] = carry


def _decay_placement():
    place = np.zeros((3, PAIR, (C_HEADS // 2) * PAIR), np.float32)
    for h in range(C_HEADS):
        for n in range(3):
            place[n, h, (h // 2) * PAIR + 3 * (h % 2) + n] = -1.0
    return jnp.asarray(place, dtype=BF16)


def fox_decay(f_tail, f_bias, batch, seq):
    steps = seq // DECAY_STEP
    width = (C_HEADS // 2) * PAIR
    return pl.pallas_call(
        _decay_kernel,
        grid=(batch, steps),
        in_specs=[pl.BlockSpec((DECAY_STEP, PAIR), lambda b, s: (b * steps + s, 0)),
                  pl.BlockSpec((1, PAIR), lambda b, s: (0, 0)),
                  pl.BlockSpec((3, PAIR, width), lambda b, s: (0, 0, 0))],
        out_specs=pl.BlockSpec((DECAY_STEP, width), lambda b, s: (b * steps + s, 0)),
        out_shape=jax.ShapeDtypeStruct((batch * seq, width), BF16),
        scratch_shapes=[pltpu.VMEM((1, PAIR), F32)],
        compiler_params=_cparams("parallel", "arbitrary"),
        name="fox_decay",
    )(f_tail, f_bias, _decay_placement())


def _softmax_tile(lane_tiles, n_chunks, rc, n_lane_tiles, m_ref, l_ref, a_ref, pm_ref, p_ref, keep=None,
                  kept_tiles=None):
    second = lane_tiles if kept_tiles is None else kept_tiles
    for c in range(n_chunks):
        pm = None
        for u in range(n_lane_tiles):
            s = lane_tiles(c, u)
            if s is None:
                continue
            if keep is not None:
                keep(c, u, s)
            pm = s if pm is None else jnp.maximum(pm, s)
        pm_ref[c * rc:(c + 1) * rc, :] = pm
    m_old = m_ref[...]
    m_new = jnp.maximum(m_old, jnp.max(pm_ref[...], axis=-1, keepdims=True))
    a_ref[...] = jnp.exp2(m_old - m_new)
    m_ref[...] = m_new
    for c in range(n_chunks):
        rows = slice(c * rc, (c + 1) * rc)
        mb = m_ref[rows, :]
        psum = None
        for u in range(n_lane_tiles):
            s = second(c, u)
            if s is None:
                p_ref[rows, u * Q_BLOCK:(u + 1) * Q_BLOCK] = jnp.zeros((rc, Q_BLOCK), BF16)
                continue
            p = jnp.exp2(s - mb)
            p_ref[rows, u * Q_BLOCK:(u + 1) * Q_BLOCK] = p.astype(BF16)
            psum = p if psum is None else psum + p
        l_ref[rows, :] = a_ref[rows, :] * l_ref[rows, :] + psum


def _fox_kernel(q_ref, k_ref, v_ref, e_ref, o_ref, s_ref, p_ref, m_ref, l_ref, a_ref, pm_ref, acc_ref, *, tq):
    t = pl.program_id(2)
    kw = tq
    n_lane_tiles = kw // Q_BLOCK
    srows = tq // FOX_SPLIT
    n_chunks = srows // ROW_CHUNK
    n_streams = 2 * FOX_SPLIT
    q = q_ref[...]
    lane = lax.broadcasted_iota(jnp.int32, (tq, PAIR), 1)
    lo = lane < HEAD_DIM
    zero = jnp.zeros_like(q)
    q_aug = [jnp.concatenate([jnp.where(lo if hh == 0 else ~lo, q, zero),
                              jnp.where((lane >= 3 * hh) & (lane < 3 * hh + 3), 1.0, 0.0).astype(BF16)], axis=1)
             for hh in range(2)]
    qs = [q_aug[st // FOX_SPLIT][(st % FOX_SPLIT) * srows:(st % FOX_SPLIT + 1) * srows] for st in range(n_streams)]
    m_ref[...] = jnp.full(m_ref.shape, NEG_INF, F32)
    l_ref[...] = jnp.zeros(l_ref.shape, F32)
    acc_ref[...] = jnp.zeros(acc_ref.shape, F32)
    col_minus_row = (lax.broadcasted_iota(jnp.int32, (ROW_CHUNK, Q_BLOCK), 1)
                     - lax.broadcasted_iota(jnp.int32, (ROW_CHUNK, Q_BLOCK), 0))

    def scores(st, j):
        off = pl.multiple_of(j * kw, kw)
        k_aug = jnp.concatenate([k_ref[pl.ds(off, kw), :], e_ref[pl.ds(off, kw), :]], axis=1)
        s_ref[st] = _nt(qs[st], k_aug)

    def tile(j, diagonal):
        v = v_ref[pl.ds(pl.multiple_of(j * kw, kw), kw), :]
        for st in range(n_streams):
            row0 = (st % FOX_SPLIT) * srows

            def lane_tiles(c, u):
                first_row, first_key = row0 + c * ROW_CHUNK, u * Q_BLOCK
                if diagonal and first_key > first_row + ROW_CHUNK - 1:
                    return None
                s = s_ref[st, c * ROW_CHUNK:(c + 1) * ROW_CHUNK, u * Q_BLOCK:(u + 1) * Q_BLOCK]
                if diagonal and first_key + Q_BLOCK - 1 > first_row:
                    s = jnp.where(col_minus_row <= first_row - first_key, s, NEG_INF)
                return s

            _softmax_tile(lane_tiles, n_chunks, ROW_CHUNK, n_lane_tiles, m_ref.at[st], l_ref.at[st], a_ref.at[st],
                          pm_ref.at[st], p_ref.at[st])
            acc_ref[st] = a_ref[st] * acc_ref[st] + jnp.dot(p_ref[st], v, preferred_element_type=F32)
            if not diagonal:
                scores(st, j + 1)

    def body(j, carry):
        tile(j, False)
        return carry

    for st in range(n_streams):
        scores(st, 0)
    lax.fori_loop(0, t, body, 0)
    tile(t, True)
    o = [jnp.concatenate([acc_ref[st] / jnp.maximum(jnp.sum(l_ref[st], axis=-1, keepdims=True), 1e-30)
                          for st in range(hh * FOX_SPLIT, (hh + 1) * FOX_SPLIT)], axis=0) for hh in range(2)]
    o_ref[...] = jnp.where(lo, o[0], o[1]).astype(o_ref.dtype)


def fox_attention(qkv, decay, batch, seq, tq=512):
    m = batch * seq
    n_pairs = C_HEADS // 2
    nt = seq // tq
    return pl.pallas_call(
        functools.partial(_fox_kernel, tq=tq),
        grid=(batch, n_pairs, nt),
        in_specs=[pl.BlockSpec((tq, PAIR), lambda b, h, t: (b * nt + t, h)),
                  pl.BlockSpec((seq, PAIR), lambda b, h, t: (b, n_pairs + h)),
                  pl.BlockSpec((seq, PAIR), lambda b, h, t: (b, 2 * n_pairs + h)),
                  pl.BlockSpec((seq, PAIR), lambda b, h, t: (b, h))],
        out_specs=pl.BlockSpec((tq, PAIR), lambda b, h, t: (b * nt + t, h)),
        out_shape=jax.ShapeDtypeStruct((m, C_HEADS * HEAD_DIM), BF16),
        scratch_shapes=[pltpu.VMEM((2 * FOX_SPLIT, tq // FOX_SPLIT, tq), F32),
                        pltpu.VMEM((2 * FOX_SPLIT, tq // FOX_SPLIT, tq), BF16)]
        + [pltpu.VMEM((2 * FOX_SPLIT, tq // FOX_SPLIT, PAIR), F32)] * 5,
        compiler_params=_cparams("parallel", "parallel", "arbitrary"),
        name="fox",
    )(qkv, qkv, qkv, decay)


def _rel_bucket(dist):
    n = jnp.maximum(dist, 0)
    max_exact = REL_BUCKETS // 2
    nf = jnp.maximum(n, 1).astype(jnp.float32)
    large = max_exact + (jnp.log(nf / max_exact) / math.log(REL_MAX_DIST / max_exact)
                         * (REL_BUCKETS - max_exact)).astype(jnp.int32)
    return jnp.where(n < max_exact, n, jnp.minimum(large, REL_BUCKETS - 1))


def _lookup(table, idx):
    onehot = (idx[..., None] == jnp.arange(table.shape[0])).astype(F32)
    return jnp.einsum("hn,...n->h...", table.T, onehot, precision=lax.Precision.HIGHEST)


def _bias_tables(rel_bias, seq):
    nb = seq // Q_BLOCK
    n_cmp_pad = seq // NSA_CMP_STRIDE
    ql = jnp.arange(Q_BLOCK)
    d = jnp.arange(NEAR_TILES)
    dist_t = d[:, None, None] * Q_BLOCK + ql[None, :, None] - ql[None, None, :]
    tt = _lookup(rel_bias, _rel_bucket(dist_t))
    dist_a = ql[:, None] + Q_BLOCK - jnp.arange(2 * Q_BLOCK)[None, :]
    seen = (dist_a >= 0) & (dist_a < A_WINDOW)
    own = jnp.arange(2 * Q_BLOCK)[None, :] >= Q_BLOCK
    bias_a = jnp.concatenate([tt[:A_Q_HEADS, 1], tt[:A_Q_HEADS, 0]], axis=-1) * LOG2E
    bias_a = jnp.stack([jnp.where(seen, bias_a, NEG_INF), jnp.where(seen & own, bias_a, NEG_INF)])
    tb = tt[A_Q_HEADS:]
    upper = (ql[None, :] > ql[:, None])[None]
    far = jnp.broadcast_to(rel_bias[REL_BUCKETS - 1, A_Q_HEADS:][:, None, None], tb[:, 0].shape)
    entries = ([jnp.where(upper, NEG_INF, tb[:, 0])] + [tb[:, d] for d in range(1, NEAR_TILES)]
               + [far, jnp.where(upper, tb[:, WIN_TILES - 1], NEG_INF), jnp.full_like(far, NEG_INF)])
    bias_tab = jnp.stack(entries, axis=1).reshape(B_KV_HEADS, B_GROUP, TAB_ENTRIES, Q_BLOCK, Q_BLOCK)
    bias_tab = jnp.transpose(bias_tab, (0, 2, 1, 3, 4)) * LOG2E
    cend = jnp.arange(n_cmp_pad) * NSA_CMP_STRIDE + NSA_CMP_LEN - 1
    dist_c = (jnp.arange(nb)[:, None, None] * Q_BLOCK + ql[None, :, None]) - cend[None, None, :]
    bias_c = _lookup(rel_bias[:, A_Q_HEADS:] * LOG2E, _rel_bucket(dist_c))
    visible = (dist_c >= 0) & (jnp.arange(n_cmp_pad) < n_cmp_pad - 1)
    bias_c = jnp.where(visible[None], bias_c, NEG_INF)
    return bias_a, bias_tab, bias_c


def _overlap_t(seq):
    n_cmp_pad = seq // NSA_CMP_STRIDE
    n_sel = seq // NSA_SEL_LEN
    cstart = np.arange(n_cmp_pad) * NSA_CMP_STRIDE
    sstart = np.arange(n_sel) * NSA_SEL_LEN
    ov = (cstart[None, :] < sstart[:, None] + NSA_SEL_LEN) & (cstart[None, :] + NSA_CMP_LEN > sstart[:, None])
    ov[:, n_cmp_pad - 1] = False
    return jnp.asarray(ov.astype(np.float32), dtype=BF16)


def _dup(w, n_heads):
    d = w.shape[0]
    w = w.reshape(d, n_heads, 1, HEAD_DIM)
    return jnp.broadcast_to(w, (d, n_heads, 2, HEAD_DIM)).reshape(d, n_heads * PAIR)


def _even_weights(w_in):
    sizes = (A_Q_HEADS * HEAD_DIM, A_KV_HEADS * HEAD_DIM, A_KV_HEADS * HEAD_DIM, B_Q_HEADS * HEAD_DIM) \
        + (B_KV_HEADS * HEAD_DIM,) * 6 + (3 * B_Q_HEADS,)
    qa, ka, va, qb, kc, vc, ksl, vsl, kwn, vwn, gt = jnp.split(w_in, np.cumsum(sizes)[:-1].tolist(), axis=-1)
    log2_scale = ATTN_SCALE * LOG2E
    main = jnp.concatenate([qa * log2_scale, _dup(ka, A_KV_HEADS), _dup(va, A_KV_HEADS), qb * log2_scale,
                            _dup(ksl, B_KV_HEADS), _dup(vsl, B_KV_HEADS),
                            _dup(kwn, B_KV_HEADS), _dup(vwn, B_KV_HEADS)], axis=-1).astype(BF16)
    d = w_in.shape[0]
    per_group = 3 * B_GROUP
    gates = [jnp.pad(gt[:, g * per_group:(g + 1) * per_group], ((0, 0), (0, PAIR - per_group)))
             for g in range(B_KV_HEADS)]
    tail = jnp.concatenate([kc, vc] + gates, axis=-1).astype(BF16)
    return main, tail


def _odd_weights(w_in):
    c_mix = C_HEADS * HEAD_DIM
    col_scale = np.concatenate([np.full(c_mix, ATTN_SCALE * LOG2E, np.float32),
                                np.ones(w_in.shape[2] - c_mix, np.float32)])
    main = (w_in * col_scale).astype(BF16)
    tail = jnp.pad(w_in[:, :, 3 * c_mix:], ((0, 0), (0, 0), (0, PAIR - C_HEADS))).astype(BF16)
    return main, tail


def kernel(x, rel_bias, norm_mix, norm_ffn, norm_final, w_in_even, w_out_even, a_sinks, nsa_pe_k, nsa_pe_v,
           nsa_cmp_k_w1, nsa_cmp_k_w2, nsa_cmp_v_w1, nsa_cmp_v_w2, w_in_odd, w_out_odd, fox_fgate_b,
           w_ffn_up, w_ffn_down):
    batch, seq, d = x.shape
    depth = norm_mix.shape[0]
    m = batch * seq
    xf = x.reshape(m, d)
    bias_a, bias_tab, bias_c = _bias_tables(rel_bias, seq)
    ovt = _overlap_t(seq)
    n_cmp_pad = seq // NSA_CMP_STRIDE
    g_final = norm_final.reshape(1, d)
    w_up_bf16, w_down_bf16 = w_ffn_up.astype(BF16), w_ffn_down.astype(BF16)
    w_out_even_bf16, w_out_odd_bf16 = w_out_even.astype(BF16), w_out_odd.astype(BF16)
    w_odd_main, w_odd_tail = _odd_weights(w_in_odd)

    for layer in range(depth):
        g_mix = norm_mix[layer].reshape(1, d)
        if layer % 2 == 0:
            e = layer // 2
            w_main, w_tail = _even_weights(w_in_even[e])
            qkv, tail = norm_matmul(xf, g_mix, w_main[None], w_tail[None], 0)
            sinks = jnp.broadcast_to((a_sinks[e] * LOG2E).reshape(A_Q_HEADS, 1, 1), (A_Q_HEADS, 1, PAIR))
            a_out = swa_attention(qkv, bias_a, sinks, batch, seq)
            r = tail[:, :2 * B_KV_HEADS * HEAD_DIM].reshape(batch, seq, 2 * B_KV_HEADS, HEAD_DIM)
            r = jnp.transpose(r, (0, 2, 1, 3)).reshape(batch, 2 * B_KV_HEADS, n_cmp_pad, NSA_CMP_STRIDE * HEAD_DIM)
            pe = jnp.stack([nsa_pe_k[e].reshape(1, -1), nsa_pe_v[e].reshape(1, -1)])
            w1 = jnp.stack([nsa_cmp_k_w1[e], nsa_cmp_v_w1[e]]).astype(BF16)
            w2 = jnp.stack([_dup(nsa_cmp_k_w2[e], 1), _dup(nsa_cmp_v_w2[e], 1)]).astype(BF16)
            cmp_kv = nsa_compress(r, pe, w1, w2)
            b_out = nsa_attention(qkv, cmp_kv, tail, bias_c, bias_tab, ovt, batch, seq)
            mixed = (a_out, 0, b_out, 0, w_out_even_bf16, e)
        else:
            o = layer // 2
            qkv, f_tail = norm_matmul(xf, g_mix, w_odd_main, w_odd_tail, o)
            f_bias = jnp.pad(fox_fgate_b[o], (0, PAIR - C_HEADS)).reshape(1, PAIR)
            decay = fox_decay(f_tail, f_bias, batch, seq)
            c_out = fox_attention(qkv, decay, batch, seq)
            mixed = (c_out, 0, c_out, 1, w_out_odd_bf16, o)
        xf = mix_out_ffn(xf, *mixed, norm_ffn[layer].reshape(1, d), w_up_bf16, w_down_bf16, layer,
                         g_final, layer == depth - 1)
    return xf.reshape(batch, seq, d)
```

```python
import functools
import math

import jax
import jax.numpy as jnp
import numpy as np
from jax import lax
from jax.experimental import pallas as pl
from jax.experimental.pallas import tpu as pltpu

F32 = jnp.float32
BF16 = jnp.bfloat16

HEAD_DIM = 64
PAIR = 2 * HEAD_DIM
A_Q_HEADS = 16
A_KV_HEADS = 4
A_WINDOW = 128
B_Q_HEADS = 16
B_KV_HEADS = 2
B_GROUP = B_Q_HEADS // B_KV_HEADS
A_GROUP = A_Q_HEADS // A_KV_HEADS
NSA_CMP_LEN = 32
NSA_CMP_STRIDE = 16
NSA_CMP_HIDDEN = 4 * HEAD_DIM
NSA_SEL_LEN = 64
NSA_TOP_N = 16
NSA_WINDOW = 512
NSA_FORCE_SCORE = 1.0e4
C_HEADS = 32
REL_BUCKETS = 32
REL_MAX_DIST = 1024
Q_BLOCK = 128
RMS_EPS = 1e-6
ATTN_SCALE = HEAD_DIM ** -0.5
LOG2E = 1.4426950408889634
NEAR_TILES = 8
NEG_INF = float("-inf")
ROW_CHUNK = 32

VMEM_LIMIT_BYTES = 56 * 1024 * 1024


def _cparams(*sem):
    return pltpu.CompilerParams(dimension_semantics=sem, vmem_limit_bytes=VMEM_LIMIT_BYTES)


def _nt(a, b):
    return lax.dot_general(a, b, (((1,), (1,)), ((), ())), preferred_element_type=F32)


def _split3(x):
    hi = x.astype(BF16)
    r1 = x - hi.astype(F32)
    mid = r1.astype(BF16)
    lo = (r1 - mid.astype(F32)).astype(BF16)
    return hi, mid, lo


def _rms(x, g):
    ms = jnp.mean(x * x, axis=-1, keepdims=True)
    return x * lax.rsqrt(ms + RMS_EPS) * g


def _stack_pairs(q_ref, first_pair, n_pairs):
    lane = lax.broadcasted_iota(jnp.int32, (Q_BLOCK, PAIR), 1)
    lo = lane < HEAD_DIM
    parts = []
    for p in range(n_pairs):
        qp = q_ref[:, (first_pair + p) * PAIR:(first_pair + p + 1) * PAIR]
        parts.append(jnp.where(lo, qp, jnp.zeros_like(qp)))
        parts.append(jnp.where(lo, jnp.zeros_like(qp), qp))
    return jnp.concatenate(parts, axis=0)


def _unstack_pairs(o, n_pairs, rows):
    lane = lax.broadcasted_iota(jnp.int32, (rows, PAIR), 1)
    lo = lane < HEAD_DIM
    return [jnp.where(lo, o[(2 * p) * rows:(2 * p + 1) * rows], o[(2 * p + 1) * rows:(2 * p + 2) * rows])
            for p in range(n_pairs)]


def _norm_mm_kernel(x_ref, g_ref, w_ref, wt_ref, o_ref, ot_ref, h_ref):
    @pl.when(pl.program_id(1) == 0)
    def _():
        h_ref[...] = _rms(x_ref[...], g_ref[...]).astype(BF16)
        ot_ref[...] = jnp.dot(h_ref[...], wt_ref[...], preferred_element_type=F32)

    o_ref[...] = jnp.dot(h_ref[...], w_ref[...], preferred_element_type=F32).astype(o_ref.dtype)


def norm_matmul(x, g, w, w_tail, idx, tm=1024, tn=1024):
    m, d = x.shape
    n = (w.shape[2] // tn) * tn
    nt = w_tail.shape[2]
    return pl.pallas_call(
        _norm_mm_kernel,
        grid=(m // tm, n // tn),
        in_specs=[pl.BlockSpec((tm, d), lambda i, j: (i, 0)),
                  pl.BlockSpec((1, d), lambda i, j: (0, 0)),
                  pl.BlockSpec((None, d, tn), lambda i, j: (idx, 0, j)),
                  pl.BlockSpec((None, d, nt), lambda i, j: (idx, 0, 0))],
        out_specs=[pl.BlockSpec((tm, tn), lambda i, j: (i, j)),
                   pl.BlockSpec((tm, nt), lambda i, j: (i, 0))],
        out_shape=[jax.ShapeDtypeStruct((m, n), BF16), jax.ShapeDtypeStruct((m, nt), F32)],
        scratch_shapes=[pltpu.VMEM((tm, d), BF16)],
        compiler_params=_cparams("parallel", "arbitrary"),
        name="norm_matmul",
    )(x, g, w, w_tail)


def _mix_ffn_kernel(x_ref, a1_ref, a2_ref, wo1_ref, wo2_ref, g_ref, wu_ref, wd_ref, gf_ref, o_ref, h_ref, acc_ref,
                    *, final_norm):
    k = pl.program_id(1)

    @pl.when(k == 0)
    def _():
        y = jnp.dot(a1_ref[...], wo1_ref[...], preferred_element_type=F32)
        y = y + jnp.dot(a2_ref[...], wo2_ref[...], preferred_element_type=F32)
        x1 = x_ref[...] + y
        acc_ref[...] = x1
        h_ref[...] = _rms(x1, g_ref[...]).astype(BF16)

    u = jnp.dot(h_ref[...], wu_ref[...], preferred_element_type=F32)
    u = jnp.maximum(u, 0.0)
    acc_ref[...] += jnp.dot((u * u).astype(BF16), wd_ref[...], preferred_element_type=F32)

    @pl.when(k == pl.num_programs(1) - 1)
    def _():
        y = acc_ref[...]
        if final_norm:
            y = _rms(y, gf_ref[...])
        o_ref[...] = y


def mix_out_ffn(x, a1, a1_blk, a2, a2_blk, w_out, w_out_idx, g, w_up, w_down, layer, g_final, final_norm,
                tm=512, tf=1024):
    m, d = x.shape
    ff = w_up.shape[2]
    half = d // 2
    return pl.pallas_call(
        functools.partial(_mix_ffn_kernel, final_norm=final_norm),
        grid=(m // tm, ff // tf),
        in_specs=[pl.BlockSpec((tm, d), lambda i, k: (i, 0)),
                  pl.BlockSpec((tm, half), lambda i, k: (i, a1_blk)),
                  pl.BlockSpec((tm, half), lambda i, k: (i, a2_blk)),
                  pl.BlockSpec((None, half, d), lambda i, k: (w_out_idx, 0, 0), pipeline_mode=pl.Buffered(1)),
                  pl.BlockSpec((None, half, d), lambda i, k: (w_out_idx, 1, 0), pipeline_mode=pl.Buffered(1)),
                  pl.BlockSpec((1, d), lambda i, k: (0, 0)),
                  pl.BlockSpec((None, d, tf), lambda i, k: (layer, 0, k)),
                  pl.BlockSpec((None, tf, d), lambda i, k: (layer, k, 0)),
                  pl.BlockSpec((1, d), lambda i, k: (0, 0))],
        out_specs=pl.BlockSpec((tm, d), lambda i, k: (i, 0)),
        out_shape=jax.ShapeDtypeStruct((m, d), F32),
        scratch_shapes=[pltpu.VMEM((tm, d), BF16), pltpu.VMEM((tm, d), F32)],
        compiler_params=_cparams("parallel", "arbitrary"),
        name="mix_ffn",
    )(x, a1, a2, w_out, w_out, g, w_up, w_down, g_final)


def _swa_kernel(q_ref, kp_ref, kc_ref, vp_ref, vc_ref, bias_ref, sink_ref, o_ref,
                s_ref, p_ref, m_ref, l_ref, a_ref, pm_ref):
    first_block = jnp.where(pl.program_id(1) == 0, 1, 0)
    rows = A_GROUP * Q_BLOCK
    chunks_per_head = Q_BLOCK // ROW_CHUNK
    lane = lax.broadcasted_iota(jnp.int32, (Q_BLOCK, PAIR), 1)
    for g in range(A_KV_HEADS):
        qs = _stack_pairs(q_ref, g * (A_GROUP // 2), A_GROUP // 2)
        k = jnp.concatenate([kp_ref[:, g * PAIR:(g + 1) * PAIR], kc_ref[:, g * PAIR:(g + 1) * PAIR]], axis=0)
        v = jnp.concatenate([vp_ref[:, g * PAIR:(g + 1) * PAIR], vc_ref[:, g * PAIR:(g + 1) * PAIR]], axis=0)
        s_ref[g] = _nt(qs, k)
        m_ref[g] = jnp.concatenate([jnp.broadcast_to(sink_ref[g * A_GROUP + r], (Q_BLOCK, PAIR))
                                    for r in range(A_GROUP)], axis=0)
        l_ref[g] = jnp.concatenate([jnp.where(lane == 0, 1.0, 0.0)] * A_GROUP, axis=0)

        def lane_tiles(c, u):
            head = g * A_GROUP + c // chunks_per_head
            q0 = (c % chunks_per_head) * ROW_CHUNK
            return (s_ref[g, c * ROW_CHUNK:(c + 1) * ROW_CHUNK, u * Q_BLOCK:(u + 1) * Q_BLOCK]
                    + bias_ref[first_block, head, q0:q0 + ROW_CHUNK, u * Q_BLOCK:(u + 1) * Q_BLOCK])

        _softmax_tile(lane_tiles, rows // ROW_CHUNK, ROW_CHUNK, 2, m_ref.at[g], l_ref.at[g], a_ref.at[g],
                      pm_ref.at[g], p_ref.at[g])
        o = jnp.dot(p_ref[g], v, preferred_element_type=F32)
        o = o / jnp.sum(l_ref[g], axis=-1, keepdims=True)
        for p, blk in enumerate(_unstack_pairs(o, A_GROUP // 2, Q_BLOCK)):
            c0 = (g * (A_GROUP // 2) + p) * PAIR
            o_ref[:, c0:c0 + PAIR] = blk.astype(o_ref.dtype)


def swa_attention(qkv, bias_a, sinks, batch, seq):
    nb = seq // Q_BLOCK
    m = batch * seq
    qa_w = A_Q_HEADS * HEAD_DIM
    kv_w = A_KV_HEADS * PAIR
    k_blk = qa_w // kv_w
    v_blk = k_blk + 1
    row = lambda b, i: b * nb + i
    prev = lambda b, i: b * nb + jnp.maximum(i - 1, 0)
    return pl.pallas_call(
        _swa_kernel,
        grid=(batch, nb),
        in_specs=[pl.BlockSpec((Q_BLOCK, qa_w), lambda b, i: (row(b, i), 0)),
                  pl.BlockSpec((Q_BLOCK, kv_w), lambda b, i: (prev(b, i), k_blk)),
                  pl.BlockSpec((Q_BLOCK, kv_w), lambda b, i: (row(b, i), k_blk)),
                  pl.BlockSpec((Q_BLOCK, kv_w), lambda b, i: (prev(b, i), v_blk)),
                  pl.BlockSpec((Q_BLOCK, kv_w), lambda b, i: (row(b, i), v_blk)),
                  pl.BlockSpec((2, A_Q_HEADS, Q_BLOCK, 2 * Q_BLOCK), lambda b, i: (0, 0, 0, 0)),
                  pl.BlockSpec((A_Q_HEADS, 1, PAIR), lambda b, i: (0, 0, 0))],
        out_specs=pl.BlockSpec((Q_BLOCK, qa_w), lambda b, i: (row(b, i), 0)),
        out_shape=jax.ShapeDtypeStruct((m, qa_w), BF16),
        scratch_shapes=[pltpu.VMEM((A_KV_HEADS, A_GROUP * Q_BLOCK, 2 * Q_BLOCK), F32),
                        pltpu.VMEM((A_KV_HEADS, A_GROUP * Q_BLOCK, 2 * Q_BLOCK), BF16)]
        + [pltpu.VMEM((A_KV_HEADS, A_GROUP * Q_BLOCK, PAIR), F32)] * 4,
        compiler_params=_cparams("parallel", "parallel"),
        name="swa",
    )(qkv, qkv, qkv, qkv, qkv, bias_a, sinks)


def _compress_kernel(r_ref, pe_ref, w1_ref, w2_ref, o_ref):
    half = NSA_CMP_STRIDE * HEAD_DIM
    r = r_ref[0, 0]
    xa = (r + pe_ref[0, :, :half]).astype(BF16)
    xb = (r + pe_ref[0, :, half:]).astype(BF16)
    a = jnp.dot(xa, w1_ref[0, :half, :], preferred_element_type=F32)
    b = jnp.dot(xb, w1_ref[0, half:, :], preferred_element_type=F32)
    n = r.shape[0]
    hid = jax.nn.gelu(a + pltpu.roll(b, n - 1, 0))
    o_ref[0, 0] = jnp.dot(hid.astype(BF16), w2_ref[0], preferred_element_type=F32).astype(o_ref.dtype)


def nsa_compress(r, pe, w1, w2dup):
    batch, _, n, width = r.shape
    return pl.pallas_call(
        _compress_kernel,
        grid=(batch, 2 * B_KV_HEADS),
        in_specs=[pl.BlockSpec((1, 1, n, width), lambda b, w: (b, w, 0, 0)),
                  pl.BlockSpec((1, 1, 2 * width), lambda b, w: (w // B_KV_HEADS, 0, 0)),
                  pl.BlockSpec((1, 2 * width, NSA_CMP_HIDDEN), lambda b, w: (w // B_KV_HEADS, 0, 0)),
                  pl.BlockSpec((1, NSA_CMP_HIDDEN, PAIR), lambda b, w: (w // B_KV_HEADS, 0, 0))],
        out_specs=pl.BlockSpec((1, 1, n, PAIR), lambda b, w: (b, w, 0, 0)),
        out_shape=jax.ShapeDtypeStruct((batch, 2 * B_KV_HEADS, n, PAIR), BF16),
        compiler_params=_cparams("parallel", "parallel"),
        name="nsa_compress",
    )(r, pe, w1, w2dup)


NSA_KEY_TILE = 512
NSA_STREAMS = 2
WIN_TILES = NSA_WINDOW // Q_BLOCK + 1
TAB_FAR = NEAR_TILES
TAB_WIN_EDGE = NEAR_TILES + 1
TAB_NONE = NEAR_TILES + 2
TAB_ENTRIES = NEAR_TILES + 3
UNSELECTED = -1.0e30


def _nsa_kernel(q_ref, kcm_ref, vcm_ref, ks_ref, vs_ref, kw_ref, vw_ref, gate_ref, bias_c_ref, tab_ref, ovt_ref,
                o_ref, val_ref, s_ref, p_ref, m_ref, l_ref, a_ref, pm_ref, acc_ref, *, n_cmp_pad):
    i = pl.program_id(2)
    rows = B_GROUP * Q_BLOCK
    srows = rows // NSA_STREAMS
    heads_per_stream = B_GROUP // NSA_STREAMS
    chunks_per_head = Q_BLOCK // ROW_CHUNK
    n_sel = val_ref.shape[0]
    sel_shift = int(math.log2(NSA_SEL_LEN))
    qs = _stack_pairs(q_ref, 0, B_GROUP // 2)

    qs_st = [qs[st * srows:(st + 1) * srows] for st in range(NSA_STREAMS)]

    s_c = _nt(qs, kcm_ref[0, 0]).reshape(B_GROUP, Q_BLOCK, n_cmp_pad) + bias_c_ref[...]
    m_c = jnp.max(s_c, axis=-1, keepdims=True)
    m_c = jnp.where(m_c == NEG_INF, 0.0, m_c)
    e_c = jnp.exp2(s_c - m_c)
    p_c = e_c / jnp.maximum(jnp.sum(e_c, axis=-1, keepdims=True), 1e-30)
    o_c = jnp.dot(p_c.reshape(rows, n_cmp_pad).astype(BF16), vcm_ref[0, 0], preferred_element_type=F32)

    p_sum = jnp.sum(p_c, axis=0)
    ovt = ovt_ref[...]
    imp = None
    for piece in _split3(p_sum):
        t = _nt(ovt, piece)
        imp = t if imp is None else imp + t
    blk = lax.broadcasted_iota(jnp.int32, (n_sel, Q_BLOCK), 0)
    qpos = lax.broadcasted_iota(jnp.int32, (n_sel, Q_BLOCK), 1) + i * Q_BLOCK
    cur = lax.shift_right_logical(qpos, sel_shift)
    forced = (blk == 0) | (blk == cur) | (blk == cur - 1)
    future = blk * NSA_SEL_LEN > qpos
    val = jnp.where(future, NEG_INF, jnp.where(forced, NSA_FORCE_SCORE, imp))
    val_ref[...] = val
    group = 8
    ranks = []
    for g0 in range(0, n_sel, group):
        val_g = val[g0:g0 + group]
        blk_g = blk[g0:g0 + group]
        rank_g = jnp.zeros((group, Q_BLOCK), F32)
        for s2 in range(n_sel):
            other = val_ref[s2:s2 + 1, :]
            if s2 < g0:
                ahead = other >= val_g
            elif s2 >= g0 + group:
                ahead = other > val_g
            else:
                ahead = ((blk_g > s2) & (other >= val_g)) | (other > val_g)
            rank_g = rank_g + jnp.where(ahead, 1.0, 0.0)
        ranks.append(rank_g)
    rank = jnp.concatenate(ranks, axis=0)
    sel_t = jnp.where((rank < float(NSA_TOP_N)) & (val > NEG_INF), 1.0, 0.0).astype(BF16)
    if n_sel < Q_BLOCK:
        sel_t = jnp.concatenate([sel_t, jnp.zeros((Q_BLOCK - n_sel, Q_BLOCK), BF16)], axis=0)
    eye = jnp.where(lax.broadcasted_iota(jnp.int32, (Q_BLOCK, Q_BLOCK), 0)
                    == lax.broadcasted_iota(jnp.int32, (Q_BLOCK, Q_BLOCK), 1), 1.0, 0.0).astype(BF16)
    unsel = jnp.where(_nt(eye, sel_t) > 0.5, 0.0, UNSELECTED).astype(BF16)
    unsel_rows = jnp.concatenate([unsel] * heads_per_stream, axis=0)
    qs_sel = [jnp.concatenate([qs_st[st], unsel_rows], axis=1) for st in range(NSA_STREAMS)]

    key_lane = lax.broadcasted_iota(jnp.int32, (NSA_KEY_TILE, PAIR), 1)
    key_blk = lax.shift_right_logical(lax.broadcasted_iota(jnp.int32, (NSA_KEY_TILE, PAIR), 0), sel_shift)
    lane_minus_blk = key_lane - key_blk

    sel_refs = (s_ref, p_ref, m_ref, l_ref, a_ref, pm_ref, acc_ref)

    def reset(refs):
        _, _, m_r, l_r, _, _, acc_r = refs
        m_r[...] = jnp.full(m_r.shape, NEG_INF, F32)
        l_r[...] = jnp.zeros(l_r.shape, F32)
        acc_r[...] = jnp.zeros(acc_r.shape, F32)

    def soft_pv(refs, st, tab_idx, v):
        s_r, p_r, m_r, l_r, a_r, pm_r, acc_r = refs
        width = len(tab_idx) * Q_BLOCK

        def raw(c, u):
            return s_r[st, c * ROW_CHUNK:(c + 1) * ROW_CHUNK, u * Q_BLOCK:(u + 1) * Q_BLOCK]

        def biased(c, u):
            head = st * heads_per_stream + c // chunks_per_head
            q0 = (c % chunks_per_head) * ROW_CHUNK
            return raw(c, u) + tab_ref[0, tab_idx[u], head, q0:q0 + ROW_CHUNK, :]

        def keep(c, u, s):
            s_r[st, c * ROW_CHUNK:(c + 1) * ROW_CHUNK, u * Q_BLOCK:(u + 1) * Q_BLOCK] = s

        _softmax_tile(biased, srows // ROW_CHUNK, ROW_CHUNK, len(tab_idx), m_r.at[st], l_r.at[st],
                      a_r.at[st], pm_r.at[st], p_r.at[st], keep=keep, kept_tiles=raw)
        acc_r[st] = a_r[st] * acc_r[st] + jnp.dot(p_r[st, :, :width], v, preferred_element_type=F32)

    def finish(refs):
        _, _, _, l_r, _, _, acc_r = refs
        return jnp.concatenate(
            [acc_r[st] / jnp.maximum(jnp.sum(l_r[st], axis=-1, keepdims=True), 1e-30)
             for st in range(NSA_STREAMS)], axis=0)

    blocks_per_tile = NSA_KEY_TILE // NSA_SEL_LEN
    lane_tiles_per_tile = NSA_KEY_TILE // Q_BLOCK

    def sel_scores(st, jt):
        off = pl.multiple_of(jt * NSA_KEY_TILE, NSA_KEY_TILE)
        block_onehot = jnp.where(lane_minus_blk == jt * blocks_per_tile, 1.0, 0.0).astype(BF16)
        k_aug = jnp.concatenate([ks_ref[pl.ds(off, NSA_KEY_TILE), :], block_onehot], axis=1)
        s_ref[st, :, :NSA_KEY_TILE] = _nt(qs_sel[st], k_aug)

    def sel_tile(jt, last):
        v = vs_ref[pl.ds(pl.multiple_of(jt * NSA_KEY_TILE, NSA_KEY_TILE), NSA_KEY_TILE), :]
        tab_idx = []
        for u in range(lane_tiles_per_tile):
            d = i - (jt * lane_tiles_per_tile + u)
            tab_idx.append(jnp.where(d < 0, TAB_NONE, jnp.minimum(d, TAB_FAR)))
        for st in range(NSA_STREAMS):
            soft_pv(sel_refs, st, tab_idx, v)
            if not last:
                sel_scores(st, jt + 1)

    def sel_body(jt, carry):
        sel_tile(jt, False)
        return carry

    n_tiles = lax.shift_right_logical(i + lane_tiles_per_tile, int(math.log2(lane_tiles_per_tile)))
    reset(sel_refs)
    for st in range(NSA_STREAMS):
        sel_scores(st, 0)
    lax.fori_loop(0, n_tiles - 1, sel_body, 0)
    sel_tile(n_tiles - 1, True)
    o_s = finish(sel_refs)

    first_blk = jnp.maximum(i - (WIN_TILES - 1), 0)
    off = pl.multiple_of(first_blk * Q_BLOCK, Q_BLOCK)
    k_win = kw_ref[pl.ds(off, WIN_TILES * Q_BLOCK), :]
    v_win = vw_ref[pl.ds(off, WIN_TILES * Q_BLOCK), :]
    win_idx = []
    for u in range(WIN_TILES):
        d = i - (first_blk + u)
        win_idx.append(jnp.where(d < 0, TAB_NONE, jnp.where(d == WIN_TILES - 1, TAB_WIN_EDGE, d)))
    reset(sel_refs)
    for st in range(NSA_STREAMS):
        s_ref[st] = _nt(qs_st[st], k_win)
    for st in range(NSA_STREAMS):
        soft_pv(sel_refs, st, win_idx, v_win)
    o_w = finish(sel_refs)

    gates = jax.nn.sigmoid(gate_ref[...])

    def gate_col(br):
        return jnp.concatenate([gates[:, 3 * r + br:3 * r + br + 1] for r in range(B_GROUP)], axis=0)

    o = gate_col(0) * o_c + gate_col(1) * o_s + gate_col(2) * o_w
    for p, blk_out in enumerate(_unstack_pairs(o, B_GROUP // 2, Q_BLOCK)):
        o_ref[:, p * PAIR:(p + 1) * PAIR] = blk_out.astype(o_ref.dtype)


def nsa_attention(qkv, cmp_kv, tail, bias_c, bias_tab, ovt, batch, seq):
    nb = seq // Q_BLOCK
    m = batch * seq
    n_cmp_pad = seq // NSA_CMP_STRIDE
    n_sel = seq // NSA_SEL_LEN
    assert n_sel <= Q_BLOCK and seq % NSA_KEY_TILE == 0 and seq >= WIN_TILES * Q_BLOCK
    assert n_cmp_pad % Q_BLOCK == 0 and n_cmp_pad <= NSA_KEY_TILE
    grp_w = B_GROUP * HEAD_DIM
    q_blk0 = (A_Q_HEADS * HEAD_DIM + 2 * A_KV_HEADS * PAIR) // grp_w
    kv_blk0 = (A_Q_HEADS * HEAD_DIM + 2 * A_KV_HEADS * PAIR + B_Q_HEADS * HEAD_DIM) // PAIR
    kv_spec = lambda t: pl.BlockSpec((seq, PAIR), lambda b, g, i: (b, kv_blk0 + t * B_KV_HEADS + g))
    srows = B_GROUP * Q_BLOCK // NSA_STREAMS

    def branch_scratch(width):
        return ([pltpu.VMEM((NSA_STREAMS, srows, width), F32), pltpu.VMEM((NSA_STREAMS, srows, width), BF16)]
                + [pltpu.VMEM((NSA_STREAMS, srows, PAIR), F32)] * 5)

    return pl.pallas_call(
        functools.partial(_nsa_kernel, n_cmp_pad=n_cmp_pad),
        grid=(batch, B_KV_HEADS, nb),
        in_specs=[pl.BlockSpec((Q_BLOCK, grp_w), lambda b, g, i: (b * nb + i, q_blk0 + g)),
                  pl.BlockSpec((1, 1, n_cmp_pad, PAIR), lambda b, g, i: (b, g, 0, 0)),
                  pl.BlockSpec((1, 1, n_cmp_pad, PAIR), lambda b, g, i: (b, B_KV_HEADS + g, 0, 0)),
                  kv_spec(0), kv_spec(1), kv_spec(2), kv_spec(3),
                  pl.BlockSpec((Q_BLOCK, PAIR), lambda b, g, i: (b * nb + i, 2 + g)),
                  pl.BlockSpec((B_GROUP, None, Q_BLOCK, n_cmp_pad), lambda b, g, i: (g, i, 0, 0)),
                  pl.BlockSpec((1, TAB_ENTRIES, B_GROUP, Q_BLOCK, Q_BLOCK), lambda b, g, i: (g, 0, 0, 0, 0)),
                  pl.BlockSpec((n_sel, n_cmp_pad), lambda b, g, i: (0, 0))],
        out_specs=pl.BlockSpec((Q_BLOCK, grp_w), lambda b, g, i: (b * nb + i, g)),
        out_shape=jax.ShapeDtypeStruct((m, B_Q_HEADS * HEAD_DIM), BF16),
        scratch_shapes=[pltpu.VMEM((n_sel, Q_BLOCK), F32)]
        + branch_scratch(max(NSA_KEY_TILE, WIN_TILES * Q_BLOCK)),
        compiler_params=_cparams("parallel", "parallel", "arbitrary"),
        name="nsa",
    )(qkv, cmp_kv, cmp_kv, qkv, qkv, qkv, qkv, tail, bias_c, bias_tab, ovt)


DECAY_STEP = 512
FOX_SPLIT = 1


def _decay_kernel(f_ref, fb_ref, place_ref, o_ref, carry_ref):
    r_i = lax.broadcasted_iota(jnp.int32, (Q_BLOCK, Q_BLOCK), 0)
    c_i = lax.broadcasted_iota(jnp.int32, (Q_BLOCK, Q_BLOCK), 1)
    tri = jnp.where(c_i <= r_i, 1.0, 0.0).astype(BF16)

    @pl.when(pl.program_id(1) == 0)
    def _():
        carry_ref[...] = jnp.zeros_like(carry_ref)

    carry = carry_ref[...]
    for r in range(DECAY_STEP // Q_BLOCK):
        x = f_ref[r * Q_BLOCK:(r + 1) * Q_BLOCK, :] + fb_ref[...]
        ls = jax.nn.log_sigmoid(x)
        cs = jnp.broadcast_to(carry, ls.shape)
        for piece in _split3(ls):
            cs = cs + jnp.dot(tri, piece, preferred_element_type=F32)
        feat = None
        for n, piece in enumerate(_split3(cs * LOG2E)):
            term = jnp.dot(piece, place_ref[n], preferred_element_type=F32)
            feat = term if feat is None else feat + term
        o_ref[r * Q_BLOCK:(r + 1) * Q_BLOCK, :] = feat.astype(o_ref.dtype)
        carry = cs[Q_BLOCK - 1:Q_BLOCK, :]
    carry_ref[...] = carry


def _decay_placement():
    place = np.zeros((3, PAIR, (C_HEADS // 2) * PAIR), np.float32)
    for h in range(C_HEADS):
        for n in range(3):
            place[n, h, (h // 2) * PAIR + 3 * (h % 2) + n] = -1.0
    return jnp.asarray(place, dtype=BF16)


def fox_decay(f_tail, f_bias, batch, seq):
    steps = seq // DECAY_STEP
    width = (C_HEADS // 2) * PAIR
    return pl.pallas_call(
        _decay_kernel,
        grid=(batch, steps),
        in_specs=[pl.BlockSpec((DECAY_STEP, PAIR), lambda b, s: (b * steps + s, 0)),
                  pl.BlockSpec((1, PAIR), lambda b, s: (0, 0)),
                  pl.BlockSpec((3, PAIR, width), lambda b, s: (0, 0, 0))],
        out_specs=pl.BlockSpec((DECAY_STEP, width), lambda b, s: (b * steps + s, 0)),
        out_shape=jax.ShapeDtypeStruct((batch * seq, width), BF16),
        scratch_shapes=[pltpu.VMEM((1, PAIR), F32)],
        compiler_params=_cparams("parallel", "arbitrary"),
        name="fox_decay",
    )(f_tail, f_bias, _decay_placement())


def _softmax_tile(lane_tiles, n_chunks, rc, n_lane_tiles, m_ref, l_ref, a_ref, pm_ref, p_ref, keep=None,
                  kept_tiles=None):
    second = lane_tiles if kept_tiles is None else kept_tiles
    for c in range(n_chunks):
        pm = None
        for u in range(n_lane_tiles):
            s = lane_tiles(c, u)
            if s is None:
                continue
            if keep is not None:
                keep(c, u, s)
            pm = s if pm is None else jnp.maximum(pm, s)
        pm_ref[c * rc:(c + 1) * rc, :] = pm
    m_old = m_ref[...]
    m_new = jnp.maximum(m_old, jnp.max(pm_ref[...], axis=-1, keepdims=True))
    a_ref[...] = jnp.exp2(m_old - m_new)
    m_ref[...] = m_new
    for c in range(n_chunks):
        rows = slice(c * rc, (c + 1) * rc)
        mb = m_ref[rows, :]
        psum = None
        for u in range(n_lane_tiles):
            s = second(c, u)
            if s is None:
                p_ref[rows, u * Q_BLOCK:(u + 1) * Q_BLOCK] = jnp.zeros((rc, Q_BLOCK), BF16)
                continue
            p = jnp.exp2(s - mb)
            p_ref[rows, u * Q_BLOCK:(u + 1) * Q_BLOCK] = p.astype(BF16)
            psum = p if psum is None else psum + p
        l_ref[rows, :] = a_ref[rows, :] * l_ref[rows, :] + psum


def _fox_kernel(q_ref, k_ref, v_ref, e_ref, o_ref, s_ref, p_ref, m_ref, l_ref, a_ref, pm_ref, acc_ref, *, tq):
    t = pl.program_id(2)
    kw = tq
    n_lane_tiles = kw // Q_BLOCK
    srows = tq // FOX_SPLIT
    n_chunks = srows // ROW_CHUNK
    n_streams = 2 * FOX_SPLIT
    q = q_ref[...]
    lane = lax.broadcasted_iota(jnp.int32, (tq, PAIR), 1)
    lo = lane < HEAD_DIM
    zero = jnp.zeros_like(q)
    q_aug = [jnp.concatenate([jnp.where(lo if hh == 0 else ~lo, q, zero),
                              jnp.where((lane >= 3 * hh) & (lane < 3 * hh + 3), 1.0, 0.0).astype(BF16)], axis=1)
             for hh in range(2)]
    qs = [q_aug[st // FOX_SPLIT][(st % FOX_SPLIT) * srows:(st % FOX_SPLIT + 1) * srows] for st in range(n_streams)]
    m_ref[...] = jnp.full(m_ref.shape, NEG_INF, F32)
    l_ref[...] = jnp.zeros(l_ref.shape, F32)
    acc_ref[...] = jnp.zeros(acc_ref.shape, F32)
    col_minus_row = (lax.broadcasted_iota(jnp.int32, (ROW_CHUNK, Q_BLOCK), 1)
                     - lax.broadcasted_iota(jnp.int32, (ROW_CHUNK, Q_BLOCK), 0))

    def scores(st, j):
        off = pl.multiple_of(j * kw, kw)
        k_aug = jnp.concatenate([k_ref[pl.ds(off, kw), :], e_ref[pl.ds(off, kw), :]], axis=1)
        s_ref[st] = _nt(qs[st], k_aug)

    def tile(j, diagonal):
        v = v_ref[pl.ds(pl.multiple_of(j * kw, kw), kw), :]
        for st in range(n_streams):
            row0 = (st % FOX_SPLIT) * srows

            def lane_tiles(c, u):
                first_row, first_key = row0 + c * ROW_CHUNK, u * Q_BLOCK
                if diagonal and first_key > first_row + ROW_CHUNK - 1:
                    return None
                s = s_ref[st, c * ROW_CHUNK:(c + 1) * ROW_CHUNK, u * Q_BLOCK:(u + 1) * Q_BLOCK]
                if diagonal and first_key + Q_BLOCK - 1 > first_row:
                    s = jnp.where(col_minus_row <= first_row - first_key, s, NEG_INF)
                return s

            _softmax_tile(lane_tiles, n_chunks, ROW_CHUNK, n_lane_tiles, m_ref.at[st], l_ref.at[st], a_ref.at[st],
                          pm_ref.at[st], p_ref.at[st])
            acc_ref[st] = a_ref[st] * acc_ref[st] + jnp.dot(p_ref[st], v, preferred_element_type=F32)
            if not diagonal:
                scores(st, j + 1)

    def body(j, carry):
        tile(j, False)
        return carry

    for st in range(n_streams):
        scores(st, 0)
    lax.fori_loop(0, t, body, 0)
    tile(t, True)
    o = [jnp.concatenate([acc_ref[st] / jnp.maximum(jnp.sum(l_ref[st], axis=-1, keepdims=True), 1e-30)
                          for st in range(hh * FOX_SPLIT, (hh + 1) * FOX_SPLIT)], axis=0) for hh in range(2)]
    o_ref[...] = jnp.where(lo, o[0], o[1]).astype(o_ref.dtype)


def fox_attention(qkv, decay, batch, seq, tq=512):
    m = batch * seq
    n_pairs = C_HEADS // 2
    nt = seq // tq
    return pl.pallas_call(
        functools.partial(_fox_kernel, tq=tq),
        grid=(batch, n_pairs, nt),
        in_specs=[pl.BlockSpec((tq, PAIR), lambda b, h, t: (b * nt + t, h)),
                  pl.BlockSpec((seq, PAIR), lambda b, h, t: (b, n_pairs + h)),
                  pl.BlockSpec((seq, PAIR), lambda b, h, t: (b, 2 * n_pairs + h)),
                  pl.BlockSpec((seq, PAIR), lambda b, h, t: (b, h))],
        out_specs=pl.BlockSpec((tq, PAIR), lambda b, h, t: (b * nt + t, h)),
        out_shape=jax.ShapeDtypeStruct((m, C_HEADS * HEAD_DIM), BF16),
        scratch_shapes=[pltpu.VMEM((2 * FOX_SPLIT, tq // FOX_SPLIT, tq), F32),
                        pltpu.VMEM((2 * FOX_SPLIT, tq // FOX_SPLIT, tq), BF16)]
        + [pltpu.VMEM((2 * FOX_SPLIT, tq // FOX_SPLIT, PAIR), F32)] * 5,
        compiler_params=_cparams("parallel", "parallel", "arbitrary"),
        name="fox",
    )(qkv, qkv, qkv, decay)


def _rel_bucket(dist):
    n = jnp.maximum(dist, 0)
    max_exact = REL_BUCKETS // 2
    nf = jnp.maximum(n, 1).astype(jnp.float32)
    large = max_exact + (jnp.log(nf / max_exact) / math.log(REL_MAX_DIST / max_exact)
                         * (REL_BUCKETS - max_exact)).astype(jnp.int32)
    return jnp.where(n < max_exact, n, jnp.minimum(large, REL_BUCKETS - 1))


def _lookup(table, idx):
    onehot = (idx[..., None] == jnp.arange(table.shape[0])).astype(F32)
    return jnp.einsum("hn,...n->h...", table.T, onehot, precision=lax.Precision.HIGHEST)


def _bias_tables(rel_bias, seq):
    nb = seq // Q_BLOCK
    n_cmp_pad = seq // NSA_CMP_STRIDE
    ql = jnp.arange(Q_BLOCK)
    d = jnp.arange(NEAR_TILES)
    dist_t = d[:, None, None] * Q_BLOCK + ql[None, :, None] - ql[None, None, :]
    tt = _lookup(rel_bias, _rel_bucket(dist_t))
    dist_a = ql[:, None] + Q_BLOCK - jnp.arange(2 * Q_BLOCK)[None, :]
    seen = (dist_a >= 0) & (dist_a < A_WINDOW)
    own = jnp.arange(2 * Q_BLOCK)[None, :] >= Q_BLOCK
    bias_a = jnp.concatenate([tt[:A_Q_HEADS, 1], tt[:A_Q_HEADS, 0]], axis=-1) * LOG2E
    bias_a = jnp.stack([jnp.where(seen, bias_a, NEG_INF), jnp.where(seen & own, bias_a, NEG_INF)])
    tb = tt[A_Q_HEADS:]
    upper = (ql[None, :] > ql[:, None])[None]
    far = jnp.broadcast_to(rel_bias[REL_BUCKETS - 1, A_Q_HEADS:][:, None, None], tb[:, 0].shape)
    entries = ([jnp.where(upper, NEG_INF, tb[:, 0])] + [tb[:, d] for d in range(1, NEAR_TILES)]
               + [far, jnp.where(upper, tb[:, WIN_TILES - 1], NEG_INF), jnp.full_like(far, NEG_INF)])
    bias_tab = jnp.stack(entries, axis=1).reshape(B_KV_HEADS, B_GROUP, TAB_ENTRIES, Q_BLOCK, Q_BLOCK)
    bias_tab = jnp.transpose(bias_tab, (0, 2, 1, 3, 4)) * LOG2E
    cend = jnp.arange(n_cmp_pad) * NSA_CMP_STRIDE + NSA_CMP_LEN - 1
    dist_c = (jnp.arange(nb)[:, None, None] * Q_BLOCK + ql[None, :, None]) - cend[None, None, :]
    bias_c = _lookup(rel_bias[:, A_Q_HEADS:] * LOG2E, _rel_bucket(dist_c))
    visible = (dist_c >= 0) & (jnp.arange(n_cmp_pad) < n_cmp_pad - 1)
    bias_c = jnp.where(visible[None], bias_c, NEG_INF)
    return bias_a, bias_tab, bias_c


def _overlap_t(seq):
    n_cmp_pad = seq // NSA_CMP_STRIDE
    n_sel = seq // NSA_SEL_LEN
    cstart = np.arange(n_cmp_pad) * NSA_CMP_STRIDE
    sstart = np.arange(n_sel) * NSA_SEL_LEN
    ov = (cstart[None, :] < sstart[:, None] + NSA_SEL_LEN) & (cstart[None, :] + NSA_CMP_LEN > sstart[:, None])
    ov[:, n_cmp_pad - 1] = False
    return jnp.asarray(ov.astype(np.float32), dtype=BF16)


def _dup(w, n_heads):
    d = w.shape[0]
    w = w.reshape(d, n_heads, 1, HEAD_DIM)
    return jnp.broadcast_to(w, (d, n_heads, 2, HEAD_DIM)).reshape(d, n_heads * PAIR)


def _even_weights(w_in):
    sizes = (A_Q_HEADS * HEAD_DIM, A_KV_HEADS * HEAD_DIM, A_KV_HEADS * HEAD_DIM, B_Q_HEADS * HEAD_DIM) \
        + (B_KV_HEADS * HEAD_DIM,) * 6 + (3 * B_Q_HEADS,)
    qa, ka, va, qb, kc, vc, ksl, vsl, kwn, vwn, gt = jnp.split(w_in, np.cumsum(sizes)[:-1].tolist(), axis=-1)
    log2_scale = ATTN_SCALE * LOG2E
    main = jnp.concatenate([qa * log2_scale, _dup(ka, A_KV_HEADS), _dup(va, A_KV_HEADS), qb * log2_scale,
                            _dup(ksl, B_KV_HEADS), _dup(vsl, B_KV_HEADS),
                            _dup(kwn, B_KV_HEADS), _dup(vwn, B_KV_HEADS)], axis=-1).astype(BF16)
    d = w_in.shape[0]
    per_group = 3 * B_GROUP
    gates = [jnp.pad(gt[:, g * per_group:(g + 1) * per_group], ((0, 0), (0, PAIR - per_group)))
             for g in range(B_KV_HEADS)]
    tail = jnp.concatenate([kc, vc] + gates, axis=-1).astype(BF16)
    return main, tail


def _odd_weights(w_in):
    c_mix = C_HEADS * HEAD_DIM
    col_scale = np.concatenate([np.full(c_mix, ATTN_SCALE * LOG2E, np.float32),
                                np.ones(w_in.shape[2] - c_mix, np.float32)])
    main = (w_in * col_scale).astype(BF16)
    tail = jnp.pad(w_in[:, :, 3 * c_mix:], ((0, 0), (0, 0), (0, PAIR - C_HEADS))).astype(BF16)
    return main, tail


def kernel(x, rel_bias, norm_mix, norm_ffn, norm_final, w_in_even, w_out_even, a_sinks, nsa_pe_k, nsa_pe_v,
           nsa_cmp_k_w1, nsa_cmp_k_w2, nsa_cmp_v_w1, nsa_cmp_v_w2, w_in_odd, w_out_odd, fox_fgate_b,
           w_ffn_up, w_ffn_down):
    batch, seq, d = x.shape
    depth = norm_mix.shape[0]
    m = batch * seq
    xf = x.reshape(m, d)
    bias_a, bias_tab, bias_c = _bias_tables(rel_bias, seq)
    ovt = _overlap_t(seq)
    n_cmp_pad = seq // NSA_CMP_STRIDE
    g_final = norm_final.reshape(1, d)
    w_up_bf16, w_down_bf16 = w_ffn_up.astype(BF16), w_ffn_down.astype(BF16)
    w_out_even_bf16, w_out_odd_bf16 = w_out_even.astype(BF16), w_out_odd.astype(BF16)
    w_odd_main, w_odd_tail = _odd_weights(w_in_odd)

    for layer in range(depth):
        g_mix = norm_mix[layer].reshape(1, d)
        if layer % 2 == 0:
            e = layer // 2
            w_main, w_tail = _even_weights(w_in_even[e])
            qkv, tail = norm_matmul(xf, g_mix, w_main[None], w_tail[None], 0)
            sinks = jnp.broadcast_to((a_sinks[e] * LOG2E).reshape(A_Q_HEADS, 1, 1), (A_Q_HEADS, 1, PAIR))
            a_out = swa_attention(qkv, bias_a, sinks, batch, seq)
            r = tail[:, :2 * B_KV_HEADS * HEAD_DIM].reshape(batch, seq, 2 * B_KV_HEADS, HEAD_DIM)
            r = jnp.transpose(r, (0, 2, 1, 3)).reshape(batch, 2 * B_KV_HEADS, n_cmp_pad, NSA_CMP_STRIDE * HEAD_DIM)
            pe = jnp.stack([nsa_pe_k[e].reshape(1, -1), nsa_pe_v[e].reshape(1, -1)])
            w1 = jnp.stack([nsa_cmp_k_w1[e], nsa_cmp_v_w1[e]]).astype(BF16)
            w2 = jnp.stack([_dup(nsa_cmp_k_w2[e], 1), _dup(nsa_cmp_v_w2[e], 1)]).astype(BF16)
            cmp_kv = nsa_compress(r, pe, w1, w2)
            b_out = nsa_attention(qkv, cmp_kv, tail, bias_c, bias_tab, ovt, batch, seq)
            mixed = (a_out, 0, b_out, 0, w_out_even_bf16, e)
        else:
            o = layer // 2
            qkv, f_tail = norm_matmul(xf, g_mix, w_odd_main, w_odd_tail, o)
            f_bias = jnp.pad(fox_fgate_b[o], (0, PAIR - C_HEADS)).reshape(1, PAIR)
            decay = fox_decay(f_tail, f_bias, batch, seq)
            c_out = fox_attention(qkv, decay, batch, seq)
            mixed = (c_out, 0, c_out, 1, w_out_odd_bf16, o)
        xf = mix_out_ffn(xf, *mixed, norm_ffn[layer].reshape(1, d), w_up_bf16, w_down_bf16, layer,
                         g_final, layer == depth - 1)
    return xf.reshape(batch, seq, d)
```
